```python
import math
import jax
import jax.numpy as jnp
from jax import lax
import numpy as np

D_MODEL = 1024
BATCH = 16
SEQ = 2048
DEPTH = 2
DEC_BATCH = 32
DEC_SEQ = 64
PAST_LEN = 2048

CHUNK = 64
EPS = 1e-6
D_INNER = 2 * D_MODEL
SSM_HEAD_DIM = 64
SSM_HEADS = D_INNER // SSM_HEAD_DIM
SSM_GROUPS = 4
SSM_HEADS_PER_GROUP = SSM_HEADS // SSM_GROUPS
D_STATE = 128
CONV_K = 4
CONV_CH = D_INNER + 2 * SSM_GROUPS * D_STATE
HEAD_DIM = 64
N_Q_HEADS = D_MODEL // HEAD_DIM
N_KV_HEADS = 4
Q_PER_KV = N_Q_HEADS // N_KV_HEADS
WINDOW = 128
WIN_CHUNKS = WINDOW // CHUNK
N_BUCKETS = 32
MAX_DISTANCE = 128
D_FF = math.ceil(8 * D_MODEL / 3 / 256) * 256
IN_SPLITS = (
    D_INNER,
    D_INNER + CONV_CH,
    D_INNER + CONV_CH + SSM_HEADS,
    D_INNER + CONV_CH + SSM_HEADS + N_Q_HEADS * HEAD_DIM,
    D_INNER + CONV_CH + SSM_HEADS + N_Q_HEADS * HEAD_DIM + N_KV_HEADS * HEAD_DIM,
    D_INNER + CONV_CH + SSM_HEADS + N_Q_HEADS * HEAD_DIM + 2 * N_KV_HEADS * HEAD_DIM,
    D_INNER + CONV_CH + SSM_HEADS + N_Q_HEADS * HEAD_DIM + 2 * N_KV_HEADS * HEAD_DIM + D_MODEL,
)
IN_COLS = IN_SPLITS[-1] + D_MODEL

kernel_name = "hybrid_ssd_swa_stream_step"


def rmsnorm(x, g):
    xf = x.astype(jnp.float32)
    y = xf * lax.rsqrt(jnp.mean(xf * xf, axis=-1, keepdims=True) + EPS)
    return (y * g.astype(jnp.float32)).astype(x.dtype)


def t5_bucket(rel):
    n = -rel
    half = N_BUCKETS // 2
    max_exact = half // 2
    ret = jnp.where(n < 0, half, 0)
    n = jnp.abs(n)
    nf = jnp.maximum(n, 1).astype(jnp.float32)
    large = max_exact + (jnp.log(nf / max_exact) / math.log(MAX_DISTANCE / max_exact)
                         * (half - max_exact)).astype(jnp.int32)
    large = jnp.minimum(large, half - 1)
    return ret + jnp.where(n < max_exact, n, large)


def causal_conv(u, prev, w, b):
    full = jnp.concatenate([prev.astype(u.dtype), u], axis=1)
    L = u.shape[1]
    out = full[:, 0:L] * w[0]
    for k in range(1, CONV_K):
        out = out + full[:, k:k + L] * w[k]
    return out + b, full[:, -(CONV_K - 1):]


def ssd_scan(xh, dt, a, bm, cm, state0):
    b, L = xh.shape[:2]
    lc = min(CHUNK, L)
    nc = L // lc

    def chunks(t):
        return jnp.swapaxes(t.reshape((b, nc, lc) + t.shape[2:]), 0, 1)

    xs = (chunks(xh.astype(jnp.float32) * dt[..., None]), chunks(dt * a),
          chunks(bm.astype(jnp.float32)), chunks(cm.astype(jnp.float32)))
    causal = jnp.tril(jnp.ones((lc, lc), dtype=bool))

    def step(state, inp):
        xdt, da, bc, cc = inp
        acum = jnp.cumsum(da, axis=1)
        seg = acum[:, :, None, :] - acum[:, None, :, :]
        decay = jnp.exp(jnp.where(causal[None, :, :, None], seg, -jnp.inf))
        decay = decay.reshape(b, lc, lc, SSM_GROUPS, SSM_HEADS_PER_GROUP)
        cb = jnp.einsum("blgn,bsgn->blsg", cc, bc)
        xg = xdt.reshape(b, lc, SSM_GROUPS, SSM_HEADS_PER_GROUP, SSM_HEAD_DIM)
        y_in = jnp.einsum("blsg,blsgh,bsghp->blghp", cb, decay, xg)
        sg = state.reshape(b, SSM_GROUPS, SSM_HEADS_PER_GROUP, SSM_HEAD_DIM, D_STATE)
        y_past = jnp.einsum("blgn,bghpn->blghp", cc, sg) * jnp.exp(acum).reshape(
            b, lc, SSM_GROUPS, SSM_HEADS_PER_GROUP)[..., None]
        tail = jnp.exp(acum[:, -1:] - acum).reshape(b, lc, SSM_GROUPS, SSM_HEADS_PER_GROUP)
        new = sg * jnp.exp(acum[:, -1]).reshape(b, SSM_GROUPS, SSM_HEADS_PER_GROUP)[..., None, None] \
            + jnp.einsum("blgn,blgh,blghp->bghpn", bc, tail, xg)
        return new.reshape(b, SSM_HEADS, SSM_HEAD_DIM, D_STATE), y_in + y_past

    final, ys = lax.scan(step, state0.astype(jnp.float32), xs)
    y = jnp.swapaxes(ys, 0, 1).reshape(b, L, SSM_HEADS, SSM_HEAD_DIM)
    return y, final


def band_rows(past, new, nc):
    rows = jnp.concatenate([past.astype(new.dtype), new], axis=1)
    if nc == 1:
        return rows[:, None]
    rc = rows.reshape((rows.shape[0], WIN_CHUNKS + nc, CHUNK) + rows.shape[2:])
    return jnp.concatenate([rc[:, m:m + nc] for m in range(WIN_CHUNKS + 1)], axis=2)


def sliding_attention(q, k, v, k_past, v_past, past_valid, q_norm_g, k_norm_g, sinks, rel_bias):
    b, L = q.shape[:2]
    lq = min(CHUNK, L)
    nc = L // lq
    lk = WINDOW + lq
    q = rmsnorm(q.reshape(b, L, N_KV_HEADS, Q_PER_KV, HEAD_DIM), q_norm_g)
    k = rmsnorm(k.reshape(b, L, N_KV_HEADS, HEAD_DIM), k_norm_g)
    v = v.reshape(b, L, N_KV_HEADS, HEAD_DIM)
    kb = band_rows(k_past, k, nc)
    vb = band_rows(v_past, v, nc)
    key_pos = jnp.arange(nc)[:, None] * lq + jnp.arange(lk)[None, :] - WINDOW
    valid = jnp.logical_or(key_pos >= 0, past_valid)
    rel = jnp.arange(lk)[None, :] - WINDOW - jnp.arange(lq)[:, None]
    bias = jnp.transpose(rel_bias[t5_bucket(rel)], (2, 0, 1)).reshape(
        N_KV_HEADS, Q_PER_KV, lq, lk).astype(jnp.float32)
    qb = q.reshape(b, nc, lq, N_KV_HEADS, Q_PER_KV, HEAD_DIM)
    s = jnp.einsum("bnqkgd,bnskd->bnkgqs", qb, kb).astype(jnp.float32) * (HEAD_DIM ** -0.5) + bias
    s = jnp.where(valid[None, :, None, None, None, :], s, -jnp.inf)
    sink = jnp.broadcast_to(sinks.astype(jnp.float32).reshape(1, 1, N_KV_HEADS, Q_PER_KV, 1, 1),
                            s.shape[:-1] + (1,))
    p = jax.nn.softmax(jnp.concatenate([s, sink], axis=-1), axis=-1)[..., :-1]
    o = jnp.einsum("bnkgqs,bnskd->bnqkgd", p.astype(v.dtype), vb)
    return o.reshape(b, L, N_Q_HEADS * HEAD_DIM), k, v


def trunk_layer(x, c, conv_prev, ssm_prev, k_past, v_past, past_valid,
                ada_w, ada_b, norm_mix_g, norm_ffn_g, w_in, conv_w, conv_b, dt_bias, a_log, d_skip,
                ssm_norm_g, q_norm_g, k_norm_g, sinks, rel_bias, w_br_ssm, w_br_attn, w_out,
                w_gate_up, w_down):
    b, L, _ = x.shape
    mod = jax.nn.silu(c) @ ada_w + ada_b
    sh_m, sc_m, gt_m, sh_f, sc_f, gt_f = jnp.split(mod[:, None, :], 6, axis=-1)
    h = rmsnorm(x, norm_mix_g) * (1 + sc_m) + sh_m
    z, xbc, dt_raw, q, k, v, g_ssm, g_attn = jnp.split(h @ w_in, IN_SPLITS, axis=-1)
    xbc, conv_new = causal_conv(xbc, conv_prev, conv_w, conv_b)
    xbc = jax.nn.silu(xbc)
    xs, bm, cm = jnp.split(xbc, [D_INNER, D_INNER + SSM_GROUPS * D_STATE], axis=-1)
    dt = jax.nn.softplus(dt_raw.astype(jnp.float32) + dt_bias.astype(jnp.float32))
    a = -jnp.exp(a_log.astype(jnp.float32))
    xh = xs.reshape(b, L, SSM_HEADS, SSM_HEAD_DIM)
    y, ssm_new = ssd_scan(xh, dt, a, bm.reshape(b, L, SSM_GROUPS, D_STATE),
                          cm.reshape(b, L, SSM_GROUPS, D_STATE), ssm_prev)
    y = (y.astype(x.dtype) + xh * d_skip[:, None]).reshape(b, L, D_INNER) * jax.nn.silu(z)
    y = rmsnorm(y.reshape(b, L, SSM_GROUPS, D_INNER // SSM_GROUPS),
                ssm_norm_g.reshape(SSM_GROUPS, D_INNER // SSM_GROUPS)).reshape(b, L, D_INNER)
    o, k_new, v_new = sliding_attention(q, k, v, k_past, v_past, past_valid,
                                        q_norm_g, k_norm_g, sinks, rel_bias)
    mixed = jax.nn.sigmoid(g_ssm) * (y @ w_br_ssm) + jax.nn.sigmoid(g_attn) * (o @ w_br_attn)
    x = x + gt_m * (mixed @ w_out)
    h2 = rmsnorm(x, norm_ffn_g) * (1 + sc_f) + sh_f
    gate, up = jnp.split(h2 @ w_gate_up, 2, axis=-1)
    x = x + gt_f * ((jax.nn.silu(gate) * up) @ w_down)
    return x, conv_new, ssm_new.astype(x.dtype), k_new, v_new


def setup_inputs(seed: int = 0) -> dict:
    key = jax.random.key(seed)
    ks = jax.random.split(key, 32)
    f32 = jnp.float32

    def nrm(i, shape, scale):
        return jax.random.normal(ks[i], shape, f32) * scale

    u = jax.random.uniform(ks[16], (DEPTH, SSM_HEADS), f32)
    dt0 = jnp.exp(u * (math.log(0.1) - math.log(1e-3)) + math.log(1e-3))
    return {
        "x_prompt": nrm(0, (BATCH, SEQ, D_MODEL), 1.0),
        "x_sample": nrm(1, (DEC_BATCH, DEC_SEQ, D_MODEL), 1.0),
        "cache_k": nrm(2, (DEPTH, DEC_BATCH, WINDOW, N_KV_HEADS, HEAD_DIM), 1.0),
        "cache_v": nrm(3, (DEPTH, DEC_BATCH, WINDOW, N_KV_HEADS, HEAD_DIM), 1.0),
        "state_conv": nrm(4, (DEPTH, DEC_BATCH, CONV_K - 1, CONV_CH), 1.0),
        "state_ssm": nrm(5, (DEPTH, DEC_BATCH, SSM_HEADS, SSM_HEAD_DIM, D_STATE), 0.3),
        "c_prompt": nrm(6, (BATCH, D_MODEL), 1.0),
        "c_sample": nrm(7, (DEC_BATCH, D_MODEL), 1.0),
        "rel_bias": nrm(8, (N_BUCKETS, N_Q_HEADS), 0.2),
        "ada_w": nrm(9, (DEPTH, D_MODEL, 6 * D_MODEL), 0.5 * D_MODEL ** -0.5),
        "ada_b": nrm(10, (DEPTH, 6 * D_MODEL), 0.02),
        "norm_mix_g": 1.0 + nrm(11, (DEPTH, D_MODEL), 0.05),
        "norm_ffn_g": 1.0 + nrm(12, (DEPTH, D_MODEL), 0.05),
        "w_in": nrm(13, (DEPTH, D_MODEL, IN_COLS), D_MODEL ** -0.5),
        "conv_w": nrm(14, (DEPTH, CONV_K, CONV_CH), CONV_K ** -0.5),
        "conv_b": nrm(15, (DEPTH, CONV_CH), 0.02),
        "dt_bias": dt0 + jnp.log(-jnp.expm1(-dt0)),
        "a_log": jnp.log(jax.random.uniform(ks[17], (DEPTH, SSM_HEADS), f32, 1.0, 16.0)),
        "d_skip": 1.0 + nrm(18, (DEPTH, SSM_HEADS), 0.05),
        "ssm_norm_g": 1.0 + nrm(19, (DEPTH, D_INNER), 0.05),
        "q_norm_g": 1.0 + nrm(20, (DEPTH, HEAD_DIM), 0.05),
        "k_norm_g": 1.0 + nrm(21, (DEPTH, HEAD_DIM), 0.05),
        "sinks": nrm(22, (DEPTH, N_Q_HEADS), 0.5),
        "w_br_ssm": nrm(23, (DEPTH, D_INNER, D_MODEL), D_INNER ** -0.5),
        "w_br_attn": nrm(24, (DEPTH, N_Q_HEADS * HEAD_DIM, D_MODEL), (N_Q_HEADS * HEAD_DIM) ** -0.5),
        "w_out": nrm(25, (DEPTH, D_MODEL, D_MODEL), D_MODEL ** -0.5),
        "w_gate_up": nrm(26, (DEPTH, D_MODEL, 2 * D_FF), D_MODEL ** -0.5),
        "w_down": nrm(27, (DEPTH, D_FF, D_MODEL), D_FF ** -0.5),
    }


def reference(x_prompt, x_sample, cache_k, cache_v, state_conv, state_ssm, c_prompt, c_sample,
              rel_bias, ada_w, ada_b, norm_mix_g, norm_ffn_g, w_in, conv_w, conv_b, dt_bias, a_log,
              d_skip, ssm_norm_g, q_norm_g, k_norm_g, sinks, w_br_ssm, w_br_attn, w_out,
              w_gate_up, w_down):
    bp = x_prompt.shape[0]
    zero_conv = jnp.zeros((bp, CONV_K - 1, CONV_CH), x_prompt.dtype)
    zero_ssm = jnp.zeros((bp, SSM_HEADS, SSM_HEAD_DIM, D_STATE), jnp.float32)
    zero_kv = jnp.zeros((bp, WINDOW, N_KV_HEADS, HEAD_DIM), x_prompt.dtype)
    xp, xs = x_prompt, x_sample
    conv_p, conv_s, ssm_p, ssm_s, k_p, k_s, v_p, v_s = [], [], [], [], [], [], [], []
    for l in range(DEPTH):
        lp = (ada_w[l], ada_b[l], norm_mix_g[l], norm_ffn_g[l], w_in[l], conv_w[l], conv_b[l],
              dt_bias[l], a_log[l], d_skip[l], ssm_norm_g[l], q_norm_g[l], k_norm_g[l], sinks[l],
              rel_bias, w_br_ssm[l], w_br_attn[l], w_out[l], w_gate_up[l], w_down[l])
        xp, cvp, ssp, kp, vp = trunk_layer(xp, c_prompt, zero_conv, zero_ssm, zero_kv, zero_kv, False, *lp)
        xs, cvs, sss, ksn, vsn = trunk_layer(xs, c_sample, state_conv[l], state_ssm[l],
                                             cache_k[l], cache_v[l], True, *lp)
        conv_p.append(cvp)
        conv_s.append(cvs)
        ssm_p.append(ssp)
        ssm_s.append(sss)
        k_p.append(kp[:, -WINDOW:])
        v_p.append(vp[:, -WINDOW:])
        k_s.append(ksn)
        v_s.append(vsn)
    return (xp, xs, jnp.stack(conv_p), jnp.stack(conv_s), jnp.stack(ssm_p), jnp.stack(ssm_s),
            jnp.stack(k_p), jnp.stack(k_s), jnp.stack(v_p), jnp.stack(v_s))
```

```python
import functools
import math

import jax
import jax.numpy as jnp
from jax import lax
from jax.experimental import pallas as pl
from jax.experimental.pallas import tpu as pltpu

F32 = jnp.float32
BF16 = jnp.bfloat16

CHUNK = 64
EPS = 1e-6
SSM_GROUPS = 4
D_STATE = 128
SSM_HEAD_DIM = 64
CONV_K = 4
HEAD_DIM = 64
N_KV_HEADS = 4
WINDOW = 128
N_BUCKETS = 32
MAX_DISTANCE = 128

LANES = 128
SUBLANES = 8
KEYS = WINDOW + CHUNK
KEYS_PAD = 256
NEG = -1e30
VMEM_LIMIT = 56 * 1024 * 1024


def _dot(a, b):
    return jnp.dot(a, b, preferred_element_type=F32)


def _dot_nt(a, b):
    return lax.dot_general(a, b, (((1,), (1,)), ((), ())), preferred_element_type=F32)


def _dot_tn(a, b):
    return lax.dot_general(a, b, (((0,), (0,)), ((), ())), preferred_element_type=F32)


def _sigmoid(x):
    return 1.0 / (1.0 + jnp.exp(-x))


def _split2(x):
    hi = x.astype(BF16)
    lo = (x - hi.astype(F32)).astype(BF16)
    return hi, lo


def _rms_scale(x):
    return lax.rsqrt(jnp.mean(x * x, axis=-1, keepdims=True) + EPS)


def _ada_kernel(c_ref, w_ref, b_ref, o_ref):
    c = c_ref[...]
    s = (c * _sigmoid(c)).astype(BF16)
    o_ref[0] = _dot(s, w_ref[0].astype(BF16)) + b_ref[0]


def _ada_mod(c_all, ada_w, ada_b):
    depth, d, six_d = ada_w.shape
    n = c_all.shape[0]
    nblk = six_d // d
    return pl.pallas_call(
        _ada_kernel,
        grid=(depth, nblk),
        in_specs=[
            pl.BlockSpec((n, d), lambda l, j: (0, 0)),
            pl.BlockSpec((1, d, d), lambda l, j: (l, 0, j)),
            pl.BlockSpec((1, 1, d), lambda l, j: (l, 0, j)),
        ],
        out_specs=pl.BlockSpec((1, n, d), lambda l, j: (l, 0, j)),
        out_shape=jax.ShapeDtypeStruct((depth, n, six_d), F32),
        name="ada_mod",
    )(c_all, ada_w, ada_b.reshape(depth, 1, six_d))


def _bias_kernel(bkt_ref, rb_ref, o_ref, *, n_heads):
    bkt = bkt_ref[...]
    for h in range(n_heads):
        def body(b, acc, h=h):
            return jnp.where(bkt == b, rb_ref[b, h], acc)
        acc = lax.fori_loop(0, N_BUCKETS, body, jnp.full(bkt.shape, NEG, F32))
        o_ref[h // 2, :, (h % 2) * KEYS_PAD:(h % 2 + 1) * KEYS_PAD] = acc


def _t5_bucket(rel):
    n = -rel
    half = N_BUCKETS // 2
    max_exact = half // 2
    ret = jnp.where(n < 0, half, 0)
    n = jnp.abs(n)
    nf = jnp.maximum(n, 1).astype(F32)
    large = max_exact + (jnp.log(nf / max_exact) / math.log(MAX_DISTANCE / max_exact)
                         * (half - max_exact)).astype(jnp.int32)
    large = jnp.minimum(large, half - 1)
    return ret + jnp.where(n < max_exact, n, large)


def _pair_bias(rel_bias):
    n_heads = rel_bias.shape[1]
    rel = jnp.arange(KEYS)[None, :] - WINDOW - jnp.arange(CHUNK)[:, None]
    bkt = jnp.pad(_t5_bucket(rel).astype(jnp.int32), ((0, 0), (0, KEYS_PAD - KEYS)), constant_values=-1)
    return pl.pallas_call(
        functools.partial(_bias_kernel, n_heads=n_heads),
        in_specs=[pl.BlockSpec((CHUNK, KEYS_PAD), lambda: (0, 0)),
                  pl.BlockSpec(memory_space=pltpu.SMEM)],
        out_specs=pl.BlockSpec((n_heads // 2, CHUNK, 2 * KEYS_PAD), lambda: (0, 0, 0)),
        out_shape=jax.ShapeDtypeStruct((n_heads // 2, CHUNK, 2 * KEYS_PAD), F32),
        name="pair_bias",
    )(bkt, rel_bias)


def _mixer_kernel(*refs, nc, has_past, d_model, d_inner, n_heads, cols):
    t = nc * CHUNK
    conv_ch = d_inner + 2 * SSM_GROUPS * D_STATE
    gw = d_inner // SSM_GROUPS
    n_pairs = n_heads // 2
    kvw = N_KV_HEADS * LANES
    it = iter(refs)
    x_ref, mod_ref = next(it), next(it)
    if has_past:
        conv0_ref, ssm0_ref, k0_ref, v0_ref = next(it), next(it), next(it), next(it)
    (nmg_ref, wall_ref, convw_ref, convb_ref, dtb_ref, alog_ref, dskip_ref, ssmg_ref, qg_ref, kg_ref,
     sinks_ref, bias_ref, e_ref, wbs_ref, wba_ref, wo_ref) = [next(it) for _ in range(16)]
    xo_ref, convn_ref, ssmn_ref, kn_ref, vn_ref = [next(it) for _ in range(5)]
    (h_s, cb_s, z_s, xs_s, bc_s, dt_s, q_s, klo_s, khi_s, vlo_s, vhi_s, y_s, o_s, st_s) = list(it)

    s_idx = pl.program_id(1)
    n_steps = pl.num_programs(1)

    lane = lax.broadcasted_iota(jnp.int32, (CHUNK, LANES), 1)
    row = lax.broadcasted_iota(jnp.int32, (CHUNK, LANES), 0)
    low = lane < HEAD_DIM
    causal = row >= (lane & (CHUNK - 1))
    diag = row == (lane & (CHUNK - 1))
    trow = lax.broadcasted_iota(jnp.int32, (CHUNK, CHUNK), 0)
    tcol = lax.broadcasted_iota(jnp.int32, (CHUNK, CHUNK), 1)
    tril = jnp.where(trow >= tcol, 1.0, 0.0).astype(BF16)
    lowk = lax.broadcasted_iota(jnp.int32, (WINDOW, LANES), 1) < HEAD_DIM

    @pl.when(s_idx == 0)
    def _():
        if has_past:
            cb_s[0:SUBLANES, :] = conv0_ref[0]
            st_s[...] = ssm0_ref[0].T
            for src_ref, lo_s, hi_s in ((k0_ref, klo_s, khi_s), (v0_ref, vlo_s, vhi_s)):
                for j in range(N_KV_HEADS // 2):
                    nat = src_ref[0, :, j * LANES:(j + 1) * LANES]
                    rol = pltpu.roll(nat, HEAD_DIM, 1)
                    a, b = 2 * j * LANES, (2 * j + 1) * LANES
                    lo_s[0:WINDOW, a:a + LANES] = jnp.where(lowk, nat, 0.0).astype(BF16)
                    hi_s[0:WINDOW, a:a + LANES] = jnp.where(lowk, 0.0, rol).astype(BF16)
                    lo_s[0:WINDOW, b:b + LANES] = jnp.where(lowk, rol, 0.0).astype(BF16)
                    hi_s[0:WINDOW, b:b + LANES] = jnp.where(lowk, 0.0, nat).astype(BF16)
        else:
            cb_s[0:SUBLANES, :] = jnp.zeros((SUBLANES, conv_ch), F32)
            st_s[...] = jnp.zeros(st_s.shape, F32)
            for buf in (klo_s, khi_s, vlo_s, vhi_s):
                buf[0:WINDOW, :] = jnp.zeros((WINDOW, kvw), BF16)

    x = x_ref[0]
    shift, scale = mod_ref[0, 0:1, :], mod_ref[0, 1:2, :]
    h = (x * _rms_scale(x) * nmg_ref[...] * (1.0 + scale) + shift).astype(BF16)
    h_s[...] = h

    blk = 512
    for j in range(0, d_inner, blk):
        zz = _dot(h, wall_ref[:, cols["z"] + j:cols["z"] + j + blk])
        z_s[:, j:j + blk] = zz * _sigmoid(zz)
    for j in range(0, conv_ch, blk):
        cb_s[SUBLANES:SUBLANES + t, j:j + blk] = _dot(h, wall_ref[:, cols["xbc"] + j:cols["xbc"] + j + blk])
    dt_s[...] = _dot(h, wall_ref[:, cols["dt"]:cols["dt"] + LANES])

    for j in range(0, conv_ch, blk):
        acc = cb_s[SUBLANES:SUBLANES + t, j:j + blk] * convw_ref[CONV_K - 1:CONV_K, j:j + blk]
        for k in range(CONV_K - 1):
            o = SUBLANES - (CONV_K - 1) + k
            acc = acc + cb_s[o:o + t, j:j + blk] * convw_ref[k:k + 1, j:j + blk]
        acc = acc + convb_ref[:, j:j + blk]
        u = acc * _sigmoid(acc)
        if j < d_inner:
            xs_s[:, j:j + blk] = u
        else:
            bc_s[:, j - d_inner:j - d_inner + blk] = u

    qg = qg_ref[...]
    for p in range(n_pairs):
        qv = _dot(h, wall_ref[:, cols["q"] + p * LANES:cols["q"] + (p + 1) * LANES])
        q2 = qv * qv
        sa = jnp.sum(jnp.where(low[0:1], q2, 0.0), axis=-1, keepdims=True)
        sb = jnp.sum(jnp.where(low[0:1], 0.0, q2), axis=-1, keepdims=True)
        ms = jnp.where(low[0:1], sa, sb) * (1.0 / HEAD_DIM)
        q_s[:, p * LANES:(p + 1) * LANES] = (qv * lax.rsqrt(ms + EPS) * qg).astype(BF16)

    kg = kg_ref[...]
    knat, vnat = [], []
    for kv in range(N_KV_HEADS):
        c0 = kv * LANES
        kd = _dot(h, wall_ref[:, cols["k"] + c0:cols["k"] + c0 + LANES])
        ms = jnp.sum(kd * kd, axis=-1, keepdims=True) * (1.0 / LANES)
        kn = kd * lax.rsqrt(ms + EPS) * kg
        vd = _dot(h, wall_ref[:, cols["v"] + c0:cols["v"] + c0 + LANES])
        klo_s[WINDOW:WINDOW + t, c0:c0 + LANES] = jnp.where(low[0:1], kn, 0.0).astype(BF16)
        khi_s[WINDOW:WINDOW + t, c0:c0 + LANES] = jnp.where(low[0:1], 0.0, kn).astype(BF16)
        vlo_s[WINDOW:WINDOW + t, c0:c0 + LANES] = jnp.where(low[0:1], vd, 0.0).astype(BF16)
        vhi_s[WINDOW:WINDOW + t, c0:c0 + LANES] = jnp.where(low[0:1], 0.0, vd).astype(BF16)
        knat.append(kn)
        vnat.append(vd)

    dtb = dtb_ref[...]
    a_row = -jnp.exp(alog_ref[...])
    zpad = jnp.zeros((KEYS_PAD - KEYS, LANES), BF16)
    lane_s = lax.broadcasted_iota(jnp.int32, (CHUNK, 2 * KEYS_PAD), 1) & (KEYS_PAD - 1)

    for c in range(nc):
        r0 = c * CHUNK
        rows = slice(r0, r0 + CHUNK)

        xv = dt_s[rows, :] + dtb
        dt = jnp.maximum(xv, 0.0) + jnp.log1p(jnp.exp(-jnp.abs(xv)))
        da = dt * a_row
        d1 = da.astype(BF16)
        r1 = da - d1.astype(F32)
        d2 = r1.astype(BF16)
        d3 = (r1 - d2.astype(F32)).astype(BF16)
        acum = _dot(tril, d1) + _dot(tril, d2) + _dot(tril, d3)
        dt_hi, dt_lo = _split2(dt)
        ac_hi, ac_lo = _split2(acum)

        for g in range(SSM_GROUPS):
            gs = slice(g * gw, (g + 1) * gw)
            e_g = e_ref[:, gs]
            dte = _dot(dt_hi, e_g) + _dot(dt_lo, e_g)
            ae = _dot(ac_hi, e_g) + _dot(ac_lo, e_g)
            xs = xs_s[rows, gs]
            xdt = xs * dte
            bb = bc_s[rows, g * D_STATE:(g + 1) * D_STATE].astype(BF16)
            cc = bc_s[rows, (SSM_GROUPS + g) * D_STATE:(SSM_GROUPS + g + 1) * D_STATE].astype(BF16)
            cb2 = _dot_nt(cc, jnp.concatenate([bb, bb], axis=0))
            st = st_s[:, gs]
            y = _dot(cc, st.astype(BF16)) * jnp.exp(ae)
            ys = []
            for i in range(gw // LANES):
                ls = slice(i * LANES, (i + 1) * LANES)
                acol = ae[:, ls]
                arow = jnp.sum(jnp.where(diag, acol, 0.0), axis=0, keepdims=True)
                dec = jnp.exp(jnp.where(causal, acol - arow, NEG))
                mp = (cb2 * dec).astype(BF16)
                xp = xdt[:, ls]
                xbd = jnp.concatenate([jnp.where(low, xp, 0.0).astype(BF16),
                                       jnp.where(low, 0.0, xp).astype(BF16)], axis=0)
                ys.append(_dot(mp, xbd))
            y = y + jnp.concatenate(ys, axis=1) + xs * dskip_ref[:, gs]
            y = y * z_s[rows, gs]
            y = y * _rms_scale(y) * ssmg_ref[:, gs]
            y_s[rows, gs] = y.astype(BF16)
            ae_l = ae[CHUNK - 1:CHUNK, :]
            xt = (xdt * jnp.exp(ae_l - ae)).astype(BF16)
            st_s[:, gs] = st * jnp.exp(ae_l) + _dot_tn(bb, xt)

        if not has_past:
            thr = WINDOW - CHUNK * (s_idx * nc + c)
            valid = lane_s >= thr
        for kv in range(N_KV_HEADS):
            c0 = kv * LANES
            krows = slice(r0, r0 + KEYS)
            kbd = jnp.concatenate([klo_s[krows, c0:c0 + LANES], zpad, khi_s[krows, c0:c0 + LANES], zpad], axis=0)
            vbd = jnp.concatenate([vlo_s[krows, c0:c0 + LANES], zpad, vhi_s[krows, c0:c0 + LANES], zpad], axis=0)
            for pp in range(n_pairs // N_KV_HEADS):
                p = kv * (n_pairs // N_KV_HEADS) + pp
                sc = _dot_nt(q_s[rows, p * LANES:(p + 1) * LANES], kbd) + bias_ref[p]
                if not has_past:
                    sc = jnp.where(valid, sc, NEG)
                es, rs = [], []
                for hh in range(2):
                    sh = sc[:, hh * KEYS_PAD:(hh + 1) * KEYS_PAD]
                    sink = sinks_ref[2 * p + hh]
                    m = jnp.maximum(jnp.max(sh, axis=-1, keepdims=True), sink)
                    e = jnp.exp(sh - m)
                    den = jnp.sum(e, axis=-1, keepdims=True) + jnp.exp(sink - m)
                    es.append(e.astype(BF16))
                    rs.append(1.0 / den)
                o = _dot(jnp.concatenate(es, axis=1), vbd) * jnp.where(low, rs[0], rs[1])
                o_s[rows, p * LANES:(p + 1) * LANES] = o.astype(BF16)

    gate_m = mod_ref[0, 2:3, :]
    g_ssm = _sigmoid(_dot(h_s[...], wall_ref[:, cols["gs"]:cols["gs"] + d_model]))
    mixed = g_ssm * _dot(y_s[...], wbs_ref[...])
    g_att = _sigmoid(_dot(h_s[...], wall_ref[:, cols["ga"]:cols["ga"] + d_model]))
    mixed = mixed + g_att * _dot(o_s[...], wba_ref[...])
    xo_ref[0] = x_ref[0] + gate_m * _dot(mixed.astype(BF16), wo_ref[...])

    if not has_past:
        cb_s[0:SUBLANES, :] = cb_s[t:t + SUBLANES, :]
        for buf in (klo_s, khi_s, vlo_s, vhi_s):
            buf[0:WINDOW, :] = buf[t:t + WINDOW, :]

    @pl.when(s_idx == n_steps - 1)
    def _():
        convn_ref[0] = cb_s[t:t + SUBLANES, :]
        ssmn_ref[0] = st_s[...].T
        kr = kn_ref.shape[1]
        for j in range(N_KV_HEADS // 2):
            kn_ref[0, :, j * LANES:(j + 1) * LANES] = jnp.where(low[0:1], knat[2 * j], knat[2 * j + 1])[t - kr:t]
            vn_ref[0, :, j * LANES:(j + 1) * LANES] = jnp.where(low[0:1], vnat[2 * j], vnat[2 * j + 1])[t - kr:t]


def _const_spec(shape, single_buffer=False):
    idx = (0,) * len(shape)
    if single_buffer:
        return pl.BlockSpec(shape, lambda b, s: idx, pipeline_mode=pl.Buffered(1))
    return pl.BlockSpec(shape, lambda b, s: idx)


def _mixer_layer(x, mod, past, lw, pair_bias, e_mat, *, nc):
    bsz, seq, d_model = x.shape
    t = nc * CHUNK
    has_past = past is not None
    d_inner = lw["wbs"].shape[0]
    n_heads = lw["wba"].shape[0] // HEAD_DIM
    conv_ch = d_inner + 2 * SSM_GROUPS * D_STATE
    kvw = N_KV_HEADS * LANES
    kv_rows = seq if has_past else WINDOW
    assert seq % t == 0 and (has_past and seq == t or not has_past and t >= WINDOW)

    per_b = lambda *blk: pl.BlockSpec((1,) + blk, lambda b, s: (b,) + (0,) * len(blk))
    in_specs = [pl.BlockSpec((1, t, d_model), lambda b, s: (b, s, 0)), per_b(6, d_model)]
    args = [x, mod]
    if has_past:
        in_specs += [per_b(SUBLANES, conv_ch), per_b(d_inner, D_STATE),
                     per_b(WINDOW, N_KV_HEADS * HEAD_DIM), per_b(WINDOW, N_KV_HEADS * HEAD_DIM)]
        args += list(past)
    consts = [lw["nmg"], lw["wall"], lw["convw"], lw["convb"], lw["dtb"], lw["alog"], lw["dskip"], lw["ssmg"],
              lw["qg"], lw["kg"]]
    in_specs += [_const_spec(a.shape, single_buffer=a.size > (1 << 20)) for a in consts]
    in_specs.append(pl.BlockSpec(memory_space=pltpu.SMEM))
    tail = [pair_bias, e_mat, lw["wbs"], lw["wba"], lw["wo"]]
    in_specs += [_const_spec(a.shape, single_buffer=a.size > (1 << 18)) for a in tail]
    args += consts + [lw["sinks"]] + tail

    out_shape = (jax.ShapeDtypeStruct(x.shape, F32),
                 jax.ShapeDtypeStruct((bsz, SUBLANES, conv_ch), F32),
                 jax.ShapeDtypeStruct((bsz, d_inner, D_STATE), F32),
                 jax.ShapeDtypeStruct((bsz, kv_rows, N_KV_HEADS * HEAD_DIM), F32),
                 jax.ShapeDtypeStruct((bsz, kv_rows, N_KV_HEADS * HEAD_DIM), F32))
    out_specs = (pl.BlockSpec((1, t, d_model), lambda b, s: (b, s, 0)),
                 per_b(SUBLANES, conv_ch), per_b(d_inner, D_STATE),
                 per_b(kv_rows, N_KV_HEADS * HEAD_DIM), per_b(kv_rows, N_KV_HEADS * HEAD_DIM))
    scratch = [
        pltpu.VMEM((t, d_model), BF16),
        pltpu.VMEM((SUBLANES + t, conv_ch), F32),
        pltpu.VMEM((t, d_inner), F32),
        pltpu.VMEM((t, d_inner), F32),
        pltpu.VMEM((t, 2 * SSM_GROUPS * D_STATE), F32),
        pltpu.VMEM((t, LANES), F32),
        pltpu.VMEM((t, d_model), BF16),
        pltpu.VMEM((WINDOW + t, kvw), BF16),
        pltpu.VMEM((WINDOW + t, kvw), BF16),
        pltpu.VMEM((WINDOW + t, kvw), BF16),
        pltpu.VMEM((WINDOW + t, kvw), BF16),
        pltpu.VMEM((t, d_inner), BF16),
        pltpu.VMEM((t, d_model), BF16),
        pltpu.VMEM((D_STATE, d_inner), F32),
    ]
    kern = functools.partial(_mixer_kernel, nc=nc, has_past=has_past, d_model=d_model, d_inner=d_inner,
                             n_heads=n_heads, cols=lw["cols"])
    return pl.pallas_call(
        kern, grid=(bsz, seq // t), in_specs=in_specs, out_specs=out_specs, out_shape=out_shape,
        scratch_shapes=scratch,
        compiler_params=pltpu.CompilerParams(dimension_semantics=("arbitrary", "arbitrary"),
                                             vmem_limit_bytes=VMEM_LIMIT),
        name="mixer_past" if has_past else "mixer_prompt",
    )(*args)


def _ffn_kernel(x_ref, mod_ref, g_ref, wg_ref, wu_ref, wd_ref, o_ref):
    x = x_ref[0]
    shift, scale, gate_f = mod_ref[0, 3:4, :], mod_ref[0, 4:5, :], mod_ref[0, 5:6, :]
    h = (x * _rms_scale(x) * g_ref[...] * (1.0 + scale) + shift).astype(BF16)
    gate = _dot(h, wg_ref[...])
    up = _dot(h, wu_ref[...])
    act = (gate * _sigmoid(gate) * up).astype(BF16)
    o_ref[0] = x + gate_f * _dot(act, wd_ref[...])


def _ffn_layer(x, mod, lw, *, tm):
    bsz, seq, d_model = x.shape
    assert seq % tm == 0
    xspec = pl.BlockSpec((1, tm, d_model), lambda b, s: (b, s, 0))
    return pl.pallas_call(
        _ffn_kernel, grid=(bsz, seq // tm),
        in_specs=[xspec, pl.BlockSpec((1, 6, d_model), lambda b, s: (b, 0, 0)),
                  _const_spec(lw["nfg"].shape), _const_spec(lw["wg"].shape, True),
                  _const_spec(lw["wu"].shape, True), _const_spec(lw["wd"].shape, True)],
        out_specs=xspec, out_shape=jax.ShapeDtypeStruct(x.shape, F32),
        compiler_params=pltpu.CompilerParams(dimension_semantics=("arbitrary", "arbitrary"),
                                             vmem_limit_bytes=VMEM_LIMIT),
        name="ffn",
    )(x, mod, lw["nfg"], lw["wg"], lw["wu"], lw["wd"])


def _layer_weights(l, p):
    d_model = p["w_out"].shape[1]
    d_inner = p["w_br_ssm"].shape[1]
    n_ssm_heads = p["dt_bias"].shape[1]
    d_ff = p["w_down"].shape[1]
    nq = p["w_br_attn"].shape[1]
    nkv = N_KV_HEADS * HEAD_DIM
    conv_ch = d_inner + 2 * SSM_GROUPS * D_STATE
    wi = p["w_in"][l]
    bounds = [0, d_inner, d_inner + conv_ch, d_inner + conv_ch + n_ssm_heads]
    bounds += [bounds[-1] + nq, bounds[-1] + nq + nkv, bounds[-1] + nq + 2 * nkv]
    bounds += [bounds[-1] + d_model, bounds[-1] + 2 * d_model]
    z, xbc, dtc, q, k, v, gs, ga = [wi[:, a:b] for a, b in zip(bounds[:-1], bounds[1:])]
    dup = lambda w: jnp.concatenate([w.reshape(d_model, N_KV_HEADS, HEAD_DIM)] * 2, axis=-1).reshape(d_model, -1)
    pieces = [("z", z), ("xbc", xbc), ("dt", jnp.pad(dtc, ((0, 0), (0, LANES - n_ssm_heads)))),
              ("q", q), ("k", dup(k)), ("v", dup(v)), ("gs", gs), ("ga", ga)]
    cols, off = {}, 0
    for name, w in pieces:
        cols[name] = off
        off += w.shape[1]
    rep = lambda vec: jnp.repeat(vec, SSM_HEAD_DIM)[None, :]
    padl = lambda vec: jnp.pad(vec, (0, LANES - n_ssm_heads))[None, :]
    return dict(
        cols=cols,
        wall=jnp.concatenate([w for _, w in pieces], axis=1).astype(BF16),
        nmg=p["norm_mix_g"][l][None, :], nfg=p["norm_ffn_g"][l][None, :],
        convw=p["conv_w"][l], convb=p["conv_b"][l][None, :],
        dtb=padl(p["dt_bias"][l]), alog=padl(p["a_log"][l]), dskip=rep(p["d_skip"][l]),
        ssmg=p["ssm_norm_g"][l][None, :],
        qg=jnp.tile(p["q_norm_g"][l] * (HEAD_DIM ** -0.5), 2)[None, :], kg=jnp.tile(p["k_norm_g"][l], 2)[None, :],
        sinks=p["sinks"][l],
        wbs=p["w_br_ssm"][l].astype(BF16), wba=p["w_br_attn"][l].astype(BF16), wo=p["w_out"][l].astype(BF16),
        wg=p["w_gate_up"][l][:, :d_ff].astype(BF16), wu=p["w_gate_up"][l][:, d_ff:].astype(BF16),
        wd=p["w_down"][l].astype(BF16),
    )


def _expand_matrix(n_ssm_heads):
    r = jnp.arange(LANES)[:, None]
    c = jnp.arange(n_ssm_heads * SSM_HEAD_DIM)[None, :] // SSM_HEAD_DIM
    return (r == c).astype(BF16)


def kernel(x_prompt, x_sample, cache_k, cache_v, state_conv, state_ssm, c_prompt, c_sample, rel_bias, ada_w, ada_b, norm_mix_g, norm_ffn_g, w_in, conv_w, conv_b, dt_bias, a_log, d_skip, ssm_norm_g, q_norm_g, k_norm_g, sinks, w_br_ssm, w_br_attn, w_out, w_gate_up, w_down):
    p = dict(norm_mix_g=norm_mix_g, norm_ffn_g=norm_ffn_g, w_in=w_in, conv_w=conv_w, conv_b=conv_b,
             dt_bias=dt_bias, a_log=a_log, d_skip=d_skip, ssm_norm_g=ssm_norm_g, q_norm_g=q_norm_g,
             k_norm_g=k_norm_g, sinks=sinks, w_br_ssm=w_br_ssm, w_br_attn=w_br_attn, w_out=w_out,
             w_gate_up=w_gate_up, w_down=w_down)
    depth = w_in.shape[0]
    bp, seq_p, d_model = x_prompt.shape
    bs, seq_s, _ = x_sample.shape
    n_ssm_heads, ssm_p, d_state = state_ssm.shape[2:]
    d_inner = n_ssm_heads * ssm_p
    nkv = N_KV_HEADS * HEAD_DIM

    mod = _ada_mod(jnp.concatenate([c_prompt, c_sample], axis=0), ada_w, ada_b)
    mod = mod.reshape(depth, bp + bs, 6, d_model)
    pair_bias = _pair_bias(rel_bias)
    e_mat = _expand_matrix(n_ssm_heads)

    xp, xs = x_prompt, x_sample
    outs = [[] for _ in range(8)]
    for l in range(depth):
        lw = _layer_weights(l, p)
        mod_p, mod_s = mod[l, :bp], mod[l, bp:]
        past = (jnp.pad(state_conv[l], ((0, 0), (SUBLANES - (CONV_K - 1), 0), (0, 0))),
                state_ssm[l].reshape(bs, d_inner, d_state),
                cache_k[l].reshape(bs, WINDOW, nkv), cache_v[l].reshape(bs, WINDOW, nkv))
        xp, cvp, ssp, kp, vp = _mixer_layer(xp, mod_p, None, lw, pair_bias, e_mat, nc=2)
        xs, cvs, sss, ksn, vsn = _mixer_layer(xs, mod_s, past, lw, pair_bias, e_mat, nc=1)
        xp = _ffn_layer(xp, mod_p, lw, tm=min(512, seq_p))
        xs = _ffn_layer(xs, mod_s, lw, tm=seq_s)
        tail = SUBLANES - (CONV_K - 1)
        for lst, val in zip(outs, (
                cvp[:, tail:], cvs[:, tail:],
                ssp.reshape(bp, n_ssm_heads, ssm_p, d_state), sss.reshape(bs, n_ssm_heads, ssm_p, d_state),
                kp.reshape(bp, WINDOW, N_KV_HEADS, HEAD_DIM), ksn.reshape(bs, seq_s, N_KV_HEADS, HEAD_DIM),
                vp.reshape(bp, WINDOW, N_KV_HEADS, HEAD_DIM), vsn.reshape(bs, seq_s, N_KV_HEADS, HEAD_DIM))):
            lst.append(val)
    return (xp, xs) + tuple(jnp.stack(o) for o in outs)
```

```python
import functools
import math

import jax
import jax.numpy as jnp
from jax import lax
from jax.experimental import pallas as pl
from jax.experimental.pallas import tpu as pltpu

F32 = jnp.float32
BF16 = jnp.bfloat16

CHUNK = 64
EPS = 1e-6
SSM_GROUPS = 4
D_STATE = 128
SSM_HEAD_DIM = 64
CONV_K = 4
HEAD_DIM = 64
N_KV_HEADS = 4
WINDOW = 128
N_BUCKETS = 32
MAX_DISTANCE = 128

LANES = 128
SUBLANES = 8
KEYS = WINDOW + CHUNK
KEYS_PAD = 256
NEG = -1e30
VMEM_LIMIT = 56 * 1024 * 1024


def _dot(a, b):
    return jnp.dot(a, b, preferred_element_type=F32)


def _dot_nt(a, b):
    return lax.dot_general(a, b, (((1,), (1,)), ((), ())), preferred_element_type=F32)


def _dot_tn(a, b):
    return lax.dot_general(a, b, (((0,), (0,)), ((), ())), preferred_element_type=F32)


def _sigmoid(x):
    return 1.0 / (1.0 + jnp.exp(-x))


def _split2(x):
    hi = x.astype(BF16)
    lo = (x - hi.astype(F32)).astype(BF16)
    return hi, lo


def _rms_scale(x):
    return lax.rsqrt(jnp.mean(x * x, axis=-1, keepdims=True) + EPS)


def _ada_kernel(c_ref, w_ref, b_ref, o_ref):
    c = c_ref[...]
    s = (c * _sigmoid(c)).astype(BF16)
    o_ref[0] = _dot(s, w_ref[0].astype(BF16)) + b_ref[0]


def _ada_mod(c_all, ada_w, ada_b):
    depth, d, six_d = ada_w.shape
    n = c_all.shape[0]
    nblk = six_d // d
    return pl.pallas_call(
        _ada_kernel,
        grid=(depth, nblk),
        in_specs=[
            pl.BlockSpec((n, d), lambda l, j: (0, 0)),
            pl.BlockSpec((1, d, d), lambda l, j: (l, 0, j)),
            pl.BlockSpec((1, 1, d), lambda l, j: (l, 0, j)),
        ],
        out_specs=pl.BlockSpec((1, n, d), lambda l, j: (l, 0, j)),
        out_shape=jax.ShapeDtypeStruct((depth, n, six_d), F32),
        name="ada_mod",
    )(c_all, ada_w, ada_b.reshape(depth, 1, six_d))


def _bias_kernel(bkt_ref, rb_ref, o_ref, *, n_heads):
    bkt = bkt_ref[...]
    for h in range(n_heads):
        def body(b, acc, h=h):
            return jnp.where(bkt == b, rb_ref[b, h], acc)
        acc = lax.fori_loop(0, N_BUCKETS, body, jnp.full(bkt.shape, NEG, F32))
        o_ref[h // 2, :, (h % 2) * KEYS_PAD:(h % 2 + 1) * KEYS_PAD] = acc


def _t5_bucket(rel):
    n = -rel
    half = N_BUCKETS // 2
    max_exact = half // 2
    ret = jnp.where(n < 0, half, 0)
    n = jnp.abs(n)
    nf = jnp.maximum(n, 1).astype(F32)
    large = max_exact + (jnp.log(nf / max_exact) / math.log(MAX_DISTANCE / max_exact)
                         * (half - max_exact)).astype(jnp.int32)
    large = jnp.minimum(large, half - 1)
    return ret + jnp.where(n < max_exact, n, large)


def _pair_bias(rel_bias):
    n_heads = rel_bias.shape[1]
    rel = jnp.arange(KEYS)[None, :] - WINDOW - jnp.arange(CHUNK)[:, None]
    bkt = jnp.pad(_t5_bucket(rel).astype(jnp.int32), ((0, 0), (0, KEYS_PAD - KEYS)), constant_values=-1)
    return pl.pallas_call(
        functools.partial(_bias_kernel, n_heads=n_heads),
        in_specs=[pl.BlockSpec((CHUNK, KEYS_PAD), lambda: (0, 0)),
                  pl.BlockSpec(memory_space=pltpu.SMEM)],
        out_specs=pl.BlockSpec((n_heads // 2, CHUNK, 2 * KEYS_PAD), lambda: (0, 0, 0)),
        out_shape=jax.ShapeDtypeStruct((n_heads // 2, CHUNK, 2 * KEYS_PAD), F32),
        name="pair_bias",
    )(bkt, rel_bias)


def _mixer_kernel(*refs, nc, has_past, d_model, d_inner, n_heads, cols):
    t = nc * CHUNK
    conv_ch = d_inner + 2 * SSM_GROUPS * D_STATE
    gw = d_inner // SSM_GROUPS
    n_pairs = n_heads // 2
    kvw = N_KV_HEADS * LANES
    it = iter(refs)
    x_ref, mod_ref = next(it), next(it)
    if has_past:
        conv0_ref, ssm0_ref, k0_ref, v0_ref = next(it), next(it), next(it), next(it)
    (nmg_ref, wall_ref, convw_ref, convb_ref, dtb_ref, alog_ref, dskip_ref, ssmg_ref, qg_ref, kg_ref,
     sinks_ref, bias_ref, e_ref, wbs_ref, wba_ref, wo_ref) = [next(it) for _ in range(16)]
    xo_ref, convn_ref, ssmn_ref, kn_ref, vn_ref = [next(it) for _ in range(5)]
    (h_s, cb_s, z_s, xs_s, bc_s, dt_s, q_s, klo_s, khi_s, vlo_s, vhi_s, y_s, o_s, st_s) = list(it)

    s_idx = pl.program_id(1)
    n_steps = pl.num_programs(1)

    lane = lax.broadcasted_iota(jnp.int32, (CHUNK, LANES), 1)
    row = lax.broadcasted_iota(jnp.int32, (CHUNK, LANES), 0)
    low = lane < HEAD_DIM
    causal = row >= (lane & (CHUNK - 1))
    diag = row == (lane & (CHUNK - 1))
    trow = lax.broadcasted_iota(jnp.int32, (CHUNK, CHUNK), 0)
    tcol = lax.broadcasted_iota(jnp.int32, (CHUNK, CHUNK), 1)
    tril = jnp.where(trow >= tcol, 1.0, 0.0).astype(BF16)
    lowk = lax.broadcasted_iota(jnp.int32, (WINDOW, LANES), 1) < HEAD_DIM
    lowf = jnp.where(lax.broadcasted_iota(jnp.int32, (1, LANES), 1) < HEAD_DIM, 1.0, 0.0)
    highf = 1.0 - lowf

    @pl.when(s_idx == 0)
    def _():
        if has_past:
            cb_s[0:SUBLANES, :] = conv0_ref[0]
            st_s[...] = ssm0_ref[0].T
            for src_ref, lo_s, hi_s in ((k0_ref, klo_s, khi_s), (v0_ref, vlo_s, vhi_s)):
                for j in range(N_KV_HEADS // 2):
                    nat = src_ref[0, :, j * LANES:(j + 1) * LANES]
                    rol = pltpu.roll(nat, HEAD_DIM, 1)
                    a, b = 2 * j * LANES, (2 * j + 1) * LANES
                    lo_s[0:WINDOW, a:a + LANES] = jnp.where(lowk, nat, 0.0).astype(BF16)
                    hi_s[0:WINDOW, a:a + LANES] = jnp.where(lowk, 0.0, rol).astype(BF16)
                    lo_s[0:WINDOW, b:b + LANES] = jnp.where(lowk, rol, 0.0).astype(BF16)
                    hi_s[0:WINDOW, b:b + LANES] = jnp.where(lowk, 0.0, nat).astype(BF16)
        else:
            cb_s[0:SUBLANES, :] = jnp.zeros((SUBLANES, conv_ch), F32)
            st_s[...] = jnp.zeros(st_s.shape, F32)
            for buf in (klo_s, khi_s, vlo_s, vhi_s):
                buf[0:WINDOW, :] = jnp.zeros((WINDOW, kvw), BF16)

    x = x_ref[0]
    shift, scale = mod_ref[0, 0:1, :], mod_ref[0, 1:2, :]
    h = (x * _rms_scale(x) * nmg_ref[...] * (1.0 + scale) + shift).astype(BF16)
    h_s[...] = h

    blk = 512
    for j in range(0, d_inner, blk):
        zz = _dot(h, wall_ref[:, cols["z"] + j:cols["z"] + j + blk])
        z_s[:, j:j + blk] = zz * _sigmoid(zz)
    for j in range(0, conv_ch, blk):
        cb_s[SUBLANES:SUBLANES + t, j:j + blk] = _dot(h, wall_ref[:, cols["xbc"] + j:cols["xbc"] + j + blk])
    dt_s[...] = _dot(h, wall_ref[:, cols["dt"]:cols["dt"] + LANES])

    for j in range(0, conv_ch, blk):
        acc = cb_s[SUBLANES:SUBLANES + t, j:j + blk] * convw_ref[CONV_K - 1:CONV_K, j:j + blk]
        for k in range(CONV_K - 1):
            o = SUBLANES - (CONV_K - 1) + k
            acc = acc + cb_s[o:o + t, j:j + blk] * convw_ref[k:k + 1, j:j + blk]
        acc = acc + convb_ref[:, j:j + blk]
        u = acc * _sigmoid(acc)
        if j < d_inner:
            xs_s[:, j:j + blk] = u
        else:
            bc_s[:, j - d_inner:j - d_inner + blk] = u

    qg = qg_ref[...]
    for p in range(n_pairs):
        qv = _dot(h, wall_ref[:, cols["q"] + p * LANES:cols["q"] + (p + 1) * LANES])
        q2 = qv * qv
        sa = jnp.sum(q2 * lowf, axis=-1, keepdims=True)
        sb = jnp.sum(q2 * highf, axis=-1, keepdims=True)
        ms = (sa * lowf + sb * highf) * (1.0 / HEAD_DIM)
        q_s[:, p * LANES:(p + 1) * LANES] = (qv * lax.rsqrt(ms + EPS) * qg).astype(BF16)

    kg = kg_ref[...]
    knat, vnat = [], []
    for kv in range(N_KV_HEADS):
        c0 = kv * LANES
        kd = _dot(h, wall_ref[:, cols["k"] + c0:cols["k"] + c0 + LANES])
        ms = jnp.sum(kd * kd, axis=-1, keepdims=True) * (1.0 / LANES)
        kn = kd * lax.rsqrt(ms + EPS) * kg
        vd = _dot(h, wall_ref[:, cols["v"] + c0:cols["v"] + c0 + LANES])
        klo_s[WINDOW:WINDOW + t, c0:c0 + LANES] = (kn * lowf).astype(BF16)
        khi_s[WINDOW:WINDOW + t, c0:c0 + LANES] = (kn * highf).astype(BF16)
        vlo_s[WINDOW:WINDOW + t, c0:c0 + LANES] = (vd * lowf).astype(BF16)
        vhi_s[WINDOW:WINDOW + t, c0:c0 + LANES] = (vd * highf).astype(BF16)
        knat.append(kn)
        vnat.append(vd)

    dtb = dtb_ref[...]
    a_row = -jnp.exp(alog_ref[...])
    zpad = jnp.zeros((KEYS_PAD - KEYS, LANES), BF16)
    lane_s = lax.broadcasted_iota(jnp.int32, (CHUNK, 2 * KEYS_PAD), 1) & (KEYS_PAD - 1)

    for c in range(nc):
        r0 = c * CHUNK
        rows = slice(r0, r0 + CHUNK)

        xv = dt_s[rows, :] + dtb
        dt = jnp.maximum(xv, 0.0) + jnp.log1p(jnp.exp(-jnp.abs(xv)))
        da = dt * a_row
        d1 = da.astype(BF16)
        r1 = da - d1.astype(F32)
        d2 = r1.astype(BF16)
        d3 = (r1 - d2.astype(F32)).astype(BF16)
        acum = _dot(tril, d1) + _dot(tril, d2) + _dot(tril, d3)
        dt_hi, dt_lo = _split2(dt)
        ac_hi, ac_lo = _split2(acum)

        for g in range(SSM_GROUPS):
            gs = slice(g * gw, (g + 1) * gw)
            e_g = e_ref[:, gs]
            dte = _dot(dt_hi, e_g) + _dot(dt_lo, e_g)
            ae = _dot(ac_hi, e_g) + _dot(ac_lo, e_g)
            xs = xs_s[rows, gs]
            xdt = xs * dte
            bb = bc_s[rows, g * D_STATE:(g + 1) * D_STATE].astype(BF16)
            cc = bc_s[rows, (SSM_GROUPS + g) * D_STATE:(SSM_GROUPS + g + 1) * D_STATE].astype(BF16)
            cb2 = _dot_nt(cc, jnp.concatenate([bb, bb], axis=0))
            st = st_s[:, gs]
            y = _dot(cc, st.astype(BF16)) * jnp.exp(ae)
            ys = []
            for i in range(gw // LANES):
                ls = slice(i * LANES, (i + 1) * LANES)
                acol = ae[:, ls]
                arow = jnp.sum(jnp.where(diag, acol, 0.0), axis=0, keepdims=True)
                dec = jnp.exp(jnp.where(causal, acol - arow, NEG))
                mp = (cb2 * dec).astype(BF16)
                xp = xdt[:, ls]
                xbd = jnp.concatenate([jnp.where(low, xp, 0.0).astype(BF16),
                                       jnp.where(low, 0.0, xp).astype(BF16)], axis=0)
                ys.append(_dot(mp, xbd))
            y = y + jnp.concatenate(ys, axis=1) + xs * dskip_ref[:, gs]
            y = y * z_s[rows, gs]
            y = y * _rms_scale(y) * ssmg_ref[:, gs]
            y_s[rows, gs] = y.astype(BF16)
            ae_l = ae[CHUNK - 1:CHUNK, :]
            xt = (xdt * jnp.exp(ae_l - ae)).astype(BF16)
            st_s[:, gs] = st * jnp.exp(ae_l) + _dot_tn(bb, xt)

        if not has_past:
            thr = WINDOW - CHUNK * (s_idx * nc + c)
            valid = lane_s >= thr
        for kv in range(N_KV_HEADS):
            c0 = kv * LANES
            krows = slice(r0, r0 + KEYS)
            kbd = jnp.concatenate([klo_s[krows, c0:c0 + LANES], zpad, khi_s[krows, c0:c0 + LANES], zpad], axis=0)
            vbd = jnp.concatenate([vlo_s[krows, c0:c0 + LANES], zpad, vhi_s[krows, c0:c0 + LANES], zpad], axis=0)
            for pp in range(n_pairs // N_KV_HEADS):
                p = kv * (n_pairs // N_KV_HEADS) + pp
                sc = _dot_nt(q_s[rows, p * LANES:(p + 1) * LANES], kbd) + bias_ref[p]
                if not has_past:
                    sc = jnp.where(valid, sc, NEG)
                es, rs = [], []
                for hh in range(2):
                    sh = sc[:, hh * KEYS_PAD:(hh + 1) * KEYS_PAD]
                    sink = sinks_ref[2 * p + hh]
                    m = jnp.maximum(jnp.max(sh, axis=-1, keepdims=True), sink)
                    e = jnp.exp(sh - m)
                    den = jnp.sum(e, axis=-1, keepdims=True) + jnp.exp(sink - m)
                    es.append(e.astype(BF16))
                    rs.append(1.0 / den)
                o = _dot(jnp.concatenate(es, axis=1), vbd) * (rs[0] * lowf + rs[1] * highf)
                o_s[rows, p * LANES:(p + 1) * LANES] = o.astype(BF16)

    gate_m = mod_ref[0, 2:3, :]
    g_ssm = _sigmoid(_dot(h_s[...], wall_ref[:, cols["gs"]:cols["gs"] + d_model]))
    mixed = g_ssm * _dot(y_s[...], wbs_ref[...])
    g_att = _sigmoid(_dot(h_s[...], wall_ref[:, cols["ga"]:cols["ga"] + d_model]))
    mixed = mixed + g_att * _dot(o_s[...], wba_ref[...])
    xo_ref[0] = x_ref[0] + gate_m * _dot(mixed.astype(BF16), wo_ref[...])

    if not has_past:
        cb_s[0:SUBLANES, :] = cb_s[t:t + SUBLANES, :]
        for buf in (klo_s, khi_s, vlo_s, vhi_s):
            buf[0:WINDOW, :] = buf[t:t + WINDOW, :]

    @pl.when(s_idx == n_steps - 1)
    def _():
        convn_ref[0] = cb_s[t:t + SUBLANES, :]
        ssmn_ref[0] = st_s[...].T
        kr = kn_ref.shape[1]
        for j in range(N_KV_HEADS // 2):
            kn_ref[0, :, j * LANES:(j + 1) * LANES] = (knat[2 * j] * lowf + knat[2 * j + 1] * highf)[t - kr:t]
            vn_ref[0, :, j * LANES:(j + 1) * LANES] = (vnat[2 * j] * lowf + vnat[2 * j + 1] * highf)[t - kr:t]


def _const_spec(shape, single_buffer=False):
    idx = (0,) * len(shape)
    if single_buffer:
        return pl.BlockSpec(shape, lambda b, s: idx, pipeline_mode=pl.Buffered(1))
    return pl.BlockSpec(shape, lambda b, s: idx)


def _mixer_layer(x, mod, past, lw, pair_bias, e_mat, *, nc):
    bsz, seq, d_model = x.shape
    t = nc * CHUNK
    has_past = past is not None
    d_inner = lw["wbs"].shape[0]
    n_heads = lw["wba"].shape[0] // HEAD_DIM
    conv_ch = d_inner + 2 * SSM_GROUPS * D_STATE
    kvw = N_KV_HEADS * LANES
    kv_rows = seq if has_past else WINDOW
    assert seq % t == 0 and (has_past and seq == t or not has_past and t >= WINDOW)

    per_b = lambda *blk: pl.BlockSpec((1,) + blk, lambda b, s: (b,) + (0,) * len(blk))
    in_specs = [pl.BlockSpec((1, t, d_model), lambda b, s: (b, s, 0)), per_b(6, d_model)]
    args = [x, mod]
    if has_past:
        in_specs += [per_b(SUBLANES, conv_ch), per_b(d_inner, D_STATE),
                     per_b(WINDOW, N_KV_HEADS * HEAD_DIM), per_b(WINDOW, N_KV_HEADS * HEAD_DIM)]
        args += list(past)
    consts = [lw["nmg"], lw["wall"], lw["convw"], lw["convb"], lw["dtb"], lw["alog"], lw["dskip"], lw["ssmg"],
              lw["qg"], lw["kg"]]
    in_specs += [_const_spec(a.shape, single_buffer=a.size > (1 << 20)) for a in consts]
    in_specs.append(pl.BlockSpec(memory_space=pltpu.SMEM))
    tail = [pair_bias, e_mat, lw["wbs"], lw["wba"], lw["wo"]]
    in_specs += [_const_spec(a.shape, single_buffer=a.size > (1 << 18)) for a in tail]
    args += consts + [lw["sinks"]] + tail

    out_shape = (jax.ShapeDtypeStruct(x.shape, F32),
                 jax.ShapeDtypeStruct((bsz, SUBLANES, conv_ch), F32),
                 jax.ShapeDtypeStruct((bsz, d_inner, D_STATE), F32),
                 jax.ShapeDtypeStruct((bsz, kv_rows, N_KV_HEADS * HEAD_DIM), F32),
                 jax.ShapeDtypeStruct((bsz, kv_rows, N_KV_HEADS * HEAD_DIM), F32))
    out_specs = (pl.BlockSpec((1, t, d_model), lambda b, s: (b, s, 0)),
                 per_b(SUBLANES, conv_ch), per_b(d_inner, D_STATE),
                 per_b(kv_rows, N_KV_HEADS * HEAD_DIM), per_b(kv_rows, N_KV_HEADS * HEAD_DIM))
    scratch = [
        pltpu.VMEM((t, d_model), BF16),
        pltpu.VMEM((SUBLANES + t, conv_ch), F32),
        pltpu.VMEM((t, d_inner), F32),
        pltpu.VMEM((t, d_inner), F32),
        pltpu.VMEM((t, 2 * SSM_GROUPS * D_STATE), F32),
        pltpu.VMEM((t, LANES), F32),
        pltpu.VMEM((t, d_model), BF16),
        pltpu.VMEM((WINDOW + t, kvw), BF16),
        pltpu.VMEM((WINDOW + t, kvw), BF16),
        pltpu.VMEM((WINDOW + t, kvw), BF16),
        pltpu.VMEM((WINDOW + t, kvw), BF16),
        pltpu.VMEM((t, d_inner), BF16),
        pltpu.VMEM((t, d_model), BF16),
        pltpu.VMEM((D_STATE, d_inner), F32),
    ]
    kern = functools.partial(_mixer_kernel, nc=nc, has_past=has_past, d_model=d_model, d_inner=d_inner,
                             n_heads=n_heads, cols=lw["cols"])
    return pl.pallas_call(
        kern, grid=(bsz, seq // t), in_specs=in_specs, out_specs=out_specs, out_shape=out_shape,
        scratch_shapes=scratch,
        compiler_params=pltpu.CompilerParams(dimension_semantics=("arbitrary", "arbitrary"),
                                             vmem_limit_bytes=VMEM_LIMIT),
        name="mixer_past" if has_past else "mixer_prompt",
    )(*args)


def _ffn_kernel(x_ref, mod_ref, g_ref, wg_ref, wu_ref, wd_ref, o_ref):
    x = x_ref[0]
    shift, scale, gate_f = mod_ref[0, 3:4, :], mod_ref[0, 4:5, :], mod_ref[0, 5:6, :]
    h = (x * _rms_scale(x) * g_ref[...] * (1.0 + scale) + shift).astype(BF16)
    gate = _dot(h, wg_ref[...])
    up = _dot(h, wu_ref[...])
    act = (gate * _sigmoid(gate) * up).astype(BF16)
    o_ref[0] = x + gate_f * _dot(act, wd_ref[...])


def _ffn_layer(x, mod, lw, *, tm):
    bsz, seq, d_model = x.shape
    assert seq % tm == 0
    xspec = pl.BlockSpec((1, tm, d_model), lambda b, s: (b, s, 0))
    return pl.pallas_call(
        _ffn_kernel, grid=(bsz, seq // tm),
        in_specs=[xspec, pl.BlockSpec((1, 6, d_model), lambda b, s: (b, 0, 0)),
                  _const_spec(lw["nfg"].shape), _const_spec(lw["wg"].shape, True),
                  _const_spec(lw["wu"].shape, True), _const_spec(lw["wd"].shape, True)],
        out_specs=xspec, out_shape=jax.ShapeDtypeStruct(x.shape, F32),
        compiler_params=pltpu.CompilerParams(dimension_semantics=("arbitrary", "arbitrary"),
                                             vmem_limit_bytes=VMEM_LIMIT),
        name="ffn",
    )(x, mod, lw["nfg"], lw["wg"], lw["wu"], lw["wd"])


def _layer_weights(l, p):
    d_model = p["w_out"].shape[1]
    d_inner = p["w_br_ssm"].shape[1]
    n_ssm_heads = p["dt_bias"].shape[1]
    d_ff = p["w_down"].shape[1]
    nq = p["w_br_attn"].shape[1]
    nkv = N_KV_HEADS * HEAD_DIM
    conv_ch = d_inner + 2 * SSM_GROUPS * D_STATE
    wi = p["w_in"][l]
    bounds = [0, d_inner, d_inner + conv_ch, d_inner + conv_ch + n_ssm_heads]
    bounds += [bounds[-1] + nq, bounds[-1] + nq + nkv, bounds[-1] + nq + 2 * nkv]
    bounds += [bounds[-1] + d_model, bounds[-1] + 2 * d_model]
    z, xbc, dtc, q, k, v, gs, ga = [wi[:, a:b] for a, b in zip(bounds[:-1], bounds[1:])]
    dup = lambda w: jnp.concatenate([w.reshape(d_model, N_KV_HEADS, HEAD_DIM)] * 2, axis=-1).reshape(d_model, -1)
    pieces = [("z", z), ("xbc", xbc), ("dt", jnp.pad(dtc, ((0, 0), (0, LANES - n_ssm_heads)))),
              ("q", q), ("k", dup(k)), ("v", dup(v)), ("gs", gs), ("ga", ga)]
    cols, off = {}, 0
    for name, w in pieces:
        cols[name] = off
        off += w.shape[1]
    rep = lambda vec: jnp.repeat(vec, SSM_HEAD_DIM)[None, :]
    padl = lambda vec: jnp.pad(vec, (0, LANES - n_ssm_heads))[None, :]
    return dict(
        cols=cols,
        wall=jnp.concatenate([w for _, w in pieces], axis=1).astype(BF16),
        nmg=p["norm_mix_g"][l][None, :], nfg=p["norm_ffn_g"][l][None, :],
        convw=p["conv_w"][l], convb=p["conv_b"][l][None, :],
        dtb=padl(p["dt_bias"][l]), alog=padl(p["a_log"][l]), dskip=rep(p["d_skip"][l]),
        ssmg=p["ssm_norm_g"][l][None, :],
        qg=jnp.tile(p["q_norm_g"][l] * (HEAD_DIM ** -0.5), 2)[None, :], kg=jnp.tile(p["k_norm_g"][l], 2)[None, :],
        sinks=p["sinks"][l],
        wbs=p["w_br_ssm"][l].astype(BF16), wba=p["w_br_attn"][l].astype(BF16), wo=p["w_out"][l].astype(BF16),
        wg=p["w_gate_up"][l][:, :d_ff].astype(BF16), wu=p["w_gate_up"][l][:, d_ff:].astype(BF16),
        wd=p["w_down"][l].astype(BF16),
    )


def _expand_matrix(n_ssm_heads):
    r = jnp.arange(LANES)[:, None]
    c = jnp.arange(n_ssm_heads * SSM_HEAD_DIM)[None, :] // SSM_HEAD_DIM
    return (r == c).astype(BF16)


def kernel(x_prompt, x_sample, cache_k, cache_v, state_conv, state_ssm, c_prompt, c_sample, rel_bias, ada_w, ada_b, norm_mix_g, norm_ffn_g, w_in, conv_w, conv_b, dt_bias, a_log, d_skip, ssm_norm_g, q_norm_g, k_norm_g, sinks, w_br_ssm, w_br_attn, w_out, w_gate_up, w_down):
    p = dict(norm_mix_g=norm_mix_g, norm_ffn_g=norm_ffn_g, w_in=w_in, conv_w=conv_w, conv_b=conv_b,
             dt_bias=dt_bias, a_log=a_log, d_skip=d_skip, ssm_norm_g=ssm_norm_g, q_norm_g=q_norm_g,
             k_norm_g=k_norm_g, sinks=sinks, w_br_ssm=w_br_ssm, w_br_attn=w_br_attn, w_out=w_out,
             w_gate_up=w_gate_up, w_down=w_down)
    depth = w_in.shape[0]
    bp, seq_p, d_model = x_prompt.shape
    bs, seq_s, _ = x_sample.shape
    n_ssm_heads, ssm_p, d_state = state_ssm.shape[2:]
    d_inner = n_ssm_heads * ssm_p
    nkv = N_KV_HEADS * HEAD_DIM

    mod = _ada_mod(jnp.concatenate([c_prompt, c_sample], axis=0), ada_w, ada_b)
    mod = mod.reshape(depth, bp + bs, 6, d_model)
    pair_bias = _pair_bias(rel_bias)
    e_mat = _expand_matrix(n_ssm_heads)

    xp, xs = x_prompt, x_sample
    outs = [[] for _ in range(8)]
    for l in range(depth):
        lw = _layer_weights(l, p)
        mod_p, mod_s = mod[l, :bp], mod[l, bp:]
        past = (jnp.pad(state_conv[l], ((0, 0), (SUBLANES - (CONV_K - 1), 0), (0, 0))),
                state_ssm[l].reshape(bs, d_inner, d_state),
                cache_k[l].reshape(bs, WINDOW, nkv), cache_v[l].reshape(bs, WINDOW, nkv))
        xp, cvp, ssp, kp, vp = _mixer_layer(xp, mod_p, None, lw, pair_bias, e_mat, nc=2)
        xs, cvs, sss, ksn, vsn = _mixer_layer(xs, mod_s, past, lw, pair_bias, e_mat, nc=1)
        xp = _ffn_layer(xp, mod_p, lw, tm=min(512, seq_p))
        xs = _ffn_layer(xs, mod_s, lw, tm=seq_s)
        tail = SUBLANES - (CONV_K - 1)
        for lst, val in zip(outs, (
                cvp[:, tail:], cvs[:, tail:],
                ssp.reshape(bp, n_ssm_heads, ssm_p, d_state), sss.reshape(bs, n_ssm_heads, ssm_p, d_state),
                kp.reshape(bp, WINDOW, N_KV_HEADS, HEAD_DIM), ksn.reshape(bs, seq_s, N_KV_HEADS, HEAD_DIM),
                vp.reshape(bp, WINDOW, N_KV_HEADS, HEAD_DIM), vsn.reshape(bs, seq_s, N_KV_HEADS, HEAD_DIM))):
            lst.append(val)
    return (xp, xs) + tuple(jnp.stack(o) for o in outs)
```

```python
import functools
import math

import jax
import jax.numpy as jnp
from jax import lax
from jax.experimental import pallas as pl
from jax.experimental.pallas import tpu as pltpu

F32 = jnp.float32
BF16 = jnp.bfloat16

CHUNK = 64
EPS = 1e-6
SSM_GROUPS = 4
D_STATE = 128
SSM_HEAD_DIM = 64
CONV_K = 4
HEAD_DIM = 64
N_KV_HEADS = 4
WINDOW = 128
N_BUCKETS = 32
MAX_DISTANCE = 128

LANES = 128
SUBLANES = 8
KEYS = WINDOW + CHUNK
KEYS_PAD = 256
NEG = -1e30
VMEM_LIMIT = 56 * 1024 * 1024


def _dot(a, b):
    return jnp.dot(a, b, preferred_element_type=F32)


def _dot_nt(a, b):
    return lax.dot_general(a, b, (((1,), (1,)), ((), ())), preferred_element_type=F32)


def _dot_tn(a, b):
    return lax.dot_general(a, b, (((0,), (0,)), ((), ())), preferred_element_type=F32)


def _sigmoid(x):
    return 1.0 / (1.0 + jnp.exp(-x))


def _split2(x):
    hi = x.astype(BF16)
    lo = (x - hi.astype(F32)).astype(BF16)
    return hi, lo


def _rms_scale(x):
    return lax.rsqrt(jnp.mean(x * x, axis=-1, keepdims=True) + EPS)


def _ada_kernel(c_ref, w_ref, b_ref, o_ref):
    c = c_ref[...]
    s = (c * _sigmoid(c)).astype(BF16)
    o_ref[0] = _dot(s, w_ref[0].astype(BF16)) + b_ref[0]


def _ada_mod(c_all, ada_w, ada_b):
    depth, d, six_d = ada_w.shape
    n = c_all.shape[0]
    nblk = six_d // d
    return pl.pallas_call(
        _ada_kernel,
        grid=(depth, nblk),
        in_specs=[
            pl.BlockSpec((n, d), lambda l, j: (0, 0)),
            pl.BlockSpec((1, d, d), lambda l, j: (l, 0, j)),
            pl.BlockSpec((1, 1, d), lambda l, j: (l, 0, j)),
        ],
        out_specs=pl.BlockSpec((1, n, d), lambda l, j: (l, 0, j)),
        out_shape=jax.ShapeDtypeStruct((depth, n, six_d), F32),
        name="ada_mod",
    )(c_all, ada_w, ada_b.reshape(depth, 1, six_d))


def _bias_kernel(bkt_ref, rb_ref, o_ref, *, n_heads):
    bkt = bkt_ref[...]
    for h in range(n_heads):
        def body(b, acc, h=h):
            return jnp.where(bkt == b, rb_ref[b, h], acc)
        acc = lax.fori_loop(0, N_BUCKETS, body, jnp.full(bkt.shape, NEG, F32))
        o_ref[h // 2, :, (h % 2) * KEYS_PAD:(h % 2 + 1) * KEYS_PAD] = acc


def _t5_bucket(rel):
    n = -rel
    half = N_BUCKETS // 2
    max_exact = half // 2
    ret = jnp.where(n < 0, half, 0)
    n = jnp.abs(n)
    nf = jnp.maximum(n, 1).astype(F32)
    large = max_exact + (jnp.log(nf / max_exact) / math.log(MAX_DISTANCE / max_exact)
                         * (half - max_exact)).astype(jnp.int32)
    large = jnp.minimum(large, half - 1)
    return ret + jnp.where(n < max_exact, n, large)


def _pair_bias(rel_bias):
    n_heads = rel_bias.shape[1]
    rel = jnp.arange(KEYS)[None, :] - WINDOW - jnp.arange(CHUNK)[:, None]
    bkt = jnp.pad(_t5_bucket(rel).astype(jnp.int32), ((0, 0), (0, KEYS_PAD - KEYS)), constant_values=-1)
    return pl.pallas_call(
        functools.partial(_bias_kernel, n_heads=n_heads),
        in_specs=[pl.BlockSpec((CHUNK, KEYS_PAD), lambda: (0, 0)),
                  pl.BlockSpec(memory_space=pltpu.SMEM)],
        out_specs=pl.BlockSpec((n_heads // 2, CHUNK, 2 * KEYS_PAD), lambda: (0, 0, 0)),
        out_shape=jax.ShapeDtypeStruct((n_heads // 2, CHUNK, 2 * KEYS_PAD), F32),
        name="pair_bias",
    )(bkt, rel_bias)


def _mixer_kernel(*refs, nc, has_past, d_model, d_inner, n_heads, n_ssm_heads, cols):
    t = nc * CHUNK
    conv_ch = d_inner + 2 * SSM_GROUPS * D_STATE
    gw = d_inner // SSM_GROUPS
    n_pairs = n_heads // 2
    kvw = N_KV_HEADS * LANES
    it = iter(refs)
    x_ref, mod_ref = next(it), next(it)
    if has_past:
        conv0_ref, ssm0_ref, k0_ref, v0_ref = next(it), next(it), next(it), next(it)
    (nmg_ref, wall_ref, convw_ref, convb_ref, dtb_ref, alog_ref, dskip_ref, ssmg_ref, qg_ref, kg_ref,
     sinks_ref, bias_ref, e_ref, wbs_ref, wba_ref, wo_ref) = [next(it) for _ in range(16)]
    xo_ref, convn_ref, ssmn_ref, kn_ref, vn_ref = [next(it) for _ in range(5)]
    (h_s, cb_s, z_s, xs_s, bc_s, dt_s, q_s, klo_s, khi_s, vlo_s, vhi_s, y_s, o_s, st_s,
     xlo_s, xhi_s, xt_s, eae_s, edl_s, mp_s, yin_s, sc_s, pe_s, rd_s) = list(it)

    s_idx = pl.program_id(1)
    n_steps = pl.num_programs(1)

    lane = lax.broadcasted_iota(jnp.int32, (CHUNK, LANES), 1)
    row = lax.broadcasted_iota(jnp.int32, (CHUNK, LANES), 0)
    causal = row >= (lane & (CHUNK - 1))
    diag = row == (lane & (CHUNK - 1))
    lowk = lax.broadcasted_iota(jnp.int32, (WINDOW, LANES), 1) < HEAD_DIM
    lowf = jnp.where(lax.broadcasted_iota(jnp.int32, (1, LANES), 1) < HEAD_DIM, 1.0, 0.0)
    highf = 1.0 - lowf

    @pl.when(s_idx == 0)
    def _():
        if has_past:
            cb_s[0:SUBLANES, :] = conv0_ref[0]
            st_s[...] = ssm0_ref[0].T
            for src_ref, lo_s, hi_s in ((k0_ref, klo_s, khi_s), (v0_ref, vlo_s, vhi_s)):
                for j in range(N_KV_HEADS // 2):
                    nat = src_ref[0, :, j * LANES:(j + 1) * LANES]
                    rol = pltpu.roll(nat, HEAD_DIM, 1)
                    a, b = 2 * j * LANES, (2 * j + 1) * LANES
                    lo_s[0:WINDOW, a:a + LANES] = jnp.where(lowk, nat, 0.0).astype(BF16)
                    hi_s[0:WINDOW, a:a + LANES] = jnp.where(lowk, 0.0, rol).astype(BF16)
                    lo_s[0:WINDOW, b:b + LANES] = jnp.where(lowk, rol, 0.0).astype(BF16)
                    hi_s[0:WINDOW, b:b + LANES] = jnp.where(lowk, 0.0, nat).astype(BF16)
        else:
            cb_s[0:SUBLANES, :] = jnp.zeros((SUBLANES, conv_ch), F32)
            st_s[...] = jnp.zeros(st_s.shape, F32)
            for buf in (klo_s, khi_s, vlo_s, vhi_s):
                buf[0:WINDOW, :] = jnp.zeros((WINDOW, kvw), BF16)

    x = x_ref[0]
    shift, scale = mod_ref[0, 0:1, :], mod_ref[0, 1:2, :]
    h = (x * _rms_scale(x) * nmg_ref[...] * (1.0 + scale) + shift).astype(BF16)
    h_s[...] = h

    blk = 512
    for j in range(0, d_inner, blk):
        zz = _dot(h, wall_ref[:, cols["z"] + j:cols["z"] + j + blk])
        z_s[:, j:j + blk] = zz * _sigmoid(zz)
    for j in range(0, conv_ch, blk):
        cb_s[SUBLANES:SUBLANES + t, j:j + blk] = _dot(h, wall_ref[:, cols["xbc"] + j:cols["xbc"] + j + blk])
    dt_s[...] = _dot(h, wall_ref[:, cols["dt"]:cols["dt"] + LANES])

    for j in range(0, conv_ch, blk):
        acc = cb_s[SUBLANES:SUBLANES + t, j:j + blk] * convw_ref[CONV_K - 1:CONV_K, j:j + blk]
        for k in range(CONV_K - 1):
            o = SUBLANES - (CONV_K - 1) + k
            acc = acc + cb_s[o:o + t, j:j + blk] * convw_ref[k:k + 1, j:j + blk]
        acc = acc + convb_ref[:, j:j + blk]
        u = acc * _sigmoid(acc)
        if j < d_inner:
            xs_s[:, j:j + blk] = u
        else:
            bc_s[:, j - d_inner:j - d_inner + blk] = u

    qg = qg_ref[...]
    for p in range(n_pairs):
        qv = _dot(h, wall_ref[:, cols["q"] + p * LANES:cols["q"] + (p + 1) * LANES])
        q2 = qv * qv
        sa = jnp.sum(q2 * lowf, axis=-1, keepdims=True)
        sb = jnp.sum(q2 * highf, axis=-1, keepdims=True)
        ms = (sa * lowf + sb * highf) * (1.0 / HEAD_DIM)
        q_s[:, p * LANES:(p + 1) * LANES] = (qv * lax.rsqrt(ms + EPS) * qg).astype(BF16)

    kg = kg_ref[...]
    knat, vnat = [], []
    for kv in range(N_KV_HEADS):
        c0 = kv * LANES
        kd = _dot(h, wall_ref[:, cols["k"] + c0:cols["k"] + c0 + LANES])
        ms = jnp.sum(kd * kd, axis=-1, keepdims=True) * (1.0 / LANES)
        kn = kd * lax.rsqrt(ms + EPS) * kg
        vd = _dot(h, wall_ref[:, cols["v"] + c0:cols["v"] + c0 + LANES])
        klo_s[WINDOW:WINDOW + t, c0:c0 + LANES] = (kn * lowf).astype(BF16)
        khi_s[WINDOW:WINDOW + t, c0:c0 + LANES] = (kn * highf).astype(BF16)
        vlo_s[WINDOW:WINDOW + t, c0:c0 + LANES] = (vd * lowf).astype(BF16)
        vhi_s[WINDOW:WINDOW + t, c0:c0 + LANES] = (vd * highf).astype(BF16)
        knat.append(kn)
        vnat.append(vd)

    dtb = dtb_ref[...]
    a_row = -jnp.exp(alog_ref[...])
    head_lanes = jnp.where(lax.broadcasted_iota(jnp.int32, (1, LANES), 1) < n_ssm_heads, 1.0, 0.0)
    xv = dt_s[...] + dtb
    dt = (jnp.maximum(xv, 0.0) + jnp.log1p(jnp.exp(-jnp.abs(xv)))) * head_lanes
    da = dt * a_row
    d1 = da.astype(BF16)
    r1 = da - d1.astype(F32)
    d2 = r1.astype(BF16)
    d3 = (r1 - d2.astype(F32)).astype(BF16)
    chunk_shift = CHUNK.bit_length() - 1
    br = lax.broadcasted_iota(jnp.int32, (t, t), 0)
    bc = lax.broadcasted_iota(jnp.int32, (t, t), 1)
    same_chunk = jnp.right_shift(br, chunk_shift) == jnp.right_shift(bc, chunk_shift)
    tril = jnp.where((br >= bc) & same_chunk, 1.0, 0.0).astype(BF16)
    acum = _dot(tril, d1) + _dot(tril, d2) + _dot(tril, d3)

    def pack_hi_lo(v):
        hi = v.astype(BF16).astype(F32)
        return (hi + pltpu.roll(v - hi, n_ssm_heads, 1)).astype(BF16)

    pdt, pac = pack_hi_lo(dt), pack_hi_lo(acum)
    lowf_g = jnp.concatenate([lowf] * (gw // LANES), axis=1)
    highf_g = 1.0 - lowf_g
    for g in range(SSM_GROUPS):
        gs = slice(g * gw, (g + 1) * gw)
        e_g = e_ref[:, gs]
        xdt = xs_s[:, gs] * _dot(pdt, e_g)
        ae = _dot(pac, e_g)
        xlo_s[:, gs] = (xdt * lowf_g).astype(BF16)
        xhi_s[:, gs] = (xdt * highf_g).astype(BF16)
        eae_s[:, gs] = jnp.exp(ae)
        for c in range(nc):
            rows = slice(c * CHUNK, (c + 1) * CHUNK)
            ae_c = ae[rows]
            ae_l = ae_c[CHUNK - 1:CHUNK, :]
            xt_s[rows, gs] = (xdt[rows] * jnp.exp(ae_l - ae_c)).astype(BF16)
            edl_s[c * SUBLANES:(c + 1) * SUBLANES, gs] = jnp.broadcast_to(jnp.exp(ae_l), (SUBLANES, gw))
            bb = bc_s[rows, g * D_STATE:(g + 1) * D_STATE].astype(BF16)
            cc = bc_s[rows, (SSM_GROUPS + g) * D_STATE:(SSM_GROUPS + g + 1) * D_STATE].astype(BF16)
            cb2 = _dot_nt(cc, jnp.concatenate([bb, bb], axis=0))
            for i in range(gw // LANES):
                acol = ae_c[:, i * LANES:(i + 1) * LANES]
                arow = jnp.sum(jnp.where(diag, acol, 0.0), axis=0, keepdims=True)
                dec = jnp.exp(jnp.where(causal, acol - arow, NEG))
                mp_s[(c * SSM_GROUPS + g) * (gw // LANES) + i] = (cb2 * dec).astype(BF16)

    for c in range(nc):
        rows = slice(c * CHUNK, (c + 1) * CHUNK)
        for j in range(d_inner // LANES):
            ls = slice(j * LANES, (j + 1) * LANES)
            g, i = divmod(j, gw // LANES)
            xbd = jnp.concatenate([xlo_s[rows, ls], xhi_s[rows, ls]], axis=0)
            yin_s[rows, ls] = _dot(mp_s[(c * SSM_GROUPS + g) * (gw // LANES) + i], xbd)

    zpad = jnp.zeros((KEYS_PAD - KEYS, LANES), BF16)
    lane_s = lax.broadcasted_iota(jnp.int32, (CHUNK, 2 * KEYS_PAD), 1) & (KEYS_PAD - 1)
    ppk = n_pairs // N_KV_HEADS
    for c in range(nc):
        rows = slice(c * CHUNK, (c + 1) * CHUNK)
        krows = slice(c * CHUNK, c * CHUNK + KEYS)
        for kv in range(N_KV_HEADS):
            c0 = kv * LANES
            kbd = jnp.concatenate([klo_s[krows, c0:c0 + LANES], zpad, khi_s[krows, c0:c0 + LANES], zpad], axis=0)
            for pp in range(ppk):
                p = kv * ppk + pp
                sc = _dot_nt(q_s[rows, p * LANES:(p + 1) * LANES], kbd) + bias_ref[p]
                if not has_past:
                    sc = jnp.where(lane_s >= WINDOW - CHUNK * (s_idx * nc + c), sc, NEG)
                sc_s[c * n_pairs + p] = sc
    for c in range(nc):
        for p in range(n_pairs):
            es, rs = [], []
            for hh in range(2):
                sh = sc_s[c * n_pairs + p, :, hh * KEYS_PAD:(hh + 1) * KEYS_PAD]
                sink = sinks_ref[2 * p + hh]
                m = jnp.maximum(jnp.max(sh, axis=-1, keepdims=True), sink)
                e = jnp.exp(sh - m)
                den = jnp.sum(e, axis=-1, keepdims=True) + jnp.exp(sink - m)
                es.append(e.astype(BF16))
                rs.append(1.0 / den)
            pe_s[c * n_pairs + p] = jnp.concatenate(es, axis=1)
            rd_s[c * n_pairs + p] = rs[0] * lowf + rs[1] * highf
    for c in range(nc):
        rows = slice(c * CHUNK, (c + 1) * CHUNK)
        krows = slice(c * CHUNK, c * CHUNK + KEYS)
        for kv in range(N_KV_HEADS):
            c0 = kv * LANES
            vbd = jnp.concatenate([vlo_s[krows, c0:c0 + LANES], zpad, vhi_s[krows, c0:c0 + LANES], zpad], axis=0)
            for pp in range(ppk):
                p = kv * ppk + pp
                o = _dot(pe_s[c * n_pairs + p], vbd) * rd_s[c * n_pairs + p]
                o_s[rows, p * LANES:(p + 1) * LANES] = o.astype(BF16)

    for c in range(nc):
        rows = slice(c * CHUNK, (c + 1) * CHUNK)
        for g in range(SSM_GROUPS):
            gs = slice(g * gw, (g + 1) * gw)
            bb = bc_s[rows, g * D_STATE:(g + 1) * D_STATE].astype(BF16)
            cc = bc_s[rows, (SSM_GROUPS + g) * D_STATE:(SSM_GROUPS + g + 1) * D_STATE].astype(BF16)
            st = st_s[:, gs]
            y = yin_s[rows, gs] + _dot(cc, st.astype(BF16)) * eae_s[rows, gs] + xs_s[rows, gs] * dskip_ref[:, gs]
            y = y * z_s[rows, gs]
            y = y * _rms_scale(y) * ssmg_ref[:, gs]
            y_s[rows, gs] = y.astype(BF16)
            st_s[:, gs] = st * edl_s[c * SUBLANES:c * SUBLANES + 1, gs] + _dot_tn(bb, xt_s[rows, gs])

    gate_m = mod_ref[0, 2:3, :]
    g_ssm = _sigmoid(_dot(h_s[...], wall_ref[:, cols["gs"]:cols["gs"] + d_model]))
    mixed = g_ssm * _dot(y_s[...], wbs_ref[...])
    g_att = _sigmoid(_dot(h_s[...], wall_ref[:, cols["ga"]:cols["ga"] + d_model]))
    mixed = mixed + g_att * _dot(o_s[...], wba_ref[...])
    xo_ref[0] = x_ref[0] + gate_m * _dot(mixed.astype(BF16), wo_ref[...])

    if not has_past:
        cb_s[0:SUBLANES, :] = cb_s[t:t + SUBLANES, :]
        for buf in (klo_s, khi_s, vlo_s, vhi_s):
            buf[0:WINDOW, :] = buf[t:t + WINDOW, :]

    @pl.when(s_idx == n_steps - 1)
    def _():
        convn_ref[0] = cb_s[t:t + SUBLANES, :]
        ssmn_ref[0] = st_s[...].T
        kr = kn_ref.shape[1]
        for j in range(N_KV_HEADS // 2):
            kn_ref[0, :, j * LANES:(j + 1) * LANES] = (knat[2 * j] * lowf + knat[2 * j + 1] * highf)[t - kr:t]
            vn_ref[0, :, j * LANES:(j + 1) * LANES] = (vnat[2 * j] * lowf + vnat[2 * j + 1] * highf)[t - kr:t]


def _const_spec(shape, single_buffer=False):
    idx = (0,) * len(shape)
    if single_buffer:
        return pl.BlockSpec(shape, lambda b, s: idx, pipeline_mode=pl.Buffered(1))
    return pl.BlockSpec(shape, lambda b, s: idx)


def _mixer_layer(x, mod, past, lw, pair_bias, e_mat, *, nc):
    bsz, seq, d_model = x.shape
    t = nc * CHUNK
    has_past = past is not None
    d_inner = lw["wbs"].shape[0]
    n_heads = lw["wba"].shape[0] // HEAD_DIM
    conv_ch = d_inner + 2 * SSM_GROUPS * D_STATE
    kvw = N_KV_HEADS * LANES
    kv_rows = seq if has_past else WINDOW
    assert seq % t == 0 and (has_past and seq == t or not has_past and t >= WINDOW)

    per_b = lambda *blk: pl.BlockSpec((1,) + blk, lambda b, s: (b,) + (0,) * len(blk))
    in_specs = [pl.BlockSpec((1, t, d_model), lambda b, s: (b, s, 0)), per_b(6, d_model)]
    args = [x, mod]
    if has_past:
        in_specs += [per_b(SUBLANES, conv_ch), per_b(d_inner, D_STATE),
                     per_b(WINDOW, N_KV_HEADS * HEAD_DIM), per_b(WINDOW, N_KV_HEADS * HEAD_DIM)]
        args += list(past)
    consts = [lw["nmg"], lw["wall"], lw["convw"], lw["convb"], lw["dtb"], lw["alog"], lw["dskip"], lw["ssmg"],
              lw["qg"], lw["kg"]]
    in_specs += [_const_spec(a.shape, single_buffer=a.size > (1 << 20)) for a in consts]
    in_specs.append(pl.BlockSpec(memory_space=pltpu.SMEM))
    tail = [pair_bias, e_mat, lw["wbs"], lw["wba"], lw["wo"]]
    in_specs += [_const_spec(a.shape, single_buffer=a.size > (1 << 18)) for a in tail]
    args += consts + [lw["sinks"]] + tail

    out_shape = (jax.ShapeDtypeStruct(x.shape, F32),
                 jax.ShapeDtypeStruct((bsz, SUBLANES, conv_ch), F32),
                 jax.ShapeDtypeStruct((bsz, d_inner, D_STATE), F32),
                 jax.ShapeDtypeStruct((bsz, kv_rows, N_KV_HEADS * HEAD_DIM), F32),
                 jax.ShapeDtypeStruct((bsz, kv_rows, N_KV_HEADS * HEAD_DIM), F32))
    out_specs = (pl.BlockSpec((1, t, d_model), lambda b, s: (b, s, 0)),
                 per_b(SUBLANES, conv_ch), per_b(d_inner, D_STATE),
                 per_b(kv_rows, N_KV_HEADS * HEAD_DIM), per_b(kv_rows, N_KV_HEADS * HEAD_DIM))
    scratch = [
        pltpu.VMEM((t, d_model), BF16),
        pltpu.VMEM((SUBLANES + t, conv_ch), F32),
        pltpu.VMEM((t, d_inner), F32),
        pltpu.VMEM((t, d_inner), F32),
        pltpu.VMEM((t, 2 * SSM_GROUPS * D_STATE), F32),
        pltpu.VMEM((t, LANES), F32),
        pltpu.VMEM((t, d_model), BF16),
        pltpu.VMEM((WINDOW + t, kvw), BF16),
        pltpu.VMEM((WINDOW + t, kvw), BF16),
        pltpu.VMEM((WINDOW + t, kvw), BF16),
        pltpu.VMEM((WINDOW + t, kvw), BF16),
        pltpu.VMEM((t, d_inner), BF16),
        pltpu.VMEM((t, d_model), BF16),
        pltpu.VMEM((D_STATE, d_inner), F32),
        pltpu.VMEM((t, d_inner), BF16),
        pltpu.VMEM((t, d_inner), BF16),
        pltpu.VMEM((t, d_inner), BF16),
        pltpu.VMEM((t, d_inner), F32),
        pltpu.VMEM((nc * SUBLANES, d_inner), F32),
        pltpu.VMEM((nc * d_inner // LANES, CHUNK, LANES), BF16),
        pltpu.VMEM((t, d_inner), F32),
        pltpu.VMEM((nc * n_heads // 2, CHUNK, 2 * KEYS_PAD), F32),
        pltpu.VMEM((nc * n_heads // 2, CHUNK, 2 * KEYS_PAD), BF16),
        pltpu.VMEM((nc * n_heads // 2, CHUNK, LANES), F32),
    ]
    kern = functools.partial(_mixer_kernel, nc=nc, has_past=has_past, d_model=d_model, d_inner=d_inner,
                             n_heads=n_heads, n_ssm_heads=d_inner // SSM_HEAD_DIM, cols=lw["cols"])
    return pl.pallas_call(
        kern, grid=(bsz, seq // t), in_specs=in_specs, out_specs=out_specs, out_shape=out_shape,
        scratch_shapes=scratch,
        compiler_params=pltpu.CompilerParams(dimension_semantics=("arbitrary", "arbitrary"),
                                             vmem_limit_bytes=VMEM_LIMIT),
        name="mixer_past" if has_past else "mixer_prompt",
    )(*args)


def _ffn_kernel(x_ref, mod_ref, g_ref, wg_ref, wu_ref, wd_ref, o_ref):
    x = x_ref[0]
    shift, scale, gate_f = mod_ref[0, 3:4, :], mod_ref[0, 4:5, :], mod_ref[0, 5:6, :]
    h = (x * _rms_scale(x) * g_ref[...] * (1.0 + scale) + shift).astype(BF16)
    gate = _dot(h, wg_ref[...])
    up = _dot(h, wu_ref[...])
    act = (gate * _sigmoid(gate) * up).astype(BF16)
    o_ref[0] = x + gate_f * _dot(act, wd_ref[...])


def _ffn_layer(x, mod, lw, *, tm):
    bsz, seq, d_model = x.shape
    assert seq % tm == 0
    xspec = pl.BlockSpec((1, tm, d_model), lambda b, s: (b, s, 0))
    return pl.pallas_call(
        _ffn_kernel, grid=(bsz, seq // tm),
        in_specs=[xspec, pl.BlockSpec((1, 6, d_model), lambda b, s: (b, 0, 0)),
                  _const_spec(lw["nfg"].shape), _const_spec(lw["wg"].shape, True),
                  _const_spec(lw["wu"].shape, True), _const_spec(lw["wd"].shape, True)],
        out_specs=xspec, out_shape=jax.ShapeDtypeStruct(x.shape, F32),
        compiler_params=pltpu.CompilerParams(dimension_semantics=("arbitrary", "arbitrary"),
                                             vmem_limit_bytes=VMEM_LIMIT),
        name="ffn",
    )(x, mod, lw["nfg"], lw["wg"], lw["wu"], lw["wd"])


def _layer_weights(l, p):
    d_model = p["w_out"].shape[1]
    d_inner = p["w_br_ssm"].shape[1]
    n_ssm_heads = p["dt_bias"].shape[1]
    d_ff = p["w_down"].shape[1]
    nq = p["w_br_attn"].shape[1]
    nkv = N_KV_HEADS * HEAD_DIM
    conv_ch = d_inner + 2 * SSM_GROUPS * D_STATE
    wi = p["w_in"][l]
    bounds = [0, d_inner, d_inner + conv_ch, d_inner + conv_ch + n_ssm_heads]
    bounds += [bounds[-1] + nq, bounds[-1] + nq + nkv, bounds[-1] + nq + 2 * nkv]
    bounds += [bounds[-1] + d_model, bounds[-1] + 2 * d_model]
    z, xbc, dtc, q, k, v, gs, ga = [wi[:, a:b] for a, b in zip(bounds[:-1], bounds[1:])]
    dup = lambda w: jnp.concatenate([w.reshape(d_model, N_KV_HEADS, HEAD_DIM)] * 2, axis=-1).reshape(d_model, -1)
    pieces = [("z", z), ("xbc", xbc), ("dt", jnp.pad(dtc, ((0, 0), (0, LANES - n_ssm_heads)))),
              ("q", q), ("k", dup(k)), ("v", dup(v)), ("gs", gs), ("ga", ga)]
    cols, off = {}, 0
    for name, w in pieces:
        cols[name] = off
        off += w.shape[1]
    rep = lambda vec: jnp.repeat(vec, SSM_HEAD_DIM)[None, :]
    padl = lambda vec: jnp.pad(vec, (0, LANES - n_ssm_heads))[None, :]
    return dict(
        cols=cols,
        wall=jnp.concatenate([w for _, w in pieces], axis=1).astype(BF16),
        nmg=p["norm_mix_g"][l][None, :], nfg=p["norm_ffn_g"][l][None, :],
        convw=p["conv_w"][l], convb=p["conv_b"][l][None, :],
        dtb=padl(p["dt_bias"][l]), alog=padl(p["a_log"][l]), dskip=rep(p["d_skip"][l]),
        ssmg=p["ssm_norm_g"][l][None, :],
        qg=jnp.tile(p["q_norm_g"][l] * (HEAD_DIM ** -0.5), 2)[None, :], kg=jnp.tile(p["k_norm_g"][l], 2)[None, :],
        sinks=p["sinks"][l],
        wbs=p["w_br_ssm"][l].astype(BF16), wba=p["w_br_attn"][l].astype(BF16), wo=p["w_out"][l].astype(BF16),
        wg=p["w_gate_up"][l][:, :d_ff].astype(BF16), wu=p["w_gate_up"][l][:, d_ff:].astype(BF16),
        wd=p["w_down"][l].astype(BF16),
    )


def _expand_matrix(n_ssm_heads):
    assert 3 * n_ssm_heads <= LANES
    r = jnp.arange(LANES)[:, None]
    src = (r < 2 * n_ssm_heads) | (r >= LANES - n_ssm_heads)
    c = jnp.arange(n_ssm_heads * SSM_HEAD_DIM)[None, :] // SSM_HEAD_DIM
    return (src & ((r % n_ssm_heads) == c)).astype(BF16)


def kernel(x_prompt, x_sample, cache_k, cache_v, state_conv, state_ssm, c_prompt, c_sample, rel_bias, ada_w, ada_b, norm_mix_g, norm_ffn_g, w_in, conv_w, conv_b, dt_bias, a_log, d_skip, ssm_norm_g, q_norm_g, k_norm_g, sinks, w_br_ssm, w_br_attn, w_out, w_gate_up, w_down):
    p = dict(norm_mix_g=norm_mix_g, norm_ffn_g=norm_ffn_g, w_in=w_in, conv_w=conv_w, conv_b=conv_b,
             dt_bias=dt_bias, a_log=a_log, d_skip=d_skip, ssm_norm_g=ssm_norm_g, q_norm_g=q_norm_g,
             k_norm_g=k_norm_g, sinks=sinks, w_br_ssm=w_br_ssm, w_br_attn=w_br_attn, w_out=w_out,
             w_gate_up=w_gate_up, w_down=w_down)
    depth = w_in.shape[0]
    bp, seq_p, d_model = x_prompt.shape
    bs, seq_s, _ = x_sample.shape
    n_ssm_heads, ssm_p, d_state = state_ssm.shape[2:]
    d_inner = n_ssm_heads * ssm_p
    nkv = N_KV_HEADS * HEAD_DIM

    mod = _ada_mod(jnp.concatenate([c_prompt, c_sample], axis=0), ada_w, ada_b)
    mod = mod.reshape(depth, bp + bs, 6, d_model)
    pair_bias = _pair_bias(rel_bias)
    e_mat = _expand_matrix(n_ssm_heads)

    xp, xs = x_prompt, x_sample
    outs = [[] for _ in range(8)]
    for l in range(depth):
        lw = _layer_weights(l, p)
        mod_p, mod_s = mod[l, :bp], mod[l, bp:]
        past = (jnp.pad(state_conv[l], ((0, 0), (SUBLANES - (CONV_K - 1), 0), (0, 0))),
                state_ssm[l].reshape(bs, d_inner, d_state),
                cache_k[l].reshape(bs, WINDOW, nkv), cache_v[l].reshape(bs, WINDOW, nkv))
        xp, cvp, ssp, kp, vp = _mixer_layer(xp, mod_p, None, lw, pair_bias, e_mat, nc=2)
        xs, cvs, sss, ksn, vsn = _mixer_layer(xs, mod_s, past, lw, pair_bias, e_mat, nc=1)
        xp = _ffn_layer(xp, mod_p, lw, tm=min(512, seq_p))
        xs = _ffn_layer(xs, mod_s, lw, tm=seq_s)
        tail = SUBLANES - (CONV_K - 1)
        for lst, val in zip(outs, (
                cvp[:, tail:], cvs[:, tail:],
                ssp.reshape(bp, n_ssm_heads, ssm_p, d_state), sss.reshape(bs, n_ssm_heads, ssm_p, d_state),
                kp.reshape(bp, WINDOW, N_KV_HEADS, HEAD_DIM), ksn.reshape(bs, seq_s, N_KV_HEADS, HEAD_DIM),
                vp.reshape(bp, WINDOW, N_KV_HEADS, HEAD_DIM), vsn.reshape(bs, seq_s, N_KV_HEADS, HEAD_DIM))):
            lst.append(val)
    return (xp, xs) + tuple(jnp.stack(o) for o in outs)
```

```python
import functools
import math

import jax
import jax.numpy as jnp
from jax import lax
from jax.experimental import pallas as pl
from jax.experimental.pallas import tpu as pltpu

F32 = jnp.float32
BF16 = jnp.bfloat16

CHUNK = 64
EPS = 1e-6
SSM_GROUPS = 4
D_STATE = 128
SSM_HEAD_DIM = 64
CONV_K = 4
HEAD_DIM = 64
N_KV_HEADS = 4
WINDOW = 128
N_BUCKETS = 32
MAX_DISTANCE = 128

LANES = 128
SUBLANES = 8
KEYS = WINDOW + CHUNK
KEYS_PAD = 256
NEG = -1e30
VMEM_LIMIT = 56 * 1024 * 1024


def _dot(a, b):
    return jnp.dot(a, b, preferred_element_type=F32)


def _dot_nt(a, b):
    return lax.dot_general(a, b, (((1,), (1,)), ((), ())), preferred_element_type=F32)


def _dot_tn(a, b):
    return lax.dot_general(a, b, (((0,), (0,)), ((), ())), preferred_element_type=F32)


def _sigmoid(x):
    return 1.0 / (1.0 + jnp.exp(-x))


def _interleave(*streams):
    totals = [float(sum(w for w, _ in st)) or 1.0 for st in streams]
    pos, done = [0] * len(streams), [0.0] * len(streams)
    while True:
        live = [k for k in range(len(streams)) if pos[k] < len(streams[k])]
        if not live:
            return
        k = min(live, key=lambda k: done[k] / totals[k])
        w, fn = streams[k][pos[k]]
        fn()
        pos[k] += 1
        done[k] += w


def _rms_scale(x):
    return lax.rsqrt(jnp.mean(x * x, axis=-1, keepdims=True) + EPS)


def _ada_kernel(c_ref, w_ref, b_ref, o_ref):
    c = c_ref[...]
    s = (c * _sigmoid(c)).astype(BF16)
    o_ref[0] = _dot(s, w_ref[0].astype(BF16)) + b_ref[0]


def _ada_mod(c_all, ada_w, ada_b):
    depth, d, six_d = ada_w.shape
    n = c_all.shape[0]
    nblk = six_d // d
    return pl.pallas_call(
        _ada_kernel,
        grid=(depth, nblk),
        in_specs=[
            pl.BlockSpec((n, d), lambda l, j: (0, 0)),
            pl.BlockSpec((1, d, d), lambda l, j: (l, 0, j)),
            pl.BlockSpec((1, 1, d), lambda l, j: (l, 0, j)),
        ],
        out_specs=pl.BlockSpec((1, n, d), lambda l, j: (l, 0, j)),
        out_shape=jax.ShapeDtypeStruct((depth, n, six_d), F32),
        name="ada_mod",
    )(c_all, ada_w, ada_b.reshape(depth, 1, six_d))


def _bias_kernel(bkt_ref, rb_ref, o_ref, *, n_heads):
    bkt = bkt_ref[...]
    for h in range(n_heads):
        def body(b, acc, h=h):
            return jnp.where(bkt == b, rb_ref[b, h], acc)
        acc = lax.fori_loop(0, N_BUCKETS, body, jnp.full(bkt.shape, NEG, F32))
        o_ref[h // 2, :, (h % 2) * KEYS_PAD:(h % 2 + 1) * KEYS_PAD] = acc


def _t5_bucket(rel):
    n = -rel
    half = N_BUCKETS // 2
    max_exact = half // 2
    ret = jnp.where(n < 0, half, 0)
    n = jnp.abs(n)
    nf = jnp.maximum(n, 1).astype(F32)
    large = max_exact + (jnp.log(nf / max_exact) / math.log(MAX_DISTANCE / max_exact)
                         * (half - max_exact)).astype(jnp.int32)
    large = jnp.minimum(large, half - 1)
    return ret + jnp.where(n < max_exact, n, large)


def _pair_bias(rel_bias):
    n_heads = rel_bias.shape[1]
    rel = jnp.arange(KEYS)[None, :] - WINDOW - jnp.arange(CHUNK)[:, None]
    bkt = jnp.pad(_t5_bucket(rel).astype(jnp.int32), ((0, 0), (0, KEYS_PAD - KEYS)), constant_values=-1)
    return pl.pallas_call(
        functools.partial(_bias_kernel, n_heads=n_heads),
        in_specs=[pl.BlockSpec((CHUNK, KEYS_PAD), lambda: (0, 0)),
                  pl.BlockSpec(memory_space=pltpu.SMEM)],
        out_specs=pl.BlockSpec((n_heads // 2, CHUNK, 2 * KEYS_PAD), lambda: (0, 0, 0)),
        out_shape=jax.ShapeDtypeStruct((n_heads // 2, CHUNK, 2 * KEYS_PAD), F32),
        name="pair_bias",
    )(bkt, rel_bias)


def _mixer_kernel(*refs, nc, has_past, d_model, d_inner, n_heads, n_ssm_heads, cols, n_blocks, blocks_per_seq):
    t = nc * CHUNK
    conv_ch = d_inner + 2 * SSM_GROUPS * D_STATE
    gw = d_inner // SSM_GROUPS
    n_pairs = n_heads // 2
    kvw = N_KV_HEADS * LANES
    it = iter(refs)
    x_ref, xb_ref, mod_ref, modb_ref = [next(it) for _ in range(4)]
    if has_past:
        conv0_ref, ssm0_ref, k0_ref, v0_ref = next(it), next(it), next(it), next(it)
    (nmg_ref, wall_ref, convw_ref, convb_ref, dtb_ref, alog_ref, dskip_ref, ssmg_ref, qg_ref, kg_ref,
     sinks_ref, bias_ref, e_ref, wbs_ref, wba_ref, wo_ref) = [next(it) for _ in range(16)]
    xo_ref, convn_ref, ssmn_ref, kn_ref, vn_ref = [next(it) for _ in range(5)]
    (h2_s, cb_s, z2_s, xs2_s, bc2_s, dt2_s, q2_s, klo2_s, khi2_s, vlo2_s, vhi2_s, y_s, o_s, st_s,
     xlo_s, xhi_s, xt_s, eae_s, edl_s, mp_s, yin_s, sc_s, pe_s, rd_s, knat_s, vnat_s, ae_s, mix_s) = list(it)

    step = pl.program_id(0)
    par = step & 1
    s_idx = lax.rem(jnp.minimum(step, n_blocks - 1), blocks_per_seq)
    sb_idx = lax.rem(jnp.maximum(step - 1, 0), blocks_per_seq)
    hand_over = (h2_s, z2_s, xs2_s, bc2_s, dt2_s, q2_s, klo2_s, khi2_s, vlo2_s, vhi2_s)
    h_s, z_s, xs_s, bc_s, dt_s, q_s, klo_s, khi_s, vlo_s, vhi_s = [r.at[par] for r in hand_over]
    h_b, z_b, xs_b, bc_b, dt_b, q_b, klo_b, khi_b, vlo_b, vhi_b = [r.at[1 - par] for r in hand_over]

    lane = lax.broadcasted_iota(jnp.int32, (CHUNK, LANES), 1)
    row = lax.broadcasted_iota(jnp.int32, (CHUNK, LANES), 0)
    causal = row >= (lane & (CHUNK - 1))
    diag = row == (lane & (CHUNK - 1))
    lowk = lax.broadcasted_iota(jnp.int32, (WINDOW, LANES), 1) < HEAD_DIM
    lowf = jnp.where(lax.broadcasted_iota(jnp.int32, (1, LANES), 1) < HEAD_DIM, 1.0, 0.0)
    highf = 1.0 - lowf

    @pl.when(step == 0)
    def _():
        for r in hand_over:
            r[1] = jnp.zeros(r.shape[1:], r.dtype)

    @pl.when(sb_idx == 0)
    def _():
        if has_past:
            st_s[...] = ssm0_ref[0].T
        else:
            st_s[...] = jnp.zeros(st_s.shape, F32)

    if blocks_per_seq > 1:
        @pl.when(s_idx > 0)
        def _():
            for buf_a, buf_b in ((klo_s, klo_b), (khi_s, khi_b), (vlo_s, vlo_b), (vhi_s, vhi_b)):
                buf_a[0:WINDOW, :] = buf_b[t:t + WINDOW, :]

    @pl.when(s_idx == 0)
    def _():
        if has_past:
            cb_s[0:SUBLANES, :] = conv0_ref[0]
            for src_ref, lo_s, hi_s in ((k0_ref, klo_s, khi_s), (v0_ref, vlo_s, vhi_s)):
                for j in range(N_KV_HEADS // 2):
                    nat = src_ref[0, :, j * LANES:(j + 1) * LANES]
                    rol = pltpu.roll(nat, HEAD_DIM, 1)
                    a, b = 2 * j * LANES, (2 * j + 1) * LANES
                    lo_s[0:WINDOW, a:a + LANES] = jnp.where(lowk, nat, 0.0).astype(BF16)
                    hi_s[0:WINDOW, a:a + LANES] = jnp.where(lowk, 0.0, rol).astype(BF16)
                    lo_s[0:WINDOW, b:b + LANES] = jnp.where(lowk, rol, 0.0).astype(BF16)
                    hi_s[0:WINDOW, b:b + LANES] = jnp.where(lowk, 0.0, nat).astype(BF16)
        else:
            cb_s[0:SUBLANES, :] = jnp.zeros((SUBLANES, conv_ch), F32)
            for buf in (klo_s, khi_s, vlo_s, vhi_s):
                buf[0:WINDOW, :] = jnp.zeros((WINDOW, kvw), BF16)

    blk = 512
    a_mxu, a_vpu, b_first, b_front, b_merge = [], [], [], [], []

    def a_norm():
        x = x_ref[0]
        shift, scale = mod_ref[0, 0:1, :], mod_ref[0, 1:2, :]
        h_s[...] = (x * _rms_scale(x) * nmg_ref[...] * (1.0 + scale) + shift).astype(BF16)
    a_mxu.append((150, a_norm))

    def a_xbc(j):
        cb_s[SUBLANES:SUBLANES + t, j:j + blk] = _dot(h_s[...], wall_ref[:, cols["xbc"] + j:cols["xbc"] + j + blk])
    for j in range(0, conv_ch, blk):
        a_mxu.append((256, functools.partial(a_xbc, j)))

    def a_dt():
        dt_s[...] = _dot(h_s[...], wall_ref[:, cols["dt"]:cols["dt"] + LANES])
    a_mxu.append((30, a_dt))

    def head_rms(v):
        v2 = v * v
        sa = jnp.sum(v2 * lowf, axis=-1, keepdims=True)
        sb = jnp.sum(v2 * highf, axis=-1, keepdims=True)
        return lax.rsqrt((sa * lowf + sb * highf) * (1.0 / HEAD_DIM) + EPS)

    def a_q(p2):
        qv = _dot(h_s[...], wall_ref[:, cols["q"] + p2 * 2 * LANES:cols["q"] + (p2 + 1) * 2 * LANES])
        for i in range(2):
            v = qv[:, i * LANES:(i + 1) * LANES]
            q_s[:, (2 * p2 + i) * LANES:(2 * p2 + i + 1) * LANES] = (v * head_rms(v) * qg_ref[...]).astype(BF16)
    for p2 in range(n_pairs // 2):
        a_mxu.append((160, functools.partial(a_q, p2)))

    def a_kv():
        nkv = N_KV_HEADS * HEAD_DIM
        kd = _dot(h_s[...], wall_ref[:, cols["k"]:cols["k"] + nkv])
        vd = _dot(h_s[...], wall_ref[:, cols["v"]:cols["v"] + nkv])
        for j in range(N_KV_HEADS // 2):
            ls = slice(j * LANES, (j + 1) * LANES)
            kx = kd[:, ls]
            kn = kx * head_rms(kx) * kg_ref[...]
            vx = vd[:, ls]
            knat_s[:, ls] = kn
            vnat_s[:, ls] = vx
            for val, lo_s, hi_s in ((kn, klo_s, khi_s), (vx, vlo_s, vhi_s)):
                rol = pltpu.roll(val, HEAD_DIM, 1)
                a, b = 2 * j * LANES, (2 * j + 1) * LANES
                lo_s[WINDOW:WINDOW + t, a:a + LANES] = (val * lowf).astype(BF16)
                hi_s[WINDOW:WINDOW + t, a:a + LANES] = (rol * highf).astype(BF16)
                lo_s[WINDOW:WINDOW + t, b:b + LANES] = (rol * lowf).astype(BF16)
                hi_s[WINDOW:WINDOW + t, b:b + LANES] = (val * highf).astype(BF16)
    a_mxu.append((300, a_kv))

    def a_z(j):
        zz = _dot(h_s[...], wall_ref[:, cols["z"] + j:cols["z"] + j + blk])
        z_s[:, j:j + blk] = zz * _sigmoid(zz)
    for j in range(0, d_inner, blk):
        a_mxu.append((260, functools.partial(a_z, j)))

    def a_conv(j):
        acc = cb_s[SUBLANES:SUBLANES + t, j:j + blk] * convw_ref[CONV_K - 1:CONV_K, j:j + blk]
        for k in range(CONV_K - 1):
            o = SUBLANES - (CONV_K - 1) + k
            acc = acc + cb_s[o:o + t, j:j + blk] * convw_ref[k:k + 1, j:j + blk]
        acc = acc + convb_ref[:, j:j + blk]
        u = acc * _sigmoid(acc)
        if j < d_inner:
            xs_s[:, j:j + blk] = u
        else:
            bc_s[:, j - d_inner:j - d_inner + blk] = u
    for j in range(0, conv_ch, blk):
        a_vpu.append((300, functools.partial(a_conv, j)))

    def a_tail():
        cb_s[0:SUBLANES, :] = cb_s[t:t + SUBLANES, :]
    if blocks_per_seq > 1:
        a_vpu.append((10, a_tail))

    sv = {}
    npg = gw // LANES

    def b_prep():
        a_row = -jnp.exp(alog_ref[...])
        head_lanes = jnp.where(lax.broadcasted_iota(jnp.int32, (1, LANES), 1) < n_ssm_heads, 1.0, 0.0)
        xv = dt_b[...] + dtb_ref[...]
        dt = (jnp.maximum(xv, 0.0) + jnp.log1p(jnp.exp(-jnp.abs(xv)))) * head_lanes
        da = dt * a_row
        d1 = da.astype(BF16)
        r1 = da - d1.astype(F32)
        d2 = r1.astype(BF16)
        d3 = (r1 - d2.astype(F32)).astype(BF16)
        chunk_shift = CHUNK.bit_length() - 1
        br = lax.broadcasted_iota(jnp.int32, (t, t), 0)
        bc = lax.broadcasted_iota(jnp.int32, (t, t), 1)
        same_chunk = jnp.right_shift(br, chunk_shift) == jnp.right_shift(bc, chunk_shift)
        tril = jnp.where((br >= bc) & same_chunk, 1.0, 0.0).astype(BF16)
        acum = _dot(tril, d1) + _dot(tril, d2) + _dot(tril, d3)

        def pack_hi_lo(v):
            hi = v.astype(BF16).astype(F32)
            return (hi + pltpu.roll(v - hi, n_ssm_heads, 1)).astype(BF16)

        sv["pdt"], sv["pac"] = pack_hi_lo(dt), pack_hi_lo(acum)
        lowf_g = jnp.concatenate([lowf] * npg, axis=1)
        sv["lowf_g"], sv["highf_g"] = lowf_g, 1.0 - lowf_g
    b_front.append((300, b_prep))

    def b_expand(g):
        gs = slice(g * gw, (g + 1) * gw)
        e_g = e_ref[:, gs]
        xdt = xs_b[:, gs] * _dot(sv["pdt"], e_g)
        ae = _dot(sv["pac"], e_g)
        xlo_s[:, gs] = (xdt * sv["lowf_g"]).astype(BF16)
        xhi_s[:, gs] = (xdt * sv["highf_g"]).astype(BF16)
        eae_s[:, gs] = jnp.exp(ae)
        ae_s[:, gs] = ae
        for c in range(nc):
            rows = slice(c * CHUNK, (c + 1) * CHUNK)
            ae_c = ae[rows]
            ae_l = ae_c[CHUNK - 1:CHUNK, :]
            xt_s[rows, gs] = (xdt[rows] * jnp.exp(ae_l - ae_c)).astype(BF16)
            edl_s[c * SUBLANES:(c + 1) * SUBLANES, gs] = jnp.broadcast_to(jnp.exp(ae_l), (SUBLANES, gw))

    def b_decay(g, c):
        rows = slice(c * CHUNK, (c + 1) * CHUNK)
        bb = bc_b[rows, g * D_STATE:(g + 1) * D_STATE].astype(BF16)
        cc = bc_b[rows, (SSM_GROUPS + g) * D_STATE:(SSM_GROUPS + g + 1) * D_STATE].astype(BF16)
        cb2 = _dot_nt(cc, jnp.concatenate([bb, bb], axis=0))
        for i in range(npg):
            acol = ae_s[rows, g * gw + i * LANES:g * gw + (i + 1) * LANES]
            arow = jnp.sum(jnp.where(diag, acol, 0.0), axis=0, keepdims=True)
            dec = jnp.exp(jnp.where(causal, acol - arow, NEG))
            mp_s[(c * SSM_GROUPS + g) * npg + i] = (cb2 * dec).astype(BF16)

    def b_yin(g, c):
        rows = slice(c * CHUNK, (c + 1) * CHUNK)
        zblk = jnp.zeros((2 * CHUNK, LANES), BF16)
        for i in range(0, npg, 2):
            base = (c * SSM_GROUPS + g) * npg + i
            l0 = slice(g * gw + i * LANES, g * gw + (i + 1) * LANES)
            l1 = slice(g * gw + (i + 1) * LANES, g * gw + (i + 2) * LANES)
            x0 = jnp.concatenate([xlo_s[rows, l0], xhi_s[rows, l0]], axis=0)
            x1 = jnp.concatenate([xlo_s[rows, l1], xhi_s[rows, l1]], axis=0)
            rhs = jnp.concatenate([jnp.concatenate([x0, zblk], axis=1), jnp.concatenate([zblk, x1], axis=1)], axis=0)
            lhs = jnp.concatenate([mp_s[base], mp_s[base + 1]], axis=1)
            yin_s[rows, g * gw + i * LANES:g * gw + (i + 2) * LANES] = _dot(lhs, rhs)

    for g in range(SSM_GROUPS):
        b_front.append((400, functools.partial(b_expand, g)))
        for c in range(nc):
            b_front.append((150, functools.partial(b_decay, g, c)))
    for g in range(SSM_GROUPS):
        for c in range(nc):
            b_front.append((60, functools.partial(b_yin, g, c)))

    ppk = n_pairs // N_KV_HEADS

    def b_scores(c, kv):
        zpad = jnp.zeros((KEYS_PAD - KEYS, LANES), BF16)
        rows = slice(c * CHUNK, (c + 1) * CHUNK)
        krows = slice(c * CHUNK, c * CHUNK + KEYS)
        c0 = kv * LANES
        kbd = jnp.concatenate([klo_b[krows, c0:c0 + LANES], zpad, khi_b[krows, c0:c0 + LANES], zpad], axis=0)
        for pp in range(ppk):
            p = kv * ppk + pp
            sc = _dot_nt(q_b[rows, p * LANES:(p + 1) * LANES], kbd) + bias_ref[p]
            if not has_past:
                lane_s = lax.broadcasted_iota(jnp.int32, (CHUNK, 2 * KEYS_PAD), 1) & (KEYS_PAD - 1)
                sc = jnp.where(lane_s >= WINDOW - CHUNK * (sb_idx * nc + c), sc, NEG)
            sc_s[c * n_pairs + p] = sc

    def b_softmax(c, p):
        es, rs = [], []
        for hh in range(2):
            sh = sc_s[c * n_pairs + p, :, hh * KEYS_PAD:(hh + 1) * KEYS_PAD]
            sink = sinks_ref[2 * p + hh]
            m = jnp.maximum(jnp.max(sh, axis=-1, keepdims=True), sink)
            e = jnp.exp(sh - m)
            den = jnp.sum(e, axis=-1, keepdims=True) + jnp.exp(sink - m)
            es.append(e.astype(BF16))
            rs.append(1.0 / den)
        pe_s[c * n_pairs + p] = jnp.concatenate(es, axis=1)
        rd_s[c * n_pairs + p] = rs[0] * lowf + rs[1] * highf

    def b_pv(c, kv):
        zpad = jnp.zeros((KEYS_PAD - KEYS, LANES), BF16)
        rows = slice(c * CHUNK, (c + 1) * CHUNK)
        krows = slice(c * CHUNK, c * CHUNK + KEYS)
        c0 = kv * LANES
        vbd = jnp.concatenate([vlo_b[krows, c0:c0 + LANES], zpad, vhi_b[krows, c0:c0 + LANES], zpad], axis=0)
        for pp in range(ppk):
            p = kv * ppk + pp
            o = _dot(pe_s[c * n_pairs + p], vbd) * rd_s[c * n_pairs + p]
            o_s[rows, p * LANES:(p + 1) * LANES] = o.astype(BF16)

    for c in range(nc):
        for kv in range(N_KV_HEADS):
            b_first.append((70, functools.partial(b_scores, c, kv)))
    for c in range(nc):
        for p in range(n_pairs):
            b_front.append((110, functools.partial(b_softmax, c, p)))
    for c in range(nc):
        for kv in range(N_KV_HEADS):
            b_front.append((70, functools.partial(b_pv, c, kv)))

    def b_scan(c, g):
        rows = slice(c * CHUNK, (c + 1) * CHUNK)
        gs = slice(g * gw, (g + 1) * gw)
        bb = bc_b[rows, g * D_STATE:(g + 1) * D_STATE].astype(BF16)
        cc = bc_b[rows, (SSM_GROUPS + g) * D_STATE:(SSM_GROUPS + g + 1) * D_STATE].astype(BF16)
        st = st_s[:, gs]
        y = yin_s[rows, gs] + _dot(cc, st.astype(BF16)) * eae_s[rows, gs] + xs_b[rows, gs] * dskip_ref[:, gs]
        y = y * z_b[rows, gs]
        y = y * _rms_scale(y) * ssmg_ref[:, gs]
        y_s[rows, gs] = y.astype(BF16)
        st_s[:, gs] = st * edl_s[c * SUBLANES:c * SUBLANES + 1, gs] + _dot_tn(bb, xt_s[rows, gs])
    for c in range(nc):
        for g in range(SSM_GROUPS):
            b_front.append((160, functools.partial(b_scan, c, g)))

    def b_branch(name, src_s, w_ref, first):
        gate = _sigmoid(_dot(h_b[...], wall_ref[:, cols[name]:cols[name] + d_model]))
        part = gate * _dot(src_s[...], w_ref[...])
        mix_s[...] = part if first else mix_s[...] + part
    b_merge.append((1500, functools.partial(b_branch, "gs", y_s, wbs_ref, True)))
    b_merge.append((1000, functools.partial(b_branch, "ga", o_s, wba_ref, False)))

    def b_out():
        xo_ref[0] = xb_ref[0] + modb_ref[0, 2:3, :] * _dot(mix_s[...].astype(BF16), wo_ref[...])
    b_merge.append((500, b_out))

    _interleave(a_mxu, b_first + b_front)
    _interleave(a_vpu, b_merge)

    @pl.when((s_idx == blocks_per_seq - 1) & (step < n_blocks))
    def _():
        convn_ref[0] = cb_s[t:t + SUBLANES, :]
        kr = kn_ref.shape[1]
        kn_ref[0] = knat_s[t - kr:t, :]
        vn_ref[0] = vnat_s[t - kr:t, :]

    @pl.when(sb_idx == blocks_per_seq - 1)
    def _():
        ssmn_ref[0] = st_s[...].T


def _const_spec(shape, single_buffer=False):
    idx = (0,) * len(shape)
    if single_buffer:
        return pl.BlockSpec(shape, lambda *_: idx, pipeline_mode=pl.Buffered(1))
    return pl.BlockSpec(shape, lambda *_: idx)


def _mixer_layer(x, mod, past, lw, pair_bias, e_mat, *, nc):
    bsz, seq, d_model = x.shape
    t = nc * CHUNK
    has_past = past is not None
    d_inner = lw["wbs"].shape[0]
    n_heads = lw["wba"].shape[0] // HEAD_DIM
    conv_ch = d_inner + 2 * SSM_GROUPS * D_STATE
    kvw = N_KV_HEADS * LANES
    kv_rows = seq if has_past else WINDOW
    assert seq % t == 0 and (has_past and seq == t or not has_past and t >= WINDOW)

    bps = seq // t
    n_blocks = bsz * bps
    blk_a = lambda i: jnp.minimum(i, n_blocks - 1)
    blk_b = lambda i: jnp.maximum(i - 1, 0)

    def x_spec(blk):
        return pl.BlockSpec((1, t, d_model), lambda i: (blk(i) // bps, blk(i) % bps, 0))

    def per_seq(blk, *shape):
        return pl.BlockSpec((1,) + shape, lambda i: (blk(i) // bps,) + (0,) * len(shape))

    in_specs = [x_spec(blk_a), x_spec(blk_b), per_seq(blk_a, 6, d_model), per_seq(blk_b, 6, d_model)]
    args = [x, x, mod, mod]
    if has_past:
        in_specs += [per_seq(blk_a, SUBLANES, conv_ch), per_seq(blk_b, d_inner, D_STATE),
                     per_seq(blk_a, WINDOW, N_KV_HEADS * HEAD_DIM), per_seq(blk_a, WINDOW, N_KV_HEADS * HEAD_DIM)]
        args += list(past)
    consts = [lw["nmg"], lw["wall"], lw["convw"], lw["convb"], lw["dtb"], lw["alog"], lw["dskip"], lw["ssmg"],
              lw["qg"], lw["kg"]]
    in_specs += [_const_spec(a.shape, single_buffer=a.size > (1 << 20)) for a in consts]
    in_specs.append(pl.BlockSpec(memory_space=pltpu.SMEM))
    tail = [pair_bias, e_mat, lw["wbs"], lw["wba"], lw["wo"]]
    in_specs += [_const_spec(a.shape, single_buffer=a.size > (1 << 18)) for a in tail]
    args += consts + [lw["sinks"]] + tail

    out_shape = (jax.ShapeDtypeStruct(x.shape, F32),
                 jax.ShapeDtypeStruct((bsz, SUBLANES, conv_ch), F32),
                 jax.ShapeDtypeStruct((bsz, d_inner, D_STATE), F32),
                 jax.ShapeDtypeStruct((bsz, kv_rows, N_KV_HEADS * HEAD_DIM), F32),
                 jax.ShapeDtypeStruct((bsz, kv_rows, N_KV_HEADS * HEAD_DIM), F32))
    out_specs = (x_spec(blk_b),
                 per_seq(blk_a, SUBLANES, conv_ch), per_seq(blk_b, d_inner, D_STATE),
                 per_seq(blk_a, kv_rows, N_KV_HEADS * HEAD_DIM), per_seq(blk_a, kv_rows, N_KV_HEADS * HEAD_DIM))
    scratch = [
        pltpu.VMEM((2, t, d_model), BF16),
        pltpu.VMEM((SUBLANES + t, conv_ch), F32),
        pltpu.VMEM((2, t, d_inner), F32),
        pltpu.VMEM((2, t, d_inner), F32),
        pltpu.VMEM((2, t, 2 * SSM_GROUPS * D_STATE), F32),
        pltpu.VMEM((2, t, LANES), F32),
        pltpu.VMEM((2, t, d_model), BF16),
        pltpu.VMEM((2, WINDOW + t, kvw), BF16),
        pltpu.VMEM((2, WINDOW + t, kvw), BF16),
        pltpu.VMEM((2, WINDOW + t, kvw), BF16),
        pltpu.VMEM((2, WINDOW + t, kvw), BF16),
        pltpu.VMEM((t, d_inner), BF16),
        pltpu.VMEM((t, d_model), BF16),
        pltpu.VMEM((D_STATE, d_inner), F32),
        pltpu.VMEM((t, d_inner), BF16),
        pltpu.VMEM((t, d_inner), BF16),
        pltpu.VMEM((t, d_inner), BF16),
        pltpu.VMEM((t, d_inner), F32),
        pltpu.VMEM((nc * SUBLANES, d_inner), F32),
        pltpu.VMEM((nc * d_inner // LANES, CHUNK, LANES), BF16),
        pltpu.VMEM((t, d_inner), F32),
        pltpu.VMEM((nc * n_heads // 2, CHUNK, 2 * KEYS_PAD), F32),
        pltpu.VMEM((nc * n_heads // 2, CHUNK, 2 * KEYS_PAD), BF16),
        pltpu.VMEM((nc * n_heads // 2, CHUNK, LANES), F32),
        pltpu.VMEM((t, N_KV_HEADS * HEAD_DIM), F32),
        pltpu.VMEM((t, N_KV_HEADS * HEAD_DIM), F32),
        pltpu.VMEM((t, d_inner), F32),
        pltpu.VMEM((t, d_model), F32),
    ]
    kern = functools.partial(_mixer_kernel, nc=nc, has_past=has_past, d_model=d_model, d_inner=d_inner,
                             n_heads=n_heads, n_ssm_heads=d_inner // SSM_HEAD_DIM, cols=lw["cols"],
                             n_blocks=n_blocks, blocks_per_seq=bps)
    return pl.pallas_call(
        kern, grid=(n_blocks + 1,), in_specs=in_specs, out_specs=out_specs, out_shape=out_shape,
        scratch_shapes=scratch,
        compiler_params=pltpu.CompilerParams(dimension_semantics=("arbitrary",),
                                             vmem_limit_bytes=VMEM_LIMIT),
        name="mixer_past" if has_past else "mixer_prompt",
    )(*args)


def _ffn_kernel(x_ref, mod_ref, g_ref, wg_ref, wu_ref, wd_ref, o_ref):
    x = x_ref[0]
    shift, scale, gate_f = mod_ref[0, 3:4, :], mod_ref[0, 4:5, :], mod_ref[0, 5:6, :]
    h = (x * _rms_scale(x) * g_ref[...] * (1.0 + scale) + shift).astype(BF16)
    gate = _dot(h, wg_ref[...])
    up = _dot(h, wu_ref[...])
    act = (gate * _sigmoid(gate) * up).astype(BF16)
    o_ref[0] = x + gate_f * _dot(act, wd_ref[...])


def _ffn_layer(x, mod, lw, *, tm):
    bsz, seq, d_model = x.shape
    assert seq % tm == 0
    xspec = pl.BlockSpec((1, tm, d_model), lambda b, s: (b, s, 0))
    return pl.pallas_call(
        _ffn_kernel, grid=(bsz, seq // tm),
        in_specs=[xspec, pl.BlockSpec((1, 6, d_model), lambda b, s: (b, 0, 0)),
                  _const_spec(lw["nfg"].shape), _const_spec(lw["wg"].shape, True),
                  _const_spec(lw["wu"].shape, True), _const_spec(lw["wd"].shape, True)],
        out_specs=xspec, out_shape=jax.ShapeDtypeStruct(x.shape, F32),
        compiler_params=pltpu.CompilerParams(dimension_semantics=("arbitrary", "arbitrary"),
                                             vmem_limit_bytes=VMEM_LIMIT),
        name="ffn",
    )(x, mod, lw["nfg"], lw["wg"], lw["wu"], lw["wd"])


def _layer_weights(l, p):
    d_model = p["w_out"].shape[1]
    d_inner = p["w_br_ssm"].shape[1]
    n_ssm_heads = p["dt_bias"].shape[1]
    d_ff = p["w_down"].shape[1]
    nq = p["w_br_attn"].shape[1]
    nkv = N_KV_HEADS * HEAD_DIM
    conv_ch = d_inner + 2 * SSM_GROUPS * D_STATE
    wi = p["w_in"][l]
    bounds = [0, d_inner, d_inner + conv_ch, d_inner + conv_ch + n_ssm_heads]
    bounds += [bounds[-1] + nq, bounds[-1] + nq + nkv, bounds[-1] + nq + 2 * nkv]
    bounds += [bounds[-1] + d_model, bounds[-1] + 2 * d_model]
    z, xbc, dtc, q, k, v, gs, ga = [wi[:, a:b] for a, b in zip(bounds[:-1], bounds[1:])]
    pieces = [("z", z), ("xbc", xbc), ("dt", jnp.pad(dtc, ((0, 0), (0, LANES - n_ssm_heads)))),
              ("q", q), ("k", k), ("v", v), ("gs", gs), ("ga", ga)]
    cols, off = {}, 0
    for name, w in pieces:
        cols[name] = off
        off += w.shape[1]
    rep = lambda vec: jnp.repeat(vec, SSM_HEAD_DIM)[None, :]
    padl = lambda vec: jnp.pad(vec, (0, LANES - n_ssm_heads))[None, :]
    return dict(
        cols=cols,
        wall=jnp.concatenate([w for _, w in pieces], axis=1).astype(BF16),
        nmg=p["norm_mix_g"][l][None, :], nfg=p["norm_ffn_g"][l][None, :],
        convw=p["conv_w"][l], convb=p["conv_b"][l][None, :],
        dtb=padl(p["dt_bias"][l]), alog=padl(p["a_log"][l]), dskip=rep(p["d_skip"][l]),
        ssmg=p["ssm_norm_g"][l][None, :],
        qg=jnp.tile(p["q_norm_g"][l] * (HEAD_DIM ** -0.5), 2)[None, :], kg=jnp.tile(p["k_norm_g"][l], 2)[None, :],
        sinks=p["sinks"][l],
        wbs=p["w_br_ssm"][l].astype(BF16), wba=p["w_br_attn"][l].astype(BF16), wo=p["w_out"][l].astype(BF16),
        wg=p["w_gate_up"][l][:, :d_ff].astype(BF16), wu=p["w_gate_up"][l][:, d_ff:].astype(BF16),
        wd=p["w_down"][l].astype(BF16),
    )


def _expand_matrix(n_ssm_heads):
    assert 3 * n_ssm_heads <= LANES
    r = jnp.arange(LANES)[:, None]
    src = (r < 2 * n_ssm_heads) | (r >= LANES - n_ssm_heads)
    c = jnp.arange(n_ssm_heads * SSM_HEAD_DIM)[None, :] // SSM_HEAD_DIM
    return (src & ((r % n_ssm_heads) == c)).astype(BF16)


def kernel(x_prompt, x_sample, cache_k, cache_v, state_conv, state_ssm, c_prompt, c_sample, rel_bias, ada_w, ada_b, norm_mix_g, norm_ffn_g, w_in, conv_w, conv_b, dt_bias, a_log, d_skip, ssm_norm_g, q_norm_g, k_norm_g, sinks, w_br_ssm, w_br_attn, w_out, w_gate_up, w_down):
    p = dict(norm_mix_g=norm_mix_g, norm_ffn_g=norm_ffn_g, w_in=w_in, conv_w=conv_w, conv_b=conv_b,
             dt_bias=dt_bias, a_log=a_log, d_skip=d_skip, ssm_norm_g=ssm_norm_g, q_norm_g=q_norm_g,
             k_norm_g=k_norm_g, sinks=sinks, w_br_ssm=w_br_ssm, w_br_attn=w_br_attn, w_out=w_out,
             w_gate_up=w_gate_up, w_down=w_down)
    depth = w_in.shape[0]
    bp, seq_p, d_model = x_prompt.shape
    bs, seq_s, _ = x_sample.shape
    n_ssm_heads, ssm_p, d_state = state_ssm.shape[2:]
    d_inner = n_ssm_heads * ssm_p
    nkv = N_KV_HEADS * HEAD_DIM

    mod = _ada_mod(jnp.concatenate([c_prompt, c_sample], axis=0), ada_w, ada_b)
    mod = mod.reshape(depth, bp + bs, 6, d_model)
    pair_bias = _pair_bias(rel_bias)
    e_mat = _expand_matrix(n_ssm_heads)

    xp, xs = x_prompt, x_sample
    outs = [[] for _ in range(8)]
    for l in range(depth):
        lw = _layer_weights(l, p)
        mod_p, mod_s = mod[l, :bp], mod[l, bp:]
        past = (jnp.pad(state_conv[l], ((0, 0), (SUBLANES - (CONV_K - 1), 0), (0, 0))),
                state_ssm[l].reshape(bs, d_inner, d_state),
                cache_k[l].reshape(bs, WINDOW, nkv), cache_v[l].reshape(bs, WINDOW, nkv))
        xp, cvp, ssp, kp, vp = _mixer_layer(xp, mod_p, None, lw, pair_bias, e_mat, nc=2)
        xs, cvs, sss, ksn, vsn = _mixer_layer(xs, mod_s, past, lw, pair_bias, e_mat, nc=1)
        xp = _ffn_layer(xp, mod_p, lw, tm=min(512, seq_p))
        xs = _ffn_layer(xs, mod_s, lw, tm=seq_s)
        tail = SUBLANES - (CONV_K - 1)
        for lst, val in zip(outs, (
                cvp[:, tail:], cvs[:, tail:],
                ssp.reshape(bp, n_ssm_heads, ssm_p, d_state), sss.reshape(bs, n_ssm_heads, ssm_p, d_state),
                kp.reshape(bp, WINDOW, N_KV_HEADS, HEAD_DIM), ksn.reshape(bs, seq_s, N_KV_HEADS, HEAD_DIM),
                vp.reshape(bp, WINDOW, N_KV_HEADS, HEAD_DIM), vsn.reshape(bs, seq_s, N_KV_HEADS, HEAD_DIM))):
            lst.append(val)
    return (xp, xs) + tuple(jnp.stack(o) for o in outs)
```

```python
import functools
import math

import jax
import jax.numpy as jnp
from jax import lax
from jax.experimental import pallas as pl
from jax.experimental.pallas import tpu as pltpu

F32 = jnp.float32
BF16 = jnp.bfloat16

CHUNK = 64
EPS = 1e-6
SSM_GROUPS = 4
D_STATE = 128
SSM_HEAD_DIM = 64
CONV_K = 4
HEAD_DIM = 64
N_KV_HEADS = 4
WINDOW = 128
N_BUCKETS = 32
MAX_DISTANCE = 128

LANES = 128
SUBLANES = 8
KEYS = WINDOW + CHUNK
KEYS_PAD = 256
NEG = -1e30
VMEM_LIMIT = 56 * 1024 * 1024
FFN_ROWS = 512


def _dot(a, b):
    return jnp.dot(a, b, preferred_element_type=F32)


def _dot_nt(a, b):
    return lax.dot_general(a, b, (((1,), (1,)), ((), ())), preferred_element_type=F32)


def _dot_tn(a, b):
    return lax.dot_general(a, b, (((0,), (0,)), ((), ())), preferred_element_type=F32)


def _sigmoid(x):
    return 1.0 / (1.0 + jnp.exp(-x))


def _interleave(*streams):
    totals = [float(sum(w for w, _ in st)) or 1.0 for st in streams]
    pos, done = [0] * len(streams), [0.0] * len(streams)
    while True:
        live = [k for k in range(len(streams)) if pos[k] < len(streams[k])]
        if not live:
            return
        k = min(live, key=lambda k: done[k] / totals[k])
        w, fn = streams[k][pos[k]]
        fn()
        pos[k] += 1
        done[k] += w


def _rms_scale(x):
    return lax.rsqrt(jnp.mean(x * x, axis=-1, keepdims=True) + EPS)


def _ada_kernel(c_ref, w_ref, b_ref, o_ref):
    c = c_ref[...]
    s = (c * _sigmoid(c)).astype(BF16)
    o_ref[0] = _dot(s, w_ref[0].astype(BF16)) + b_ref[0]


def _ada_mod(c_all, ada_w, ada_b):
    depth, d, six_d = ada_w.shape
    n = c_all.shape[0]
    nblk = six_d // d
    return pl.pallas_call(
        _ada_kernel,
        grid=(depth, nblk),
        in_specs=[
            pl.BlockSpec((n, d), lambda l, j: (0, 0)),
            pl.BlockSpec((1, d, d), lambda l, j: (l, 0, j)),
            pl.BlockSpec((1, 1, d), lambda l, j: (l, 0, j)),
        ],
        out_specs=pl.BlockSpec((1, n, d), lambda l, j: (l, 0, j)),
        out_shape=jax.ShapeDtypeStruct((depth, n, six_d), F32),
        name="ada_mod",
    )(c_all, ada_w, ada_b.reshape(depth, 1, six_d))


def _bias_kernel(bkt_ref, rb_ref, o_ref, *, n_heads):
    bkt = bkt_ref[...]
    for h in range(n_heads):
        def body(b, acc, h=h):
            return jnp.where(bkt == b, rb_ref[b, h], acc)
        acc = lax.fori_loop(0, N_BUCKETS, body, jnp.full(bkt.shape, NEG, F32))
        o_ref[h // 2, :, (h % 2) * KEYS_PAD:(h % 2 + 1) * KEYS_PAD] = acc


def _t5_bucket(rel):
    n = -rel
    half = N_BUCKETS // 2
    max_exact = half // 2
    ret = jnp.where(n < 0, half, 0)
    n = jnp.abs(n)
    nf = jnp.maximum(n, 1).astype(F32)
    large = max_exact + (jnp.log(nf / max_exact) / math.log(MAX_DISTANCE / max_exact)
                         * (half - max_exact)).astype(jnp.int32)
    large = jnp.minimum(large, half - 1)
    return ret + jnp.where(n < max_exact, n, large)


def _pair_bias(rel_bias):
    n_heads = rel_bias.shape[1]
    rel = jnp.arange(KEYS)[None, :] - WINDOW - jnp.arange(CHUNK)[:, None]
    bkt = jnp.pad(_t5_bucket(rel).astype(jnp.int32), ((0, 0), (0, KEYS_PAD - KEYS)), constant_values=-1)
    return pl.pallas_call(
        functools.partial(_bias_kernel, n_heads=n_heads),
        in_specs=[pl.BlockSpec((CHUNK, KEYS_PAD), lambda: (0, 0)),
                  pl.BlockSpec(memory_space=pltpu.SMEM)],
        out_specs=pl.BlockSpec((n_heads // 2, CHUNK, 2 * KEYS_PAD), lambda: (0, 0, 0)),
        out_shape=jax.ShapeDtypeStruct((n_heads // 2, CHUNK, 2 * KEYS_PAD), F32),
        name="pair_bias",
    )(bkt, rel_bias)


def _mixer_kernel(*refs, nc, has_past, d_model, d_inner, n_heads, n_ssm_heads, cols):
    t = nc * CHUNK
    conv_ch = d_inner + 2 * SSM_GROUPS * D_STATE
    gw = d_inner // SSM_GROUPS
    n_pairs = n_heads // 2
    kvw = N_KV_HEADS * LANES
    it = iter(refs)
    x_ref, mod_ref = next(it), next(it)
    if has_past:
        conv0_ref, ssm0_ref, k0_ref, v0_ref = next(it), next(it), next(it), next(it)
    (nmg_ref, wall_ref, convw_ref, convb_ref, dtb_ref, alog_ref, dskip_ref, ssmg_ref, qg_ref, kg_ref,
     sinks_ref, bias_ref, e_ref) = [next(it) for _ in range(13)]
    wall_ref = wall_ref.at[0]
    xo_ref, convn_ref, ssmn_ref, kn_ref, vn_ref = [next(it) for _ in range(5)]
    (h_s, cb_s, z_s, xs_s, bc_s, dt_s, q_s, klo_s, khi_s, vlo_s, vhi_s, y_s, o_s, st_s,
     xlo_s, xhi_s, xt_s, eae_s, edl_s, mp_s, yin_s, sc_s, pe_s, rd_s, knat_s, vnat_s, ae_s, mix_s,
     gate_s) = list(it)

    s_idx = pl.program_id(1)
    n_steps = pl.num_programs(1)

    lane = lax.broadcasted_iota(jnp.int32, (CHUNK, LANES), 1)
    row = lax.broadcasted_iota(jnp.int32, (CHUNK, LANES), 0)
    causal = row >= (lane & (CHUNK - 1))
    diag = row == (lane & (CHUNK - 1))
    lowk = lax.broadcasted_iota(jnp.int32, (WINDOW, LANES), 1) < HEAD_DIM
    lowf = jnp.where(lax.broadcasted_iota(jnp.int32, (1, LANES), 1) < HEAD_DIM, 1.0, 0.0)
    highf = 1.0 - lowf

    @pl.when(s_idx == 0)
    def _():
        if has_past:
            cb_s[0:SUBLANES, :] = conv0_ref[0]
            st_s[...] = ssm0_ref[0].T
            for src_ref, lo_s, hi_s in ((k0_ref, klo_s, khi_s), (v0_ref, vlo_s, vhi_s)):
                for j in range(N_KV_HEADS // 2):
                    nat = src_ref[0, :, j * LANES:(j + 1) * LANES]
                    rol = pltpu.roll(nat, HEAD_DIM, 1)
                    a, b = 2 * j * LANES, (2 * j + 1) * LANES
                    lo_s[0:WINDOW, a:a + LANES] = jnp.where(lowk, nat, 0.0).astype(BF16)
                    hi_s[0:WINDOW, a:a + LANES] = jnp.where(lowk, 0.0, rol).astype(BF16)
                    lo_s[0:WINDOW, b:b + LANES] = jnp.where(lowk, rol, 0.0).astype(BF16)
                    hi_s[0:WINDOW, b:b + LANES] = jnp.where(lowk, 0.0, nat).astype(BF16)
        else:
            cb_s[0:SUBLANES, :] = jnp.zeros((SUBLANES, conv_ch), F32)
            st_s[...] = jnp.zeros(st_s.shape, F32)
            for buf in (klo_s, khi_s, vlo_s, vhi_s):
                buf[0:WINDOW, :] = jnp.zeros((WINDOW, kvw), BF16)

    blk = 512
    a_mxu, a_late, a_vpu, b_first, b_front, b_scans, b_merge = [], [], [], [], [], [], []
    h_b, z_b, xs_b, bc_b, dt_b, q_b, klo_b, khi_b, vlo_b, vhi_b = (
        h_s, z_s, xs_s, bc_s, dt_s, q_s, klo_s, khi_s, vlo_s, vhi_s)

    def a_norm():
        x = x_ref[0]
        shift, scale = mod_ref[0, 0:1, :], mod_ref[0, 1:2, :]
        h_s[...] = (x * _rms_scale(x) * nmg_ref[...] * (1.0 + scale) + shift).astype(BF16)
    a_mxu.append((150, a_norm))

    def a_xbc(j):
        cb_s[SUBLANES:SUBLANES + t, j:j + blk] = _dot(h_s[...], wall_ref[:, cols["xbc"] + j:cols["xbc"] + j + blk])
    for j in range(0, conv_ch, blk):
        a_mxu.append((256, functools.partial(a_xbc, j)))

    def a_dt():
        dt_s[...] = _dot(h_s[...], wall_ref[:, cols["dt"]:cols["dt"] + LANES])
    a_mxu.append((30, a_dt))

    def head_rms(v):
        v2 = v * v
        sa = jnp.sum(v2 * lowf, axis=-1, keepdims=True)
        sb = jnp.sum(v2 * highf, axis=-1, keepdims=True)
        return lax.rsqrt((sa * lowf + sb * highf) * (1.0 / HEAD_DIM) + EPS)

    def a_q(p2):
        qv = _dot(h_s[...], wall_ref[:, cols["q"] + p2 * 2 * LANES:cols["q"] + (p2 + 1) * 2 * LANES])
        for i in range(2):
            v = qv[:, i * LANES:(i + 1) * LANES]
            q_s[:, (2 * p2 + i) * LANES:(2 * p2 + i + 1) * LANES] = (v * head_rms(v) * qg_ref[...]).astype(BF16)
    for p2 in range(n_pairs // 2):
        a_mxu.append((160, functools.partial(a_q, p2)))

    def a_kv():
        nkv = N_KV_HEADS * HEAD_DIM
        kd = _dot(h_s[...], wall_ref[:, cols["k"]:cols["k"] + nkv])
        vd = _dot(h_s[...], wall_ref[:, cols["v"]:cols["v"] + nkv])
        for j in range(N_KV_HEADS // 2):
            ls = slice(j * LANES, (j + 1) * LANES)
            kx = kd[:, ls]
            kn = kx * head_rms(kx) * kg_ref[...]
            vx = vd[:, ls]
            knat_s[:, ls] = kn
            vnat_s[:, ls] = vx
            for val, lo_s, hi_s in ((kn, klo_s, khi_s), (vx, vlo_s, vhi_s)):
                rol = pltpu.roll(val, HEAD_DIM, 1)
                a, b = 2 * j * LANES, (2 * j + 1) * LANES
                lo_s[WINDOW:WINDOW + t, a:a + LANES] = (val * lowf).astype(BF16)
                hi_s[WINDOW:WINDOW + t, a:a + LANES] = (rol * highf).astype(BF16)
                lo_s[WINDOW:WINDOW + t, b:b + LANES] = (rol * lowf).astype(BF16)
                hi_s[WINDOW:WINDOW + t, b:b + LANES] = (val * highf).astype(BF16)
    a_mxu.append((300, a_kv))

    def a_z(j):
        zz = _dot(h_s[...], wall_ref[:, cols["z"] + j:cols["z"] + j + blk])
        z_s[:, j:j + blk] = zz * _sigmoid(zz)
    for j in range(0, d_inner, blk):
        a_late.append((256, functools.partial(a_z, j)))

    def a_conv(j):
        acc = cb_s[SUBLANES:SUBLANES + t, j:j + blk] * convw_ref[CONV_K - 1:CONV_K, j:j + blk]
        for k in range(CONV_K - 1):
            o = SUBLANES - (CONV_K - 1) + k
            acc = acc + cb_s[o:o + t, j:j + blk] * convw_ref[k:k + 1, j:j + blk]
        acc = acc + convb_ref[:, j:j + blk]
        u = acc * _sigmoid(acc)
        if j < d_inner:
            xs_s[:, j:j + blk] = u
        else:
            bc_s[:, j - d_inner:j - d_inner + blk] = u
    for j in range(0, conv_ch, blk):
        a_vpu.append((300, functools.partial(a_conv, j)))

    def a_tail():
        cb_s[0:SUBLANES, :] = cb_s[t:t + SUBLANES, :]
    if not has_past:
        a_vpu.append((10, a_tail))

    sv = {}
    npg = gw // LANES

    def b_prep():
        a_row = -jnp.exp(alog_ref[...])
        head_lanes = jnp.where(lax.broadcasted_iota(jnp.int32, (1, LANES), 1) < n_ssm_heads, 1.0, 0.0)
        xv = dt_b[...] + dtb_ref[...]
        dt = (jnp.maximum(xv, 0.0) + jnp.log1p(jnp.exp(-jnp.abs(xv)))) * head_lanes
        da = dt * a_row
        d1 = da.astype(BF16)
        r1 = da - d1.astype(F32)
        d2 = r1.astype(BF16)
        d3 = (r1 - d2.astype(F32)).astype(BF16)
        chunk_shift = CHUNK.bit_length() - 1
        br = lax.broadcasted_iota(jnp.int32, (t, t), 0)
        bc = lax.broadcasted_iota(jnp.int32, (t, t), 1)
        same_chunk = jnp.right_shift(br, chunk_shift) == jnp.right_shift(bc, chunk_shift)
        tril = jnp.where((br >= bc) & same_chunk, 1.0, 0.0).astype(BF16)
        acum = _dot(tril, d1) + _dot(tril, d2) + _dot(tril, d3)

        def pack_hi_lo(v):
            hi = v.astype(BF16).astype(F32)
            return (hi + pltpu.roll(v - hi, n_ssm_heads, 1)).astype(BF16)

        sv["pdt"], sv["pac"] = pack_hi_lo(dt), pack_hi_lo(acum)
        lowf_g = jnp.concatenate([lowf] * npg, axis=1)
        sv["lowf_g"], sv["highf_g"] = lowf_g, 1.0 - lowf_g
    b_front.append((300, b_prep))

    def b_expand(g):
        gs = slice(g * gw, (g + 1) * gw)
        e_g = e_ref[:, gs]
        xdt = xs_b[:, gs] * _dot(sv["pdt"], e_g)
        ae = _dot(sv["pac"], e_g)
        xlo_s[:, gs] = (xdt * sv["lowf_g"]).astype(BF16)
        xhi_s[:, gs] = (xdt * sv["highf_g"]).astype(BF16)
        eae_s[:, gs] = jnp.exp(ae)
        ae_s[:, gs] = ae
        for c in range(nc):
            rows = slice(c * CHUNK, (c + 1) * CHUNK)
            ae_c = ae[rows]
            ae_l = ae_c[CHUNK - 1:CHUNK, :]
            xt_s[rows, gs] = (xdt[rows] * jnp.exp(ae_l - ae_c)).astype(BF16)
            edl_s[c * SUBLANES:(c + 1) * SUBLANES, gs] = jnp.broadcast_to(jnp.exp(ae_l), (SUBLANES, gw))

    def b_decay(g, c):
        rows = slice(c * CHUNK, (c + 1) * CHUNK)
        bb = bc_b[rows, g * D_STATE:(g + 1) * D_STATE].astype(BF16)
        cc = bc_b[rows, (SSM_GROUPS + g) * D_STATE:(SSM_GROUPS + g + 1) * D_STATE].astype(BF16)
        cb2 = _dot_nt(cc, jnp.concatenate([bb, bb], axis=0))
        for i in range(npg):
            acol = ae_s[rows, g * gw + i * LANES:g * gw + (i + 1) * LANES]
            arow = jnp.sum(jnp.where(diag, acol, 0.0), axis=0, keepdims=True)
            dec = jnp.exp(jnp.where(causal, acol - arow, NEG))
            mp_s[(c * SSM_GROUPS + g) * npg + i] = (cb2 * dec).astype(BF16)

    def b_yin(g, c):
        rows = slice(c * CHUNK, (c + 1) * CHUNK)
        zblk = jnp.zeros((2 * CHUNK, LANES), BF16)
        for i in range(0, npg, 2):
            base = (c * SSM_GROUPS + g) * npg + i
            l0 = slice(g * gw + i * LANES, g * gw + (i + 1) * LANES)
            l1 = slice(g * gw + (i + 1) * LANES, g * gw + (i + 2) * LANES)
            x0 = jnp.concatenate([xlo_s[rows, l0], xhi_s[rows, l0]], axis=0)
            x1 = jnp.concatenate([xlo_s[rows, l1], xhi_s[rows, l1]], axis=0)
            rhs = jnp.concatenate([jnp.concatenate([x0, zblk], axis=1), jnp.concatenate([zblk, x1], axis=1)], axis=0)
            lhs = jnp.concatenate([mp_s[base], mp_s[base + 1]], axis=1)
            yin_s[rows, g * gw + i * LANES:g * gw + (i + 2) * LANES] = _dot(lhs, rhs)

    for g in range(SSM_GROUPS):
        b_front.append((400, functools.partial(b_expand, g)))
        for c in range(nc):
            b_front.append((150, functools.partial(b_decay, g, c)))
    for g in range(SSM_GROUPS):
        for c in range(nc):
            b_front.append((60, functools.partial(b_yin, g, c)))

    ppk = n_pairs // N_KV_HEADS

    def b_scores(c, kv):
        zpad = jnp.zeros((KEYS_PAD - KEYS, LANES), BF16)
        rows = slice(c * CHUNK, (c + 1) * CHUNK)
        krows = slice(c * CHUNK, c * CHUNK + KEYS)
        c0 = kv * LANES
        kbd = jnp.concatenate([klo_b[krows, c0:c0 + LANES], zpad, khi_b[krows, c0:c0 + LANES], zpad], axis=0)
        for pp in range(ppk):
            p = kv * ppk + pp
            sc = _dot_nt(q_b[rows, p * LANES:(p + 1) * LANES], kbd) + bias_ref[p]
            if not has_past:
                lane_s = lax.broadcasted_iota(jnp.int32, (CHUNK, 2 * KEYS_PAD), 1) & (KEYS_PAD - 1)
                sc = jnp.where(lane_s >= WINDOW - CHUNK * (s_idx * nc + c), sc, NEG)
            sc_s[c * n_pairs + p] = sc

    def b_softmax(c, p):
        es, rs = [], []
        for hh in range(2):
            sh = sc_s[c * n_pairs + p, :, hh * KEYS_PAD:(hh + 1) * KEYS_PAD]
            sink = sinks_ref[2 * p + hh]
            m = jnp.maximum(jnp.max(sh, axis=-1, keepdims=True), sink)
            e = jnp.exp(sh - m)
            den = jnp.sum(e, axis=-1, keepdims=True) + jnp.exp(sink - m)
            es.append(e.astype(BF16))
            rs.append(1.0 / den)
        pe_s[c * n_pairs + p] = jnp.concatenate(es, axis=1)
        rd_s[c * n_pairs + p] = rs[0] * lowf + rs[1] * highf

    def b_pv(c, kv):
        zpad = jnp.zeros((KEYS_PAD - KEYS, LANES), BF16)
        rows = slice(c * CHUNK, (c + 1) * CHUNK)
        krows = slice(c * CHUNK, c * CHUNK + KEYS)
        c0 = kv * LANES
        vbd = jnp.concatenate([vlo_b[krows, c0:c0 + LANES], zpad, vhi_b[krows, c0:c0 + LANES], zpad], axis=0)
        for pp in range(ppk):
            p = kv * ppk + pp
            o = _dot(pe_s[c * n_pairs + p], vbd) * rd_s[c * n_pairs + p]
            o_s[rows, p * LANES:(p + 1) * LANES] = o.astype(BF16)

    for c in range(nc):
        for kv in range(N_KV_HEADS):
            b_first.append((70, functools.partial(b_scores, c, kv)))
    for c in range(nc):
        for p in range(n_pairs):
            b_front.append((110, functools.partial(b_softmax, c, p)))
    for c in range(nc):
        for kv in range(N_KV_HEADS):
            b_front.append((70, functools.partial(b_pv, c, kv)))

    def b_scan(c, g):
        rows = slice(c * CHUNK, (c + 1) * CHUNK)
        gs = slice(g * gw, (g + 1) * gw)
        bb = bc_b[rows, g * D_STATE:(g + 1) * D_STATE].astype(BF16)
        cc = bc_b[rows, (SSM_GROUPS + g) * D_STATE:(SSM_GROUPS + g + 1) * D_STATE].astype(BF16)
        st = st_s[:, gs]
        y = yin_s[rows, gs] + _dot(cc, st.astype(BF16)) * eae_s[rows, gs] + xs_b[rows, gs] * dskip_ref[:, gs]
        y = y * z_b[rows, gs]
        y = y * _rms_scale(y) * ssmg_ref[:, gs]
        y_s[rows, gs] = y.astype(BF16)
        st_s[:, gs] = st * edl_s[c * SUBLANES:c * SUBLANES + 1, gs] + _dot_tn(bb, xt_s[rows, gs])
    for c in range(nc):
        for g in range(SSM_GROUPS):
            b_scans.append((160, functools.partial(b_scan, c, g)))

    def b_gate(i, name):
        gate_s[i] = _sigmoid(_dot(h_b[...], wall_ref[:, cols[name]:cols[name] + d_model]))
    b_gates = [(512, functools.partial(b_gate, 0, "gs")), (512, functools.partial(b_gate, 1, "ga"))]

    def wcol(name):
        return wall_ref[:, cols[name]:cols[name] + d_model]

    def b_branch(i, src_s, names):
        acc = _dot(src_s[:, 0:d_model], wcol(names[0]))
        for k in range(1, len(names)):
            acc = acc + _dot(src_s[:, k * d_model:(k + 1) * d_model], wcol(names[k]))
        part = gate_s[i] * acc
        mix_s[...] = part if i == 0 else mix_s[...] + part
    b_merge.append((1000, functools.partial(b_branch, 0, y_s, ["wbs%d" % k for k in range(d_inner // d_model)])))
    b_merge.append((500, functools.partial(b_branch, 1, o_s, ["wba"])))

    def b_out():
        xo_ref[0] = x_ref[0] + mod_ref[0, 2:3, :] * _dot(mix_s[...].astype(BF16), wcol("wo"))
    b_merge.append((500, b_out))

    _interleave(a_mxu)
    _interleave(a_vpu, a_late + b_gates)
    _interleave(b_first + b_front + b_scans)
    _interleave(b_merge)

    if not has_past:
        for buf in (klo_s, khi_s, vlo_s, vhi_s):
            buf[0:WINDOW, :] = buf[t:t + WINDOW, :]

    @pl.when(s_idx == n_steps - 1)
    def _():
        convn_ref[0] = cb_s[t:t + SUBLANES, :]
        ssmn_ref[0] = st_s[...].T
        kr = kn_ref.shape[1]
        kn_ref[0] = knat_s[t - kr:t, :]
        vn_ref[0] = vnat_s[t - kr:t, :]


def _const_spec(shape, single_buffer=False):
    idx = (0,) * len(shape)
    if single_buffer:
        return pl.BlockSpec(shape, lambda *_: idx, pipeline_mode=pl.Buffered(1))
    return pl.BlockSpec(shape, lambda *_: idx)


def _layer_spec(stacked, layer):
    blk = (1,) + stacked.shape[1:]
    idx = (layer,) + (0,) * (stacked.ndim - 1)
    return pl.BlockSpec(blk, lambda *_: idx, pipeline_mode=pl.Buffered(1))


def _mixer_layer(x, mod, past, lw, pair_bias, e_mat, *, nc):
    bsz, seq, d_model = x.shape
    t = nc * CHUNK
    has_past = past is not None
    d_inner, n_heads = lw["d_inner"], lw["n_heads"]
    conv_ch = d_inner + 2 * SSM_GROUPS * D_STATE
    kvw = N_KV_HEADS * LANES
    kv_rows = seq if has_past else WINDOW
    assert seq % t == 0 and (has_past and seq == t or not has_past and t >= WINDOW)

    x_spec = pl.BlockSpec((1, t, d_model), lambda b, s: (b, s, 0))

    def per_seq(*shape):
        return pl.BlockSpec((1,) + shape, lambda b, s: (b,) + (0,) * len(shape))

    in_specs = [x_spec, per_seq(6, d_model)]
    args = [x, mod]
    if has_past:
        in_specs += [per_seq(SUBLANES, conv_ch), per_seq(d_inner, D_STATE),
                     per_seq(WINDOW, N_KV_HEADS * HEAD_DIM), per_seq(WINDOW, N_KV_HEADS * HEAD_DIM)]
        args += list(past)
    consts = [lw["nmg"], lw["wall"], lw["convw"], lw["convb"], lw["dtb"], lw["alog"], lw["dskip"], lw["ssmg"],
              lw["qg"], lw["kg"]]
    in_specs += [_const_spec(a.shape) for a in consts]
    in_specs[-len(consts) + 1] = _layer_spec(lw["wall"], lw["layer"])
    in_specs.append(pl.BlockSpec(memory_space=pltpu.SMEM))
    tail = [pair_bias, e_mat]
    in_specs += [_const_spec(a.shape, single_buffer=a.size > (1 << 18)) for a in tail]
    args += consts + [lw["sinks"]] + tail

    out_shape = (jax.ShapeDtypeStruct(x.shape, F32),
                 jax.ShapeDtypeStruct((bsz, SUBLANES, conv_ch), F32),
                 jax.ShapeDtypeStruct((bsz, d_inner, D_STATE), F32),
                 jax.ShapeDtypeStruct((bsz, kv_rows, N_KV_HEADS * HEAD_DIM), F32),
                 jax.ShapeDtypeStruct((bsz, kv_rows, N_KV_HEADS * HEAD_DIM), F32))
    out_specs = (x_spec, per_seq(SUBLANES, conv_ch), per_seq(d_inner, D_STATE),
                 per_seq(kv_rows, N_KV_HEADS * HEAD_DIM), per_seq(kv_rows, N_KV_HEADS * HEAD_DIM))
    scratch = [
        pltpu.VMEM((t, d_model), BF16),
        pltpu.VMEM((SUBLANES + t, conv_ch), F32),
        pltpu.VMEM((t, d_inner), F32),
        pltpu.VMEM((t, d_inner), F32),
        pltpu.VMEM((t, 2 * SSM_GROUPS * D_STATE), F32),
        pltpu.VMEM((t, LANES), F32),
        pltpu.VMEM((t, d_model), BF16),
        pltpu.VMEM((WINDOW + t, kvw), BF16),
        pltpu.VMEM((WINDOW + t, kvw), BF16),
        pltpu.VMEM((WINDOW + t, kvw), BF16),
        pltpu.VMEM((WINDOW + t, kvw), BF16),
        pltpu.VMEM((t, d_inner), BF16),
        pltpu.VMEM((t, d_model), BF16),
        pltpu.VMEM((D_STATE, d_inner), F32),
        pltpu.VMEM((t, d_inner), BF16),
        pltpu.VMEM((t, d_inner), BF16),
        pltpu.VMEM((t, d_inner), BF16),
        pltpu.VMEM((t, d_inner), F32),
        pltpu.VMEM((nc * SUBLANES, d_inner), F32),
        pltpu.VMEM((nc * d_inner // LANES, CHUNK, LANES), BF16),
        pltpu.VMEM((t, d_inner), F32),
        pltpu.VMEM((nc * n_heads // 2, CHUNK, 2 * KEYS_PAD), F32),
        pltpu.VMEM((nc * n_heads // 2, CHUNK, 2 * KEYS_PAD), BF16),
        pltpu.VMEM((nc * n_heads // 2, CHUNK, LANES), F32),
        pltpu.VMEM((t, N_KV_HEADS * HEAD_DIM), F32),
        pltpu.VMEM((t, N_KV_HEADS * HEAD_DIM), F32),
        pltpu.VMEM((t, d_inner), F32),
        pltpu.VMEM((t, d_model), F32),
        pltpu.VMEM((2, t, d_model), F32),
    ]
    kern = functools.partial(_mixer_kernel, nc=nc, has_past=has_past, d_model=d_model, d_inner=d_inner,
                             n_heads=n_heads, n_ssm_heads=d_inner // SSM_HEAD_DIM, cols=lw["cols"])
    return pl.pallas_call(
        kern, grid=(bsz, seq // t), in_specs=in_specs, out_specs=out_specs, out_shape=out_shape,
        scratch_shapes=scratch,
        compiler_params=pltpu.CompilerParams(dimension_semantics=("arbitrary", "arbitrary"),
                                             vmem_limit_bytes=VMEM_LIMIT),
        name="mixer_past" if has_past else "mixer_prompt",
    )(*args)


def _ffn_kernel(x_ref, mod_ref, g_ref, wgu_ref, wd_ref, o_ref):
    nb, tm, _ = x_ref.shape
    d_ff = wd_ref.shape[1]
    wg_ref, wu_ref, wd_ref = wgu_ref.at[0, :, 0:d_ff], wgu_ref.at[0, :, d_ff:2 * d_ff], wd_ref.at[0]
    hs = []
    for i in range(nb):
        x = x_ref[i]
        shift, scale = mod_ref[i, 3:4, :], mod_ref[i, 4:5, :]
        hs.append((x * _rms_scale(x) * g_ref[...] * (1.0 + scale) + shift).astype(BF16))
    h = hs[0] if nb == 1 else jnp.concatenate(hs, axis=0)
    gate = _dot(h, wg_ref[...])
    up = _dot(h, wu_ref[...])
    act = (gate * _sigmoid(gate) * up).astype(BF16)
    y = _dot(act, wd_ref[...])
    for i in range(nb):
        o_ref[i] = x_ref[i] + mod_ref[i, 5:6, :] * y[i * tm:(i + 1) * tm]


def _ffn_layer(x, mod, lw, *, nb, tm):
    bsz, seq, d_model = x.shape
    assert seq % tm == 0 and bsz % nb == 0
    xspec = pl.BlockSpec((nb, tm, d_model), lambda b, s: (b, s, 0))
    return pl.pallas_call(
        _ffn_kernel, grid=(bsz // nb, seq // tm),
        in_specs=[xspec, pl.BlockSpec((nb, 6, d_model), lambda b, s: (b, 0, 0)),
                  _const_spec(lw["nfg"].shape), _layer_spec(lw["wgu"], lw["layer"]),
                  _layer_spec(lw["wd"], lw["layer"])],
        out_specs=xspec, out_shape=jax.ShapeDtypeStruct(x.shape, F32),
        compiler_params=pltpu.CompilerParams(dimension_semantics=("arbitrary", "arbitrary"),
                                             vmem_limit_bytes=VMEM_LIMIT),
        name="ffn",
    )(x, mod, lw["nfg"], lw["wgu"], lw["wd"])


def _stacked_weights(p):
    d_model = p["w_out"].shape[2]
    d_inner = p["w_br_ssm"].shape[1]
    n_ssm_heads = p["dt_bias"].shape[1]
    nq = p["w_br_attn"].shape[1]
    nkv = N_KV_HEADS * HEAD_DIM
    conv_ch = d_inner + 2 * SSM_GROUPS * D_STATE
    wi = p["w_in"]
    bounds = [0, d_inner, d_inner + conv_ch, d_inner + conv_ch + n_ssm_heads]
    bounds += [bounds[-1] + nq, bounds[-1] + nq + nkv, bounds[-1] + nq + 2 * nkv]
    bounds += [bounds[-1] + d_model, bounds[-1] + 2 * d_model]
    z, xbc, dtc, q, k, v, gs, ga = [wi[:, :, a:b] for a, b in zip(bounds[:-1], bounds[1:])]
    assert nq == d_model and d_inner % d_model == 0
    pieces = [("z", z), ("xbc", xbc), ("dt", jnp.pad(dtc, ((0, 0), (0, 0), (0, LANES - n_ssm_heads)))),
              ("q", q), ("k", k), ("v", v), ("gs", gs), ("ga", ga),
              ("wba", p["w_br_attn"]), ("wo", p["w_out"])]
    pieces += [("wbs%d" % i, p["w_br_ssm"][:, i * d_model:(i + 1) * d_model]) for i in range(d_inner // d_model)]
    cols, off = {}, 0
    for name, w in pieces:
        cols[name] = off
        off += w.shape[2]
    return dict(cols=cols, wall=jnp.concatenate([w for _, w in pieces], axis=2).astype(BF16),
                wgu=p["w_gate_up"].astype(BF16), wd=p["w_down"].astype(BF16),
                d_inner=d_inner, n_heads=nq // HEAD_DIM)


def _layer_weights(l, p, stacked):
    n_ssm_heads = p["dt_bias"].shape[1]
    rep = lambda vec: jnp.repeat(vec, SSM_HEAD_DIM)[None, :]
    padl = lambda vec: jnp.pad(vec, (0, LANES - n_ssm_heads))[None, :]
    return dict(
        stacked, layer=l,
        nmg=p["norm_mix_g"][l][None, :], nfg=p["norm_ffn_g"][l][None, :],
        convw=p["conv_w"][l], convb=p["conv_b"][l][None, :],
        dtb=padl(p["dt_bias"][l]), alog=padl(p["a_log"][l]), dskip=rep(p["d_skip"][l]),
        ssmg=p["ssm_norm_g"][l][None, :],
        qg=jnp.tile(p["q_norm_g"][l] * (HEAD_DIM ** -0.5), 2)[None, :], kg=jnp.tile(p["k_norm_g"][l], 2)[None, :],
        sinks=p["sinks"][l],
    )


def _expand_matrix(n_ssm_heads):
    assert 3 * n_ssm_heads <= LANES
    r = jnp.arange(LANES)[:, None]
    src = (r < 2 * n_ssm_heads) | (r >= LANES - n_ssm_heads)
    c = jnp.arange(n_ssm_heads * SSM_HEAD_DIM)[None, :] // SSM_HEAD_DIM
    return (src & ((r % n_ssm_heads) == c)).astype(BF16)


def kernel(x_prompt, x_sample, cache_k, cache_v, state_conv, state_ssm, c_prompt, c_sample, rel_bias, ada_w, ada_b, norm_mix_g, norm_ffn_g, w_in, conv_w, conv_b, dt_bias, a_log, d_skip, ssm_norm_g, q_norm_g, k_norm_g, sinks, w_br_ssm, w_br_attn, w_out, w_gate_up, w_down):
    p = dict(norm_mix_g=norm_mix_g, norm_ffn_g=norm_ffn_g, w_in=w_in, conv_w=conv_w, conv_b=conv_b,
             dt_bias=dt_bias, a_log=a_log, d_skip=d_skip, ssm_norm_g=ssm_norm_g, q_norm_g=q_norm_g,
             k_norm_g=k_norm_g, sinks=sinks, w_br_ssm=w_br_ssm, w_br_attn=w_br_attn, w_out=w_out,
             w_gate_up=w_gate_up, w_down=w_down)
    depth = w_in.shape[0]
    bp, seq_p, d_model = x_prompt.shape
    bs, seq_s, _ = x_sample.shape
    n_ssm_heads, ssm_p, d_state = state_ssm.shape[2:]
    d_inner = n_ssm_heads * ssm_p
    nkv = N_KV_HEADS * HEAD_DIM

    mod = _ada_mod(jnp.concatenate([c_prompt, c_sample], axis=0), ada_w, ada_b)
    mod = mod.reshape(depth, bp + bs, 6, d_model)
    pair_bias = _pair_bias(rel_bias)
    e_mat = _expand_matrix(n_ssm_heads)

    xp, xs = x_prompt, x_sample
    outs = [[] for _ in range(8)]
    stacked = _stacked_weights(p)
    for l in range(depth):
        lw = _layer_weights(l, p, stacked)
        mod_p, mod_s = mod[l, :bp], mod[l, bp:]
        past = (jnp.pad(state_conv[l], ((0, 0), (SUBLANES - (CONV_K - 1), 0), (0, 0))),
                state_ssm[l].reshape(bs, d_inner, d_state),
                cache_k[l].reshape(bs, WINDOW, nkv), cache_v[l].reshape(bs, WINDOW, nkv))
        xp, cvp, ssp, kp, vp = _mixer_layer(xp, mod_p, None, lw, pair_bias, e_mat, nc=2)
        xs, cvs, sss, ksn, vsn = _mixer_layer(xs, mod_s, past, lw, pair_bias, e_mat, nc=1)
        xp = _ffn_layer(xp, mod_p, lw, nb=1, tm=min(FFN_ROWS, seq_p))
        nb_s = math.gcd(bs, max(1, FFN_ROWS // seq_s))
        xs = _ffn_layer(xs, mod_s, lw, nb=nb_s, tm=seq_s)
        tail = SUBLANES - (CONV_K - 1)
        for lst, val in zip(outs, (
                cvp[:, tail:], cvs[:, tail:],
                ssp.reshape(bp, n_ssm_heads, ssm_p, d_state), sss.reshape(bs, n_ssm_heads, ssm_p, d_state),
                kp.reshape(bp, WINDOW, N_KV_HEADS, HEAD_DIM), ksn.reshape(bs, seq_s, N_KV_HEADS, HEAD_DIM),
                vp.reshape(bp, WINDOW, N_KV_HEADS, HEAD_DIM), vsn.reshape(bs, seq_s, N_KV_HEADS, HEAD_DIM))):
            lst.append(val)
    return (xp, xs) + tuple(jnp.stack(o) for o in outs)
```

```python
import functools
import math

import jax
import jax.numpy as jnp
from jax import lax
from jax.experimental import pallas as pl
from jax.experimental.pallas import tpu as pltpu

F32 = jnp.float32
BF16 = jnp.bfloat16

CHUNK = 64
EPS = 1e-6
SSM_GROUPS = 4
D_STATE = 128
SSM_HEAD_DIM = 64
CONV_K = 4
HEAD_DIM = 64
N_KV_HEADS = 4
WINDOW = 128
N_BUCKETS = 32
MAX_DISTANCE = 128

LANES = 128
SUBLANES = 8
KEYS = WINDOW + CHUNK
KEYS_PAD = 256
NEG = -1e30
LOG2E = math.log2(math.e)
VMEM_LIMIT = 56 * 1024 * 1024
FFN_ROWS = 512
MIXER_CHUNKS = 2


def _dot(a, b):
    return jnp.dot(a, b, preferred_element_type=F32)


def _dot_nt(a, b):
    return lax.dot_general(a, b, (((1,), (1,)), ((), ())), preferred_element_type=F32)


def _dot_tn(a, b):
    return lax.dot_general(a, b, (((0,), (0,)), ((), ())), preferred_element_type=F32)


def _sigmoid(x):
    return 1.0 / (1.0 + jnp.exp(-x))


def _interleave(*streams):
    totals = [float(sum(w for w, _ in st)) or 1.0 for st in streams]
    pos, done = [0] * len(streams), [0.0] * len(streams)
    while True:
        live = [k for k in range(len(streams)) if pos[k] < len(streams[k])]
        if not live:
            return
        k = min(live, key=lambda k: done[k] / totals[k])
        w, fn = streams[k][pos[k]]
        fn()
        pos[k] += 1
        done[k] += w


def _rms_scale(x):
    return lax.rsqrt(jnp.mean(x * x, axis=-1, keepdims=True) + EPS)


def _ada_kernel(c_ref, w_ref, b_ref, o_ref):
    c = c_ref[...]
    s = (c * _sigmoid(c)).astype(BF16)
    o_ref[0] = _dot(s, w_ref[0].astype(BF16)) + b_ref[0]


def _ada_mod(c_all, ada_w, ada_b):
    depth, d, six_d = ada_w.shape
    n = c_all.shape[0]
    nblk = six_d // d
    return pl.pallas_call(
        _ada_kernel,
        grid=(depth, nblk),
        in_specs=[
            pl.BlockSpec((n, d), lambda l, j: (0, 0)),
            pl.BlockSpec((1, d, d), lambda l, j: (l, 0, j)),
            pl.BlockSpec((1, 1, d), lambda l, j: (l, 0, j)),
        ],
        out_specs=pl.BlockSpec((1, n, d), lambda l, j: (l, 0, j)),
        out_shape=jax.ShapeDtypeStruct((depth, n, six_d), F32),
        name="ada_mod",
    )(c_all, ada_w, ada_b.reshape(depth, 1, six_d))


def _bias_kernel(bkt_ref, rb_ref, o_ref, *, n_heads):
    bkt = bkt_ref[...]
    for h in range(n_heads):
        def body(b, acc, h=h):
            return jnp.where(bkt == b, rb_ref[b, h] * LOG2E, acc)
        acc = lax.fori_loop(0, N_BUCKETS, body, jnp.full(bkt.shape, NEG, F32))
        o_ref[h // 2, :, (h % 2) * KEYS_PAD:(h % 2 + 1) * KEYS_PAD] = acc


def _t5_bucket(rel):
    n = -rel
    half = N_BUCKETS // 2
    max_exact = half // 2
    ret = jnp.where(n < 0, half, 0)
    n = jnp.abs(n)
    nf = jnp.maximum(n, 1).astype(F32)
    large = max_exact + (jnp.log(nf / max_exact) / math.log(MAX_DISTANCE / max_exact)
                         * (half - max_exact)).astype(jnp.int32)
    large = jnp.minimum(large, half - 1)
    return ret + jnp.where(n < max_exact, n, large)


def _pair_bias(rel_bias):
    n_heads = rel_bias.shape[1]
    rel = jnp.arange(KEYS)[None, :] - WINDOW - jnp.arange(CHUNK)[:, None]
    bkt = jnp.pad(_t5_bucket(rel).astype(jnp.int32), ((0, 0), (0, KEYS_PAD - KEYS)), constant_values=-1)
    return pl.pallas_call(
        functools.partial(_bias_kernel, n_heads=n_heads),
        in_specs=[pl.BlockSpec((CHUNK, KEYS_PAD), lambda: (0, 0)),
                  pl.BlockSpec(memory_space=pltpu.SMEM)],
        out_specs=pl.BlockSpec((n_heads // 2, CHUNK, 2 * KEYS_PAD), lambda: (0, 0, 0)),
        out_shape=jax.ShapeDtypeStruct((n_heads // 2, CHUNK, 2 * KEYS_PAD), F32),
        name="pair_bias",
    )(bkt, rel_bias)


def _mixer_kernel(*refs, nc, ns, has_past, d_model, d_inner, n_heads, n_ssm_heads, cols):
    t = nc * CHUNK
    tq = t // ns
    ncs = nc // ns
    stride = SUBLANES + tq
    conv_ch = d_inner + 2 * SSM_GROUPS * D_STATE
    gw = d_inner // SSM_GROUPS
    n_pairs = n_heads // 2
    kvw = N_KV_HEADS * LANES
    it = iter(refs)
    x_ref, mod_ref = next(it), next(it)
    if has_past:
        conv0_ref, ssm0_ref, k0_ref, v0_ref = next(it), next(it), next(it), next(it)
    (nmg_ref, wall_ref, convw_ref, convb_ref, dtb_ref, alog_ref, dskip_ref, ssmg_ref, qg_ref, kg_ref,
     sinks_ref, bias_ref, e_ref) = [next(it) for _ in range(13)]
    wall_ref = wall_ref.at[0]
    xo_ref, convn_ref, ssmn_ref, kn_ref, vn_ref = [next(it) for _ in range(5)]
    (h_s, cb_s, z_s, xs_s, bc_s, dt_s, q_s, klo_s, khi_s, vlo_s, vhi_s, y_s, o_s, st_s,
     xlo_s, xhi_s, xt_s, eae_s, edl_s, mp_s, yin_s, sc_s, pe_s, rd_s, knat_s, vnat_s, ae_s, mix_s,
     gate_s) = list(it)

    s_idx = pl.program_id(1)
    n_steps = pl.num_programs(1)

    lane = lax.broadcasted_iota(jnp.int32, (CHUNK, LANES), 1)
    row = lax.broadcasted_iota(jnp.int32, (CHUNK, LANES), 0)
    causal = row >= (lane & (CHUNK - 1))
    diag = row == (lane & (CHUNK - 1))
    lowk = lax.broadcasted_iota(jnp.int32, (WINDOW, LANES), 1) < HEAD_DIM
    lowf = jnp.where(lax.broadcasted_iota(jnp.int32, (1, LANES), 1) < HEAD_DIM, 1.0, 0.0)
    highf = 1.0 - lowf

    @pl.when(s_idx == 0)
    def _():
        for i in range(ns):
            if has_past:
                cb_s[i * stride:i * stride + SUBLANES, :] = conv0_ref[i]
                st_s[i] = ssm0_ref[i].T
                for src_ref, lo_s, hi_s in ((k0_ref, klo_s, khi_s), (v0_ref, vlo_s, vhi_s)):
                    for j in range(N_KV_HEADS // 2):
                        nat = src_ref[i, :, j * LANES:(j + 1) * LANES]
                        rol = pltpu.roll(nat, HEAD_DIM, 1)
                        a, b = 2 * j * LANES, (2 * j + 1) * LANES
                        lo_s[i, 0:WINDOW, a:a + LANES] = jnp.where(lowk, nat, 0.0).astype(BF16)
                        hi_s[i, 0:WINDOW, a:a + LANES] = jnp.where(lowk, 0.0, rol).astype(BF16)
                        lo_s[i, 0:WINDOW, b:b + LANES] = jnp.where(lowk, rol, 0.0).astype(BF16)
                        hi_s[i, 0:WINDOW, b:b + LANES] = jnp.where(lowk, 0.0, nat).astype(BF16)
            else:
                cb_s[i * stride:i * stride + SUBLANES, :] = jnp.zeros((SUBLANES, conv_ch), F32)
                st_s[i] = jnp.zeros(st_s.shape[1:], F32)
                for buf in (klo_s, khi_s, vlo_s, vhi_s):
                    buf[i, 0:WINDOW, :] = jnp.zeros((WINDOW, kvw), BF16)

    blk = 512
    a_mxu, a_late, a_vpu, b_first, b_front, b_scans, b_merge = [], [], [], [], [], [], []
    h_b, z_b, xs_b, bc_b, dt_b, q_b, klo_b, khi_b, vlo_b, vhi_b = (
        h_s, z_s, xs_s, bc_s, dt_s, q_s, klo_s, khi_s, vlo_s, vhi_s)

    def a_norm():
        for i in range(ns):
            x = x_ref[i]
            shift, scale = mod_ref[i, 0:1, :], mod_ref[i, 1:2, :]
            h_s[i * tq:(i + 1) * tq, :] = (x * _rms_scale(x) * nmg_ref[...] * (1.0 + scale) + shift).astype(BF16)
    a_mxu.append((150, a_norm))

    def a_xbc(j):
        res = _dot(h_s[...], wall_ref[:, cols["xbc"] + j:cols["xbc"] + j + blk])
        for i in range(ns):
            cb_s[i * stride + SUBLANES:(i + 1) * stride, j:j + blk] = res[i * tq:(i + 1) * tq]
    for j in range(0, conv_ch, blk):
        a_mxu.append((256, functools.partial(a_xbc, j)))

    def a_dt():
        dt_s[...] = _dot(h_s[...], wall_ref[:, cols["dt"]:cols["dt"] + LANES])
    a_mxu.append((30, a_dt))

    def head_rms(v):
        v2 = v * v
        sa = jnp.sum(v2 * lowf, axis=-1, keepdims=True)
        sb = jnp.sum(v2 * highf, axis=-1, keepdims=True)
        return lax.rsqrt((sa * lowf + sb * highf) * (1.0 / HEAD_DIM) + EPS)

    def a_q(p2):
        qv = _dot(h_s[...], wall_ref[:, cols["q"] + p2 * 2 * LANES:cols["q"] + (p2 + 1) * 2 * LANES])
        for i in range(2):
            v = qv[:, i * LANES:(i + 1) * LANES]
            q_s[:, (2 * p2 + i) * LANES:(2 * p2 + i + 1) * LANES] = (v * head_rms(v) * qg_ref[...]).astype(BF16)
    for p2 in range(n_pairs // 2):
        a_mxu.append((160, functools.partial(a_q, p2)))

    def a_kv():
        nkv = N_KV_HEADS * HEAD_DIM
        kd = _dot(h_s[...], wall_ref[:, cols["k"]:cols["k"] + nkv])
        vd = _dot(h_s[...], wall_ref[:, cols["v"]:cols["v"] + nkv])
        for j in range(N_KV_HEADS // 2):
            ls = slice(j * LANES, (j + 1) * LANES)
            kx = kd[:, ls]
            kn = kx * head_rms(kx) * kg_ref[...]
            vx = vd[:, ls]
            knat_s[:, ls] = kn
            vnat_s[:, ls] = vx
            for val, lo_s, hi_s in ((kn, klo_s, khi_s), (vx, vlo_s, vhi_s)):
                rol = pltpu.roll(val, HEAD_DIM, 1)
                a, b = 2 * j * LANES, (2 * j + 1) * LANES
                parts = ((lo_s, a, val * lowf), (hi_s, a, rol * highf), (lo_s, b, rol * lowf), (hi_s, b, val * highf))
                for dst_s, c0, piece in parts:
                    piece = piece.astype(BF16)
                    for i in range(ns):
                        dst_s[i, WINDOW:WINDOW + tq, c0:c0 + LANES] = piece[i * tq:(i + 1) * tq]
    a_mxu.append((300, a_kv))

    def a_z(j):
        zz = _dot(h_s[...], wall_ref[:, cols["z"] + j:cols["z"] + j + blk])
        z_s[:, j:j + blk] = zz * _sigmoid(zz)
    for j in range(0, d_inner, blk):
        a_late.append((256, functools.partial(a_z, j)))

    def a_conv(j):
        for i in range(ns):
            r0 = i * stride
            acc = cb_s[r0 + SUBLANES:r0 + stride, j:j + blk] * convw_ref[CONV_K - 1:CONV_K, j:j + blk]
            for k in range(CONV_K - 1):
                o = r0 + SUBLANES - (CONV_K - 1) + k
                acc = acc + cb_s[o:o + tq, j:j + blk] * convw_ref[k:k + 1, j:j + blk]
            acc = acc + convb_ref[:, j:j + blk]
            u = acc * _sigmoid(acc)
            if j < d_inner:
                xs_s[i * tq:(i + 1) * tq, j:j + blk] = u
            else:
                bc_s[i * tq:(i + 1) * tq, j - d_inner:j - d_inner + blk] = u
    for j in range(0, conv_ch, blk):
        a_vpu.append((300, functools.partial(a_conv, j)))

    def a_tail():
        for i in range(ns):
            cb_s[i * stride:i * stride + SUBLANES, :] = cb_s[i * stride + tq:(i + 1) * stride, :]
    if not has_past:
        a_vpu.append((10, a_tail))

    sv = {}
    npg = gw // LANES

    def b_prep():
        a_row = -jnp.exp(alog_ref[...]) * LOG2E
        head_lanes = jnp.where(lax.broadcasted_iota(jnp.int32, (1, LANES), 1) < n_ssm_heads, 1.0, 0.0)
        xv = dt_b[...] + dtb_ref[...]
        dt = (jnp.maximum(xv, 0.0) + jnp.log1p(jnp.exp(-jnp.abs(xv)))) * head_lanes
        da = dt * a_row
        d1 = da.astype(BF16)
        r1 = da - d1.astype(F32)
        d2 = r1.astype(BF16)
        d3 = (r1 - d2.astype(F32)).astype(BF16)
        chunk_shift = CHUNK.bit_length() - 1
        br = lax.broadcasted_iota(jnp.int32, (t, t), 0)
        bc = lax.broadcasted_iota(jnp.int32, (t, t), 1)
        same_chunk = jnp.right_shift(br, chunk_shift) == jnp.right_shift(bc, chunk_shift)
        tril = jnp.where((br >= bc) & same_chunk, 1.0, 0.0).astype(BF16)
        acum = _dot(tril, d1) + _dot(tril, d2) + _dot(tril, d3)

        def pack_hi_lo(v):
            hi = v.astype(BF16).astype(F32)
            return (hi + pltpu.roll(v - hi, n_ssm_heads, 1)).astype(BF16)

        sv["pdt"], sv["pac"] = pack_hi_lo(dt), pack_hi_lo(acum)
        lowf_g = jnp.concatenate([lowf] * npg, axis=1)
        sv["lowf_g"], sv["highf_g"] = lowf_g, 1.0 - lowf_g
    b_front.append((300, b_prep))

    def b_expand(g):
        gs = slice(g * gw, (g + 1) * gw)
        e_g = e_ref[:, gs]
        xdt = xs_b[:, gs] * _dot(sv["pdt"], e_g)
        ae = _dot(sv["pac"], e_g)
        xlo_s[:, gs] = (xdt * sv["lowf_g"]).astype(BF16)
        xhi_s[:, gs] = (xdt * sv["highf_g"]).astype(BF16)
        eae_s[:, gs] = jnp.exp2(ae)
        ae_s[:, gs] = ae
        for c in range(nc):
            rows = slice(c * CHUNK, (c + 1) * CHUNK)
            ae_c = ae[rows]
            ae_l = ae_c[CHUNK - 1:CHUNK, :]
            xt_s[rows, gs] = (xdt[rows] * jnp.exp2(ae_l - ae_c)).astype(BF16)
            edl_s[c * SUBLANES:(c + 1) * SUBLANES, gs] = jnp.broadcast_to(jnp.exp2(ae_l), (SUBLANES, gw))

    def b_decay(g, c):
        rows = slice(c * CHUNK, (c + 1) * CHUNK)
        bb = bc_b[rows, g * D_STATE:(g + 1) * D_STATE].astype(BF16)
        cc = bc_b[rows, (SSM_GROUPS + g) * D_STATE:(SSM_GROUPS + g + 1) * D_STATE].astype(BF16)
        cb2 = _dot_nt(cc, jnp.concatenate([bb, bb], axis=0))
        for i in range(npg):
            acol = ae_s[rows, g * gw + i * LANES:g * gw + (i + 1) * LANES]
            arow = jnp.sum(jnp.where(diag, acol, 0.0), axis=0, keepdims=True)
            dec = jnp.exp2(jnp.where(causal, acol - arow, NEG))
            mp_s[(c * SSM_GROUPS + g) * npg + i] = (cb2 * dec).astype(BF16)

    def b_yin(g, c):
        rows = slice(c * CHUNK, (c + 1) * CHUNK)
        zblk = jnp.zeros((2 * CHUNK, LANES), BF16)
        for i in range(0, npg, 2):
            base = (c * SSM_GROUPS + g) * npg + i
            l0 = slice(g * gw + i * LANES, g * gw + (i + 1) * LANES)
            l1 = slice(g * gw + (i + 1) * LANES, g * gw + (i + 2) * LANES)
            x0 = jnp.concatenate([xlo_s[rows, l0], xhi_s[rows, l0]], axis=0)
            x1 = jnp.concatenate([xlo_s[rows, l1], xhi_s[rows, l1]], axis=0)
            rhs = jnp.concatenate([jnp.concatenate([x0, zblk], axis=1), jnp.concatenate([zblk, x1], axis=1)], axis=0)
            lhs = jnp.concatenate([mp_s[base], mp_s[base + 1]], axis=1)
            yin_s[rows, g * gw + i * LANES:g * gw + (i + 2) * LANES] = _dot(lhs, rhs)

    for g in range(SSM_GROUPS):
        b_front.append((400, functools.partial(b_expand, g)))
        for c in range(nc):
            b_front.append((150, functools.partial(b_decay, g, c)))
    for g in range(SSM_GROUPS):
        for c in range(nc):
            b_front.append((60, functools.partial(b_yin, g, c)))

    ppk = n_pairs // N_KV_HEADS

    def b_scores(c, kv):
        zpad = jnp.zeros((KEYS_PAD - KEYS, LANES), BF16)
        rows = slice(c * CHUNK, (c + 1) * CHUNK)
        i, cl = divmod(c, ncs)
        krows = slice(cl * CHUNK, cl * CHUNK + KEYS)
        c0 = kv * LANES
        kbd = jnp.concatenate([klo_b[i, krows, c0:c0 + LANES], zpad, khi_b[i, krows, c0:c0 + LANES], zpad], axis=0)
        for pp in range(ppk):
            p = kv * ppk + pp
            sc = _dot_nt(q_b[rows, p * LANES:(p + 1) * LANES], kbd) + bias_ref[p]
            if not has_past:
                lane_s = lax.broadcasted_iota(jnp.int32, (CHUNK, 2 * KEYS_PAD), 1) & (KEYS_PAD - 1)
                sc = jnp.where(lane_s >= WINDOW - CHUNK * (s_idx * ncs + cl), sc, NEG)
            sc_s[c * n_pairs + p] = sc

    def b_softmax(c, p):
        es, rs = [], []
        for hh in range(2):
            sh = sc_s[c * n_pairs + p, :, hh * KEYS_PAD:(hh + 1) * KEYS_PAD]
            sink = sinks_ref[2 * p + hh] * LOG2E
            m = jnp.maximum(jnp.max(sh, axis=-1, keepdims=True), sink)
            e = jnp.exp2(sh - m)
            den = jnp.sum(e, axis=-1, keepdims=True) + jnp.exp2(sink - m)
            es.append(e.astype(BF16))
            rs.append(1.0 / den)
        pe_s[c * n_pairs + p] = jnp.concatenate(es, axis=1)
        rd_s[c * n_pairs + p] = rs[0] * lowf + rs[1] * highf

    def b_pv(c, kv):
        zpad = jnp.zeros((KEYS_PAD - KEYS, LANES), BF16)
        rows = slice(c * CHUNK, (c + 1) * CHUNK)
        i, cl = divmod(c, ncs)
        krows = slice(cl * CHUNK, cl * CHUNK + KEYS)
        c0 = kv * LANES
        vbd = jnp.concatenate([vlo_b[i, krows, c0:c0 + LANES], zpad, vhi_b[i, krows, c0:c0 + LANES], zpad], axis=0)
        for pp in range(ppk):
            p = kv * ppk + pp
            o = _dot(pe_s[c * n_pairs + p], vbd) * rd_s[c * n_pairs + p]
            o_s[rows, p * LANES:(p + 1) * LANES] = o.astype(BF16)

    for c in range(nc):
        for kv in range(N_KV_HEADS):
            b_first.append((70, functools.partial(b_scores, c, kv)))
    for c in range(nc):
        for p in range(n_pairs):
            b_front.append((110, functools.partial(b_softmax, c, p)))
    for c in range(nc):
        for kv in range(N_KV_HEADS):
            b_front.append((70, functools.partial(b_pv, c, kv)))

    def b_scan(c, g):
        rows = slice(c * CHUNK, (c + 1) * CHUNK)
        gs = slice(g * gw, (g + 1) * gw)
        bb = bc_b[rows, g * D_STATE:(g + 1) * D_STATE].astype(BF16)
        cc = bc_b[rows, (SSM_GROUPS + g) * D_STATE:(SSM_GROUPS + g + 1) * D_STATE].astype(BF16)
        i = c // ncs
        st = st_s[i, :, gs]
        y = yin_s[rows, gs] + _dot(cc, st.astype(BF16)) * eae_s[rows, gs] + xs_b[rows, gs] * dskip_ref[:, gs]
        y = y * z_b[rows, gs]
        y = y * _rms_scale(y) * ssmg_ref[:, gs]
        y_s[rows, gs] = y.astype(BF16)
        st_s[i, :, gs] = st * edl_s[c * SUBLANES:c * SUBLANES + 1, gs] + _dot_tn(bb, xt_s[rows, gs])
    for c in range(nc):
        for g in range(SSM_GROUPS):
            b_scans.append((160, functools.partial(b_scan, c, g)))

    def b_gate(i, name):
        gate_s[i] = _sigmoid(_dot(h_b[...], wall_ref[:, cols[name]:cols[name] + d_model]))
    b_gates = [(512, functools.partial(b_gate, 0, "gs")), (512, functools.partial(b_gate, 1, "ga"))]

    def wcol(name):
        return wall_ref[:, cols[name]:cols[name] + d_model]

    def b_branch(i, src_s, names):
        acc = _dot(src_s[:, 0:d_model], wcol(names[0]))
        for k in range(1, len(names)):
            acc = acc + _dot(src_s[:, k * d_model:(k + 1) * d_model], wcol(names[k]))
        part = gate_s[i] * acc
        mix_s[...] = part if i == 0 else mix_s[...] + part
    b_merge.append((1000, functools.partial(b_branch, 0, y_s, ["wbs%d" % k for k in range(d_inner // d_model)])))
    b_merge.append((500, functools.partial(b_branch, 1, o_s, ["wba"])))

    def b_out():
        out = _dot(mix_s[...].astype(BF16), wcol("wo"))
        for i in range(ns):
            xo_ref[i] = x_ref[i] + mod_ref[i, 2:3, :] * out[i * tq:(i + 1) * tq]
    b_merge.append((500, b_out))

    _interleave(a_mxu)
    _interleave(a_vpu, a_late + b_gates)
    _interleave(b_first + b_front + b_scans)
    _interleave(b_merge)

    if not has_past:
        for buf in (klo_s, khi_s, vlo_s, vhi_s):
            for i in range(ns):
                buf[i, 0:WINDOW, :] = buf[i, tq:tq + WINDOW, :]

    @pl.when(s_idx == n_steps - 1)
    def _():
        kr = kn_ref.shape[1]
        for i in range(ns):
            convn_ref[i] = cb_s[i * stride + tq:(i + 1) * stride, :]
            ssmn_ref[i] = st_s[i].T
            kn_ref[i] = knat_s[(i + 1) * tq - kr:(i + 1) * tq, :]
            vn_ref[i] = vnat_s[(i + 1) * tq - kr:(i + 1) * tq, :]


def _const_spec(shape, single_buffer=False):
    idx = (0,) * len(shape)
    if single_buffer:
        return pl.BlockSpec(shape, lambda *_: idx, pipeline_mode=pl.Buffered(1))
    return pl.BlockSpec(shape, lambda *_: idx)


def _layer_spec(stacked, layer):
    blk = (1,) + stacked.shape[1:]
    idx = (layer,) + (0,) * (stacked.ndim - 1)
    return pl.BlockSpec(blk, lambda *_: idx, pipeline_mode=pl.Buffered(1))


def _mixer_layer(x, mod, past, lw, pair_bias, e_mat, *, nc, ns):
    bsz, seq, d_model = x.shape
    t = nc * CHUNK
    tq = t // ns
    has_past = past is not None
    d_inner, n_heads = lw["d_inner"], lw["n_heads"]
    conv_ch = d_inner + 2 * SSM_GROUPS * D_STATE
    kvw = N_KV_HEADS * LANES
    kv_rows = seq if has_past else WINDOW
    assert nc % ns == 0 and bsz % ns == 0 and seq % tq == 0
    assert has_past and seq == tq or not has_past and tq >= WINDOW

    x_spec = pl.BlockSpec((ns, tq, d_model), lambda b, s: (b, s, 0))

    def per_seq(*shape):
        return pl.BlockSpec((ns,) + shape, lambda b, s: (b,) + (0,) * len(shape))

    in_specs = [x_spec, per_seq(6, d_model)]
    args = [x, mod]
    if has_past:
        in_specs += [per_seq(SUBLANES, conv_ch), per_seq(d_inner, D_STATE),
                     per_seq(WINDOW, N_KV_HEADS * HEAD_DIM), per_seq(WINDOW, N_KV_HEADS * HEAD_DIM)]
        args += list(past)
    consts = [lw["nmg"], lw["wall"], lw["convw"], lw["convb"], lw["dtb"], lw["alog"], lw["dskip"], lw["ssmg"],
              lw["qg"], lw["kg"]]
    in_specs += [_const_spec(a.shape) for a in consts]
    in_specs[-len(consts) + 1] = _layer_spec(lw["wall"], lw["layer"])
    in_specs.append(pl.BlockSpec(memory_space=pltpu.SMEM))
    tail = [pair_bias, e_mat]
    in_specs += [_const_spec(a.shape, single_buffer=a.size > (1 << 18)) for a in tail]
    args += consts + [lw["sinks"]] + tail

    out_shape = (jax.ShapeDtypeStruct(x.shape, F32),
                 jax.ShapeDtypeStruct((bsz, SUBLANES, conv_ch), F32),
                 jax.ShapeDtypeStruct((bsz, d_inner, D_STATE), F32),
                 jax.ShapeDtypeStruct((bsz, kv_rows, N_KV_HEADS * HEAD_DIM), F32),
                 jax.ShapeDtypeStruct((bsz, kv_rows, N_KV_HEADS * HEAD_DIM), F32))
    out_specs = (x_spec, per_seq(SUBLANES, conv_ch), per_seq(d_inner, D_STATE),
                 per_seq(kv_rows, N_KV_HEADS * HEAD_DIM), per_seq(kv_rows, N_KV_HEADS * HEAD_DIM))
    scratch = [
        pltpu.VMEM((t, d_model), BF16),
        pltpu.VMEM((ns * (SUBLANES + tq), conv_ch), F32),
        pltpu.VMEM((t, d_inner), F32),
        pltpu.VMEM((t, d_inner), F32),
        pltpu.VMEM((t, 2 * SSM_GROUPS * D_STATE), F32),
        pltpu.VMEM((t, LANES), F32),
        pltpu.VMEM((t, d_model), BF16),
        pltpu.VMEM((ns, WINDOW + tq, kvw), BF16),
        pltpu.VMEM((ns, WINDOW + tq, kvw), BF16),
        pltpu.VMEM((ns, WINDOW + tq, kvw), BF16),
        pltpu.VMEM((ns, WINDOW + tq, kvw), BF16),
        pltpu.VMEM((t, d_inner), BF16),
        pltpu.VMEM((t, d_model), BF16),
        pltpu.VMEM((ns, D_STATE, d_inner), F32),
        pltpu.VMEM((t, d_inner), BF16),
        pltpu.VMEM((t, d_inner), BF16),
        pltpu.VMEM((t, d_inner), BF16),
        pltpu.VMEM((t, d_inner), F32),
        pltpu.VMEM((nc * SUBLANES, d_inner), F32),
        pltpu.VMEM((nc * d_inner // LANES, CHUNK, LANES), BF16),
        pltpu.VMEM((t, d_inner), F32),
        pltpu.VMEM((nc * n_heads // 2, CHUNK, 2 * KEYS_PAD), F32),
        pltpu.VMEM((nc * n_heads // 2, CHUNK, 2 * KEYS_PAD), BF16),
        pltpu.VMEM((nc * n_heads // 2, CHUNK, LANES), F32),
        pltpu.VMEM((t, N_KV_HEADS * HEAD_DIM), F32),
        pltpu.VMEM((t, N_KV_HEADS * HEAD_DIM), F32),
        pltpu.VMEM((t, d_inner), F32),
        pltpu.VMEM((t, d_model), F32),
        pltpu.VMEM((2, t, d_model), F32),
    ]
    kern = functools.partial(_mixer_kernel, nc=nc, ns=ns, has_past=has_past, d_model=d_model, d_inner=d_inner,
                             n_heads=n_heads, n_ssm_heads=d_inner // SSM_HEAD_DIM, cols=lw["cols"])
    return pl.pallas_call(
        kern, grid=(bsz // ns, seq // tq), in_specs=in_specs, out_specs=out_specs, out_shape=out_shape,
        scratch_shapes=scratch,
        compiler_params=pltpu.CompilerParams(dimension_semantics=("arbitrary", "arbitrary"),
                                             vmem_limit_bytes=VMEM_LIMIT),
        name="mixer_past" if has_past else "mixer_prompt",
    )(*args)


def _ffn_kernel(x_ref, mod_ref, g_ref, wgu_ref, wd_ref, o_ref):
    nb, tm, _ = x_ref.shape
    d_ff = wd_ref.shape[1]
    wg_ref, wu_ref, wd_ref = wgu_ref.at[0, :, 0:d_ff], wgu_ref.at[0, :, d_ff:2 * d_ff], wd_ref.at[0]
    hs = []
    for i in range(nb):
        x = x_ref[i]
        shift, scale = mod_ref[i, 3:4, :], mod_ref[i, 4:5, :]
        hs.append((x * _rms_scale(x) * g_ref[...] * (1.0 + scale) + shift).astype(BF16))
    h = hs[0] if nb == 1 else jnp.concatenate(hs, axis=0)
    gate = _dot(h, wg_ref[...])
    up = _dot(h, wu_ref[...])
    act = (gate * _sigmoid(gate) * up).astype(BF16)
    y = _dot(act, wd_ref[...])
    for i in range(nb):
        o_ref[i] = x_ref[i] + mod_ref[i, 5:6, :] * y[i * tm:(i + 1) * tm]


def _ffn_layer(x, mod, lw, *, nb, tm):
    bsz, seq, d_model = x.shape
    assert seq % tm == 0 and bsz % nb == 0
    xspec = pl.BlockSpec((nb, tm, d_model), lambda b, s: (b, s, 0))
    return pl.pallas_call(
        _ffn_kernel, grid=(bsz // nb, seq // tm),
        in_specs=[xspec, pl.BlockSpec((nb, 6, d_model), lambda b, s: (b, 0, 0)),
                  _const_spec(lw["nfg"].shape), _layer_spec(lw["wgu"], lw["layer"]),
                  _layer_spec(lw["wd"], lw["layer"])],
        out_specs=xspec, out_shape=jax.ShapeDtypeStruct(x.shape, F32),
        compiler_params=pltpu.CompilerParams(dimension_semantics=("arbitrary", "arbitrary"),
                                             vmem_limit_bytes=VMEM_LIMIT),
        name="ffn",
    )(x, mod, lw["nfg"], lw["wgu"], lw["wd"])


def _stacked_weights(p):
    d_model = p["w_out"].shape[2]
    d_inner = p["w_br_ssm"].shape[1]
    n_ssm_heads = p["dt_bias"].shape[1]
    nq = p["w_br_attn"].shape[1]
    nkv = N_KV_HEADS * HEAD_DIM
    conv_ch = d_inner + 2 * SSM_GROUPS * D_STATE
    wi = p["w_in"]
    bounds = [0, d_inner, d_inner + conv_ch, d_inner + conv_ch + n_ssm_heads]
    bounds += [bounds[-1] + nq, bounds[-1] + nq + nkv, bounds[-1] + nq + 2 * nkv]
    bounds += [bounds[-1] + d_model, bounds[-1] + 2 * d_model]
    z, xbc, dtc, q, k, v, gs, ga = [wi[:, :, a:b] for a, b in zip(bounds[:-1], bounds[1:])]
    assert nq == d_model and d_inner % d_model == 0
    pieces = [("z", z), ("xbc", xbc), ("dt", jnp.pad(dtc, ((0, 0), (0, 0), (0, LANES - n_ssm_heads)))),
              ("q", q), ("k", k), ("v", v), ("gs", gs), ("ga", ga),
              ("wba", p["w_br_attn"]), ("wo", p["w_out"])]
    pieces += [("wbs%d" % i, p["w_br_ssm"][:, i * d_model:(i + 1) * d_model]) for i in range(d_inner // d_model)]
    cols, off = {}, 0
    for name, w in pieces:
        cols[name] = off
        off += w.shape[2]
    return dict(cols=cols, wall=jnp.concatenate([w for _, w in pieces], axis=2).astype(BF16),
                wgu=p["w_gate_up"].astype(BF16), wd=p["w_down"].astype(BF16),
                d_inner=d_inner, n_heads=nq // HEAD_DIM)


def _layer_weights(l, p, stacked):
    n_ssm_heads = p["dt_bias"].shape[1]
    rep = lambda vec: jnp.repeat(vec, SSM_HEAD_DIM)[None, :]
    padl = lambda vec: jnp.pad(vec, (0, LANES - n_ssm_heads))[None, :]
    return dict(
        stacked, layer=l,
        nmg=p["norm_mix_g"][l][None, :], nfg=p["norm_ffn_g"][l][None, :],
        convw=p["conv_w"][l], convb=p["conv_b"][l][None, :],
        dtb=padl(p["dt_bias"][l]), alog=padl(p["a_log"][l]), dskip=rep(p["d_skip"][l]),
        ssmg=p["ssm_norm_g"][l][None, :],
        qg=jnp.tile(p["q_norm_g"][l] * (HEAD_DIM ** -0.5 * LOG2E), 2)[None, :], kg=jnp.tile(p["k_norm_g"][l], 2)[None, :],
        sinks=p["sinks"][l],
    )


def _expand_matrix(n_ssm_heads):
    assert 3 * n_ssm_heads <= LANES
    r = jnp.arange(LANES)[:, None]
    src = (r < 2 * n_ssm_heads) | (r >= LANES - n_ssm_heads)
    c = jnp.arange(n_ssm_heads * SSM_HEAD_DIM)[None, :] // SSM_HEAD_DIM
    return (src & ((r % n_ssm_heads) == c)).astype(BF16)


def kernel(x_prompt, x_sample, cache_k, cache_v, state_conv, state_ssm, c_prompt, c_sample, rel_bias, ada_w, ada_b, norm_mix_g, norm_ffn_g, w_in, conv_w, conv_b, dt_bias, a_log, d_skip, ssm_norm_g, q_norm_g, k_norm_g, sinks, w_br_ssm, w_br_attn, w_out, w_gate_up, w_down):
    p = dict(norm_mix_g=norm_mix_g, norm_ffn_g=norm_ffn_g, w_in=w_in, conv_w=conv_w, conv_b=conv_b,
             dt_bias=dt_bias, a_log=a_log, d_skip=d_skip, ssm_norm_g=ssm_norm_g, q_norm_g=q_norm_g,
             k_norm_g=k_norm_g, sinks=sinks, w_br_ssm=w_br_ssm, w_br_attn=w_br_attn, w_out=w_out,
             w_gate_up=w_gate_up, w_down=w_down)
    depth = w_in.shape[0]
    bp, seq_p, d_model = x_prompt.shape
    bs, seq_s, _ = x_sample.shape
    n_ssm_heads, ssm_p, d_state = state_ssm.shape[2:]
    d_inner = n_ssm_heads * ssm_p
    nkv = N_KV_HEADS * HEAD_DIM

    mod = _ada_mod(jnp.concatenate([c_prompt, c_sample], axis=0), ada_w, ada_b)
    mod = mod.reshape(depth, bp + bs, 6, d_model)
    pair_bias = _pair_bias(rel_bias)
    e_mat = _expand_matrix(n_ssm_heads)

    xp, xs = x_prompt, x_sample
    outs = [[] for _ in range(8)]
    stacked = _stacked_weights(p)
    for l in range(depth):
        lw = _layer_weights(l, p, stacked)
        mod_p, mod_s = mod[l, :bp], mod[l, bp:]
        past = (jnp.pad(state_conv[l], ((0, 0), (SUBLANES - (CONV_K - 1), 0), (0, 0))),
                state_ssm[l].reshape(bs, d_inner, d_state),
                cache_k[l].reshape(bs, WINDOW, nkv), cache_v[l].reshape(bs, WINDOW, nkv))
        xp, cvp, ssp, kp, vp = _mixer_layer(xp, mod_p, None, lw, pair_bias, e_mat, nc=MIXER_CHUNKS, ns=1)
        ns_s = math.gcd(bs, MIXER_CHUNKS)
        xs, cvs, sss, ksn, vsn = _mixer_layer(xs, mod_s, past, lw, pair_bias, e_mat, nc=ns_s, ns=ns_s)
        xp = _ffn_layer(xp, mod_p, lw, nb=1, tm=min(FFN_ROWS, seq_p))
        nb_s = math.gcd(bs, max(1, FFN_ROWS // seq_s))
        xs = _ffn_layer(xs, mod_s, lw, nb=nb_s, tm=seq_s)
        tail = SUBLANES - (CONV_K - 1)
        for lst, val in zip(outs, (
                cvp[:, tail:], cvs[:, tail:],
                ssp.reshape(bp, n_ssm_heads, ssm_p, d_state), sss.reshape(bs, n_ssm_heads, ssm_p, d_state),
                kp.reshape(bp, WINDOW, N_KV_HEADS, HEAD_DIM), ksn.reshape(bs, seq_s, N_KV_HEADS, HEAD_DIM),
                vp.reshape(bp, WINDOW, N_KV_HEADS, HEAD_DIM), vsn.reshape(bs, seq_s, N_KV_HEADS, HEAD_DIM))):
            lst.append(val)
    return (xp, xs) + tuple(jnp.stack(o) for o in outs)
```

```python
import functools
import math

import jax
import jax.numpy as jnp
from jax import lax
from jax.experimental import pallas as pl
from jax.experimental.pallas import tpu as pltpu

F32 = jnp.float32
BF16 = jnp.bfloat16

CHUNK = 64
EPS = 1e-6
SSM_GROUPS = 4
D_STATE = 128
SSM_HEAD_DIM = 64
CONV_K = 4
HEAD_DIM = 64
N_KV_HEADS = 4
WINDOW = 128
N_BUCKETS = 32
MAX_DISTANCE = 128

LANES = 128
SUBLANES = 8
KEYS = WINDOW + CHUNK
KEYS_PAD = 256
NEG = -1e30
LOG2E = math.log2(math.e)
VMEM_LIMIT = 56 * 1024 * 1024
FFN_ROWS = 1024
MIXER_CHUNKS = 2


def _dot(a, b):
    return jnp.dot(a, b, preferred_element_type=F32)


def _dot_nt(a, b):
    return lax.dot_general(a, b, (((1,), (1,)), ((), ())), preferred_element_type=F32)


def _dot_tn(a, b):
    return lax.dot_general(a, b, (((0,), (0,)), ((), ())), preferred_element_type=F32)


def _sigmoid(x):
    return 1.0 / (1.0 + jnp.exp(-x))


def _interleave(*streams):
    totals = [float(sum(w for w, _ in st)) or 1.0 for st in streams]
    pos, done = [0] * len(streams), [0.0] * len(streams)
    while True:
        live = [k for k in range(len(streams)) if pos[k] < len(streams[k])]
        if not live:
            return
        k = min(live, key=lambda k: done[k] / totals[k])
        w, fn = streams[k][pos[k]]
        fn()
        pos[k] += 1
        done[k] += w


def _rms_scale(x):
    return lax.rsqrt(jnp.mean(x * x, axis=-1, keepdims=True) + EPS)


def _ada_kernel(c_ref, w_ref, b_ref, o_ref):
    c = c_ref[...]
    s = (c * _sigmoid(c)).astype(BF16)
    o_ref[0] = _dot(s, w_ref[0].astype(BF16)) + b_ref[0]


def _ada_mod(c_all, ada_w, ada_b):
    depth, d, six_d = ada_w.shape
    n = c_all.shape[0]
    wblk = six_d // 2
    return pl.pallas_call(
        _ada_kernel,
        grid=(depth, six_d // wblk),
        in_specs=[
            pl.BlockSpec((n, d), lambda l, j: (0, 0)),
            pl.BlockSpec((1, d, wblk), lambda l, j: (l, 0, j)),
            pl.BlockSpec((1, 1, wblk), lambda l, j: (l, 0, j)),
        ],
        out_specs=pl.BlockSpec((1, n, wblk), lambda l, j: (l, 0, j)),
        out_shape=jax.ShapeDtypeStruct((depth, n, six_d), F32),
        name="ada_mod",
    )(c_all, ada_w, ada_b.reshape(depth, 1, six_d))


def _bias_kernel(bkt_ref, rb_ref, o_ref, *, n_heads):
    bkt = bkt_ref[...]
    for h in range(n_heads):
        def body(b, acc, h=h):
            return jnp.where(bkt == b, rb_ref[b, h] * LOG2E, acc)
        acc = lax.fori_loop(0, N_BUCKETS, body, jnp.full(bkt.shape, NEG, F32))
        o_ref[h // 2, :, (h % 2) * KEYS_PAD:(h % 2 + 1) * KEYS_PAD] = acc


def _t5_bucket(rel):
    n = -rel
    half = N_BUCKETS // 2
    max_exact = half // 2
    ret = jnp.where(n < 0, half, 0)
    n = jnp.abs(n)
    nf = jnp.maximum(n, 1).astype(F32)
    large = max_exact + (jnp.log(nf / max_exact) / math.log(MAX_DISTANCE / max_exact)
                         * (half - max_exact)).astype(jnp.int32)
    large = jnp.minimum(large, half - 1)
    return ret + jnp.where(n < max_exact, n, large)


def _pair_bias(rel_bias):
    n_heads = rel_bias.shape[1]
    rel = jnp.arange(KEYS)[None, :] - WINDOW - jnp.arange(CHUNK)[:, None]
    bkt = jnp.pad(_t5_bucket(rel).astype(jnp.int32), ((0, 0), (0, KEYS_PAD - KEYS)), constant_values=-1)
    return pl.pallas_call(
        functools.partial(_bias_kernel, n_heads=n_heads),
        in_specs=[pl.BlockSpec((CHUNK, KEYS_PAD), lambda: (0, 0)),
                  pl.BlockSpec(memory_space=pltpu.SMEM)],
        out_specs=pl.BlockSpec((n_heads // 2, CHUNK, 2 * KEYS_PAD), lambda: (0, 0, 0)),
        out_shape=jax.ShapeDtypeStruct((n_heads // 2, CHUNK, 2 * KEYS_PAD), F32),
        name="pair_bias",
    )(bkt, rel_bias)


def _mixer_kernel(*refs, nc, ns, has_past, d_model, d_inner, n_heads, n_ssm_heads, cols, layer):
    t = nc * CHUNK
    tq = t // ns
    ncs = nc // ns
    stride = SUBLANES + tq
    conv_ch = d_inner + 2 * SSM_GROUPS * D_STATE
    gw = d_inner // SSM_GROUPS
    n_pairs = n_heads // 2
    kvw = N_KV_HEADS * LANES
    it = iter(refs)
    x_ref, mod_ref = next(it), next(it)
    if has_past:
        conv0_ref, ssm0_ref, k0_ref, v0_ref = next(it), next(it), next(it), next(it)
    (nmg_ref, wall_ref, convw_ref, convb_ref, dtb_ref, alog_ref, dskip_ref, ssmg_ref, qg_ref, kg_ref,
     sinks_ref, bias_ref, e_ref) = [next(it) for _ in range(13)]
    (nmg_ref, wall_ref, convw_ref, convb_ref, dtb_ref, alog_ref, dskip_ref, ssmg_ref, qg_ref, kg_ref) = [
        r.at[0] for r in (nmg_ref, wall_ref, convw_ref, convb_ref, dtb_ref, alog_ref, dskip_ref, ssmg_ref,
                          qg_ref, kg_ref)]
    xo_ref, convn_ref, ssmn_ref, kn_ref, vn_ref = [next(it) for _ in range(5)]
    (h_s, cb_s, z_s, xs_s, bc_s, dt_s, q_s, klo_s, khi_s, vlo_s, vhi_s, y_s, o_s, st_s,
     xlo_s, xhi_s, xt_s, eae_s, edl_s, mp_s, yin_s, sc_s, pe_s, rd_s, knat_s, vnat_s, ae_s, mix_s,
     gate_s) = list(it)

    s_idx = pl.program_id(1)
    n_steps = pl.num_programs(1)

    lane = lax.broadcasted_iota(jnp.int32, (CHUNK, LANES), 1)
    row = lax.broadcasted_iota(jnp.int32, (CHUNK, LANES), 0)
    causal = row >= (lane & (CHUNK - 1))
    diag = row == (lane & (CHUNK - 1))
    lowk = lax.broadcasted_iota(jnp.int32, (WINDOW, LANES), 1) < HEAD_DIM
    lowf = jnp.where(lax.broadcasted_iota(jnp.int32, (1, LANES), 1) < HEAD_DIM, 1.0, 0.0)
    highf = 1.0 - lowf

    @pl.when(s_idx == 0)
    def _():
        for i in range(ns):
            if has_past:
                cb_s[i * stride:i * stride + SUBLANES, :] = conv0_ref[i]
                st_s[i] = ssm0_ref[i].T
                for src_ref, lo_s, hi_s in ((k0_ref, klo_s, khi_s), (v0_ref, vlo_s, vhi_s)):
                    for j in range(N_KV_HEADS // 2):
                        nat = src_ref[i, :, j * LANES:(j + 1) * LANES]
                        rol = pltpu.roll(nat, HEAD_DIM, 1)
                        a, b = 2 * j * LANES, (2 * j + 1) * LANES
                        lo_s[i, 0:WINDOW, a:a + LANES] = jnp.where(lowk, nat, 0.0).astype(BF16)
                        hi_s[i, 0:WINDOW, a:a + LANES] = jnp.where(lowk, 0.0, rol).astype(BF16)
                        lo_s[i, 0:WINDOW, b:b + LANES] = jnp.where(lowk, rol, 0.0).astype(BF16)
                        hi_s[i, 0:WINDOW, b:b + LANES] = jnp.where(lowk, 0.0, nat).astype(BF16)
            else:
                cb_s[i * stride:i * stride + SUBLANES, :] = jnp.zeros((SUBLANES, conv_ch), F32)
                st_s[i] = jnp.zeros(st_s.shape[1:], F32)
                for buf in (klo_s, khi_s, vlo_s, vhi_s):
                    buf[i, 0:WINDOW, :] = jnp.zeros((WINDOW, kvw), BF16)

    blk = 512
    a_mxu, a_late, a_vpu, b_first, b_front, b_scans, b_merge = [], [], [], [], [], [], []
    h_b, z_b, xs_b, bc_b, dt_b, q_b, klo_b, khi_b, vlo_b, vhi_b = (
        h_s, z_s, xs_s, bc_s, dt_s, q_s, klo_s, khi_s, vlo_s, vhi_s)

    def a_norm():
        for i in range(ns):
            x = x_ref[i]
            shift, scale = mod_ref[i, 0:1, :], mod_ref[i, 1:2, :]
            h_s[i * tq:(i + 1) * tq, :] = (x * _rms_scale(x) * nmg_ref[...] * (1.0 + scale) + shift).astype(BF16)
    a_mxu.append((150, a_norm))

    def a_xbc(j):
        res = _dot(h_s[...], wall_ref[:, cols["xbc"] + j:cols["xbc"] + j + blk])
        for i in range(ns):
            cb_s[i * stride + SUBLANES:(i + 1) * stride, j:j + blk] = res[i * tq:(i + 1) * tq]
    for j in range(0, conv_ch, blk):
        a_mxu.append((256, functools.partial(a_xbc, j)))

    def a_dt():
        dt_s[...] = _dot(h_s[...], wall_ref[:, cols["dt"]:cols["dt"] + LANES])
    a_mxu.append((30, a_dt))

    def head_rms(v):
        v2 = v * v
        sa = jnp.sum(v2 * lowf, axis=-1, keepdims=True)
        sb = jnp.sum(v2 * highf, axis=-1, keepdims=True)
        return lax.rsqrt((sa * lowf + sb * highf) * (1.0 / HEAD_DIM) + EPS)

    def a_q(p2):
        qv = _dot(h_s[...], wall_ref[:, cols["q"] + p2 * 2 * LANES:cols["q"] + (p2 + 1) * 2 * LANES])
        for i in range(2):
            v = qv[:, i * LANES:(i + 1) * LANES]
            q_s[:, (2 * p2 + i) * LANES:(2 * p2 + i + 1) * LANES] = (v * head_rms(v) * qg_ref[...]).astype(BF16)
    for p2 in range(n_pairs // 2):
        a_mxu.append((160, functools.partial(a_q, p2)))

    def a_kv():
        nkv = N_KV_HEADS * HEAD_DIM
        kd = _dot(h_s[...], wall_ref[:, cols["k"]:cols["k"] + nkv])
        vd = _dot(h_s[...], wall_ref[:, cols["v"]:cols["v"] + nkv])
        for j in range(N_KV_HEADS // 2):
            ls = slice(j * LANES, (j + 1) * LANES)
            kx = kd[:, ls]
            kn = kx * head_rms(kx) * kg_ref[...]
            vx = vd[:, ls]
            knat_s[:, ls] = kn
            vnat_s[:, ls] = vx
            for val, lo_s, hi_s in ((kn, klo_s, khi_s), (vx, vlo_s, vhi_s)):
                rol = pltpu.roll(val, HEAD_DIM, 1)
                a, b = 2 * j * LANES, (2 * j + 1) * LANES
                parts = ((lo_s, a, val * lowf), (hi_s, a, rol * highf), (lo_s, b, rol * lowf), (hi_s, b, val * highf))
                for dst_s, c0, piece in parts:
                    piece = piece.astype(BF16)
                    for i in range(ns):
                        dst_s[i, WINDOW:WINDOW + tq, c0:c0 + LANES] = piece[i * tq:(i + 1) * tq]
    a_mxu.append((300, a_kv))

    def a_z(j):
        zz = _dot(h_s[...], wall_ref[:, cols["z"] + j:cols["z"] + j + blk])
        z_s[:, j:j + blk] = zz * _sigmoid(zz)
    for j in range(0, d_inner, blk):
        a_late.append((256, functools.partial(a_z, j)))

    def a_conv(j):
        for i in range(ns):
            r0 = i * stride
            acc = cb_s[r0 + SUBLANES:r0 + stride, j:j + blk] * convw_ref[CONV_K - 1:CONV_K, j:j + blk]
            for k in range(CONV_K - 1):
                o = r0 + SUBLANES - (CONV_K - 1) + k
                acc = acc + cb_s[o:o + tq, j:j + blk] * convw_ref[k:k + 1, j:j + blk]
            acc = acc + convb_ref[:, j:j + blk]
            u = acc * _sigmoid(acc)
            if j < d_inner:
                xs_s[i * tq:(i + 1) * tq, j:j + blk] = u
            else:
                bc_s[i * tq:(i + 1) * tq, j - d_inner:j - d_inner + blk] = u
    for j in range(0, conv_ch, blk):
        a_vpu.append((300, functools.partial(a_conv, j)))

    def a_tail():
        for i in range(ns):
            cb_s[i * stride:i * stride + SUBLANES, :] = cb_s[i * stride + tq:(i + 1) * stride, :]
    if not has_past:
        a_vpu.append((10, a_tail))

    sv = {}
    npg = gw // LANES

    def b_prep():
        a_row = -jnp.exp(alog_ref[...]) * LOG2E
        head_lanes = jnp.where(lax.broadcasted_iota(jnp.int32, (1, LANES), 1) < n_ssm_heads, 1.0, 0.0)
        xv = dt_b[...] + dtb_ref[...]
        dt = (jnp.maximum(xv, 0.0) + jnp.log1p(jnp.exp(-jnp.abs(xv)))) * head_lanes
        da = dt * a_row
        d1 = da.astype(BF16)
        r1 = da - d1.astype(F32)
        d2 = r1.astype(BF16)
        d3 = (r1 - d2.astype(F32)).astype(BF16)
        chunk_shift = CHUNK.bit_length() - 1
        br = lax.broadcasted_iota(jnp.int32, (t, t), 0)
        bc = lax.broadcasted_iota(jnp.int32, (t, t), 1)
        same_chunk = jnp.right_shift(br, chunk_shift) == jnp.right_shift(bc, chunk_shift)
        tril = jnp.where((br >= bc) & same_chunk, 1.0, 0.0).astype(BF16)
        acum = _dot(tril, d1) + _dot(tril, d2) + _dot(tril, d3)

        def pack_hi_lo(v):
            hi = v.astype(BF16).astype(F32)
            return (hi + pltpu.roll(v - hi, n_ssm_heads, 1)).astype(BF16)

        sv["pdt"], sv["pac"] = pack_hi_lo(dt), pack_hi_lo(acum)
        lowf_g = jnp.concatenate([lowf] * npg, axis=1)
        sv["lowf_g"], sv["highf_g"] = lowf_g, 1.0 - lowf_g
    b_front.append((300, b_prep))

    def b_expand(g):
        gs = slice(g * gw, (g + 1) * gw)
        e_g = e_ref[:, gs]
        xdt = xs_b[:, gs] * _dot(sv["pdt"], e_g)
        ae = _dot(sv["pac"], e_g)
        xlo_s[:, gs] = (xdt * sv["lowf_g"]).astype(BF16)
        xhi_s[:, gs] = (xdt * sv["highf_g"]).astype(BF16)
        eae_s[:, gs] = jnp.exp2(ae)
        ae_s[:, gs] = ae
        for c in range(nc):
            rows = slice(c * CHUNK, (c + 1) * CHUNK)
            ae_c = ae[rows]
            ae_l = ae_c[CHUNK - 1:CHUNK, :]
            xt_s[rows, gs] = (xdt[rows] * jnp.exp2(ae_l - ae_c)).astype(BF16)
            edl_s[c * SUBLANES:(c + 1) * SUBLANES, gs] = jnp.broadcast_to(jnp.exp2(ae_l), (SUBLANES, gw))

    def b_decay(g, c):
        rows = slice(c * CHUNK, (c + 1) * CHUNK)
        bb = bc_b[rows, g * D_STATE:(g + 1) * D_STATE].astype(BF16)
        cc = bc_b[rows, (SSM_GROUPS + g) * D_STATE:(SSM_GROUPS + g + 1) * D_STATE].astype(BF16)
        cb2 = _dot_nt(cc, jnp.concatenate([bb, bb], axis=0))
        for i in range(npg):
            acol = ae_s[rows, g * gw + i * LANES:g * gw + (i + 1) * LANES]
            arow = jnp.sum(jnp.where(diag, acol, 0.0), axis=0, keepdims=True)
            dec = jnp.exp2(jnp.where(causal, acol - arow, NEG))
            mp_s[(c * SSM_GROUPS + g) * npg + i] = (cb2 * dec).astype(BF16)

    def b_yin(g, c):
        rows = slice(c * CHUNK, (c + 1) * CHUNK)
        zblk = jnp.zeros((2 * CHUNK, LANES), BF16)
        for i in range(0, npg, 2):
            base = (c * SSM_GROUPS + g) * npg + i
            l0 = slice(g * gw + i * LANES, g * gw + (i + 1) * LANES)
            l1 = slice(g * gw + (i + 1) * LANES, g * gw + (i + 2) * LANES)
            x0 = jnp.concatenate([xlo_s[rows, l0], xhi_s[rows, l0]], axis=0)
            x1 = jnp.concatenate([xlo_s[rows, l1], xhi_s[rows, l1]], axis=0)
            rhs = jnp.concatenate([jnp.concatenate([x0, zblk], axis=1), jnp.concatenate([zblk, x1], axis=1)], axis=0)
            lhs = jnp.concatenate([mp_s[base], mp_s[base + 1]], axis=1)
            yin_s[rows, g * gw + i * LANES:g * gw + (i + 2) * LANES] = _dot(lhs, rhs)

    for g in range(SSM_GROUPS):
        b_front.append((400, functools.partial(b_expand, g)))
        for c in range(nc):
            b_front.append((150, functools.partial(b_decay, g, c)))
    for g in range(SSM_GROUPS):
        for c in range(nc):
            b_front.append((60, functools.partial(b_yin, g, c)))

    ppk = n_pairs // N_KV_HEADS

    def b_scores(c, kv):
        zpad = jnp.zeros((KEYS_PAD - KEYS, LANES), BF16)
        rows = slice(c * CHUNK, (c + 1) * CHUNK)
        i, cl = divmod(c, ncs)
        krows = slice(cl * CHUNK, cl * CHUNK + KEYS)
        c0 = kv * LANES
        kbd = jnp.concatenate([klo_b[i, krows, c0:c0 + LANES], zpad, khi_b[i, krows, c0:c0 + LANES], zpad], axis=0)
        for pp in range(ppk):
            p = kv * ppk + pp
            sc = _dot_nt(q_b[rows, p * LANES:(p + 1) * LANES], kbd) + bias_ref[p]
            if not has_past:
                lane_s = lax.broadcasted_iota(jnp.int32, (CHUNK, 2 * KEYS_PAD), 1) & (KEYS_PAD - 1)
                sc = jnp.where(lane_s >= WINDOW - CHUNK * (s_idx * ncs + cl), sc, NEG)
            sc_s[c * n_pairs + p] = sc

    def b_softmax(c, p):
        es, rs = [], []
        for hh in range(2):
            sh = sc_s[c * n_pairs + p, :, hh * KEYS_PAD:(hh + 1) * KEYS_PAD]
            sink = sinks_ref[layer, 2 * p + hh] * LOG2E
            m = jnp.maximum(jnp.max(sh, axis=-1, keepdims=True), sink)
            e = jnp.exp2(sh - m)
            den = jnp.sum(e, axis=-1, keepdims=True) + jnp.exp2(sink - m)
            es.append(e.astype(BF16))
            rs.append(1.0 / den)
        pe_s[c * n_pairs + p] = jnp.concatenate(es, axis=1)
        rd_s[c * n_pairs + p] = rs[0] * lowf + rs[1] * highf

    def b_pv(c, kv):
        zpad = jnp.zeros((KEYS_PAD - KEYS, LANES), BF16)
        rows = slice(c * CHUNK, (c + 1) * CHUNK)
        i, cl = divmod(c, ncs)
        krows = slice(cl * CHUNK, cl * CHUNK + KEYS)
        c0 = kv * LANES
        vbd = jnp.concatenate([vlo_b[i, krows, c0:c0 + LANES], zpad, vhi_b[i, krows, c0:c0 + LANES], zpad], axis=0)
        for pp in range(ppk):
            p = kv * ppk + pp
            o = _dot(pe_s[c * n_pairs + p], vbd) * rd_s[c * n_pairs + p]
            o_s[rows, p * LANES:(p + 1) * LANES] = o.astype(BF16)

    for c in range(nc):
        for kv in range(N_KV_HEADS):
            b_first.append((70, functools.partial(b_scores, c, kv)))
    for c in range(nc):
        for p in range(n_pairs):
            b_front.append((110, functools.partial(b_softmax, c, p)))
    for c in range(nc):
        for kv in range(N_KV_HEADS):
            b_front.append((70, functools.partial(b_pv, c, kv)))

    def b_scan(c, g):
        rows = slice(c * CHUNK, (c + 1) * CHUNK)
        gs = slice(g * gw, (g + 1) * gw)
        bb = bc_b[rows, g * D_STATE:(g + 1) * D_STATE].astype(BF16)
        cc = bc_b[rows, (SSM_GROUPS + g) * D_STATE:(SSM_GROUPS + g + 1) * D_STATE].astype(BF16)
        i = c // ncs
        st = st_s[i, :, gs]
        y = yin_s[rows, gs] + _dot(cc, st.astype(BF16)) * eae_s[rows, gs] + xs_b[rows, gs] * dskip_ref[:, gs]
        y = y * z_b[rows, gs]
        y = y * _rms_scale(y) * ssmg_ref[:, gs]
        y_s[rows, gs] = y.astype(BF16)
        st_s[i, :, gs] = st * edl_s[c * SUBLANES:c * SUBLANES + 1, gs] + _dot_tn(bb, xt_s[rows, gs])
    for c in range(nc):
        for g in range(SSM_GROUPS):
            b_scans.append((160, functools.partial(b_scan, c, g)))

    def b_gate(i, name):
        gate_s[i] = _sigmoid(_dot(h_b[...], wall_ref[:, cols[name]:cols[name] + d_model]))
    b_gates = [(512, functools.partial(b_gate, 0, "gs")), (512, functools.partial(b_gate, 1, "ga"))]

    def wcol(name):
        return wall_ref[:, cols[name]:cols[name] + d_model]

    def b_branch(i, src_s, names):
        acc = _dot(src_s[:, 0:d_model], wcol(names[0]))
        for k in range(1, len(names)):
            acc = acc + _dot(src_s[:, k * d_model:(k + 1) * d_model], wcol(names[k]))
        part = gate_s[i] * acc
        mix_s[...] = part if i == 0 else mix_s[...] + part
    b_merge.append((1000, functools.partial(b_branch, 0, y_s, ["wbs%d" % k for k in range(d_inner // d_model)])))
    b_merge.append((500, functools.partial(b_branch, 1, o_s, ["wba"])))

    def b_out():
        out = _dot(mix_s[...].astype(BF16), wcol("wo"))
        for i in range(ns):
            xo_ref[i] = x_ref[i] + mod_ref[i, 2:3, :] * out[i * tq:(i + 1) * tq]
    b_merge.append((500, b_out))

    _interleave(a_mxu)
    _interleave(a_vpu, a_late + b_gates)
    _interleave(b_first + b_front + b_scans)
    _interleave(b_merge)

    if not has_past:
        for buf in (klo_s, khi_s, vlo_s, vhi_s):
            for i in range(ns):
                buf[i, 0:WINDOW, :] = buf[i, tq:tq + WINDOW, :]

    @pl.when(s_idx == n_steps - 1)
    def _():
        kr = kn_ref.shape[1]
        for i in range(ns):
            convn_ref[i] = cb_s[i * stride + tq:(i + 1) * stride, :]
            ssmn_ref[i] = st_s[i].T
            kn_ref[i] = knat_s[(i + 1) * tq - kr:(i + 1) * tq, :]
            vn_ref[i] = vnat_s[(i + 1) * tq - kr:(i + 1) * tq, :]


def _const_spec(shape, single_buffer=False):
    idx = (0,) * len(shape)
    if single_buffer:
        return pl.BlockSpec(shape, lambda *_: idx, pipeline_mode=pl.Buffered(1))
    return pl.BlockSpec(shape, lambda *_: idx)


def _layer_spec(stacked, layer):
    blk = (1,) + stacked.shape[1:]
    idx = (layer,) + (0,) * (stacked.ndim - 1)
    if math.prod(blk) > (1 << 20):
        return pl.BlockSpec(blk, lambda *_: idx, pipeline_mode=pl.Buffered(1))
    return pl.BlockSpec(blk, lambda *_: idx)


def _mixer_layer(x, mod, past, lw, pair_bias, e_mat, *, nc, ns):
    bsz, seq, d_model = x.shape
    t = nc * CHUNK
    tq = t // ns
    has_past = past is not None
    d_inner, n_heads = lw["d_inner"], lw["n_heads"]
    conv_ch = d_inner + 2 * SSM_GROUPS * D_STATE
    kvw = N_KV_HEADS * LANES
    kv_rows = seq if has_past else WINDOW
    assert nc % ns == 0 and bsz % ns == 0 and seq % tq == 0
    assert has_past and seq == tq or not has_past and tq >= WINDOW

    x_spec = pl.BlockSpec((ns, tq, d_model), lambda b, s: (b, s, 0))

    def per_seq(*shape):
        return pl.BlockSpec((ns,) + shape, lambda b, s: (b,) + (0,) * len(shape))

    in_specs = [x_spec, per_seq(6, d_model)]
    args = [x, mod]
    if has_past:
        in_specs += [per_seq(SUBLANES, conv_ch), per_seq(d_inner, D_STATE),
                     per_seq(WINDOW, N_KV_HEADS * HEAD_DIM), per_seq(WINDOW, N_KV_HEADS * HEAD_DIM)]
        args += list(past)
    consts = [lw["nmg"], lw["wall"], lw["convw"], lw["convb"], lw["dtb"], lw["alog"], lw["dskip"], lw["ssmg"],
              lw["qg"], lw["kg"]]
    in_specs += [_layer_spec(a, lw["layer"]) for a in consts]
    in_specs.append(pl.BlockSpec(memory_space=pltpu.SMEM))
    tail = [pair_bias, e_mat]
    in_specs += [_const_spec(a.shape, single_buffer=a.size > (1 << 18)) for a in tail]
    args += consts + [lw["sinks"]] + tail

    out_shape = (jax.ShapeDtypeStruct(x.shape, F32),
                 jax.ShapeDtypeStruct((bsz, SUBLANES, conv_ch), F32),
                 jax.ShapeDtypeStruct((bsz, d_inner, D_STATE), F32),
                 jax.ShapeDtypeStruct((bsz, kv_rows, N_KV_HEADS * HEAD_DIM), F32),
                 jax.ShapeDtypeStruct((bsz, kv_rows, N_KV_HEADS * HEAD_DIM), F32))
    out_specs = (x_spec, per_seq(SUBLANES, conv_ch), per_seq(d_inner, D_STATE),
                 per_seq(kv_rows, N_KV_HEADS * HEAD_DIM), per_seq(kv_rows, N_KV_HEADS * HEAD_DIM))
    scratch = [
        pltpu.VMEM((t, d_model), BF16),
        pltpu.VMEM((ns * (SUBLANES + tq), conv_ch), F32),
        pltpu.VMEM((t, d_inner), F32),
        pltpu.VMEM((t, d_inner), F32),
        pltpu.VMEM((t, 2 * SSM_GROUPS * D_STATE), F32),
        pltpu.VMEM((t, LANES), F32),
        pltpu.VMEM((t, d_model), BF16),
        pltpu.VMEM((ns, WINDOW + tq, kvw), BF16),
        pltpu.VMEM((ns, WINDOW + tq, kvw), BF16),
        pltpu.VMEM((ns, WINDOW + tq, kvw), BF16),
        pltpu.VMEM((ns, WINDOW + tq, kvw), BF16),
        pltpu.VMEM((t, d_inner), BF16),
        pltpu.VMEM((t, d_model), BF16),
        pltpu.VMEM((ns, D_STATE, d_inner), F32),
        pltpu.VMEM((t, d_inner), BF16),
        pltpu.VMEM((t, d_inner), BF16),
        pltpu.VMEM((t, d_inner), BF16),
        pltpu.VMEM((t, d_inner), F32),
        pltpu.VMEM((nc * SUBLANES, d_inner), F32),
        pltpu.VMEM((nc * d_inner // LANES, CHUNK, LANES), BF16),
        pltpu.VMEM((t, d_inner), F32),
        pltpu.VMEM((nc * n_heads // 2, CHUNK, 2 * KEYS_PAD), F32),
        pltpu.VMEM((nc * n_heads // 2, CHUNK, 2 * KEYS_PAD), BF16),
        pltpu.VMEM((nc * n_heads // 2, CHUNK, LANES), F32),
        pltpu.VMEM((t, N_KV_HEADS * HEAD_DIM), F32),
        pltpu.VMEM((t, N_KV_HEADS * HEAD_DIM), F32),
        pltpu.VMEM((t, d_inner), F32),
        pltpu.VMEM((t, d_model), F32),
        pltpu.VMEM((2, t, d_model), F32),
    ]
    kern = functools.partial(_mixer_kernel, nc=nc, ns=ns, has_past=has_past, d_model=d_model, d_inner=d_inner,
                             n_heads=n_heads, n_ssm_heads=d_inner // SSM_HEAD_DIM, cols=lw["cols"],
                             layer=lw["layer"])
    return pl.pallas_call(
        kern, grid=(bsz // ns, seq // tq), in_specs=in_specs, out_specs=out_specs, out_shape=out_shape,
        scratch_shapes=scratch,
        compiler_params=pltpu.CompilerParams(dimension_semantics=("arbitrary", "arbitrary"),
                                             vmem_limit_bytes=VMEM_LIMIT),
        name="mixer_past" if has_past else "mixer_prompt",
    )(*args)


def _ffn_kernel(x_ref, mod_ref, g_ref, wgu_ref, wd_ref, o_ref):
    nb, tm, _ = x_ref.shape
    d_ff = wd_ref.shape[1]
    wg_ref, wu_ref, wd_ref = wgu_ref.at[0, :, 0:d_ff], wgu_ref.at[0, :, d_ff:2 * d_ff], wd_ref.at[0]
    hs = []
    for i in range(nb):
        x = x_ref[i]
        shift, scale = mod_ref[i, 3:4, :], mod_ref[i, 4:5, :]
        hs.append((x * _rms_scale(x) * g_ref[0] * (1.0 + scale) + shift).astype(BF16))
    h = hs[0] if nb == 1 else jnp.concatenate(hs, axis=0)
    gate = _dot(h, wg_ref[...])
    up = _dot(h, wu_ref[...])
    act = (gate * _sigmoid(gate) * up).astype(BF16)
    y = _dot(act, wd_ref[...])
    for i in range(nb):
        o_ref[i] = x_ref[i] + mod_ref[i, 5:6, :] * y[i * tm:(i + 1) * tm]


def _ffn_layer(x, mod, lw, *, nb, tm):
    bsz, seq, d_model = x.shape
    assert seq % tm == 0 and bsz % nb == 0
    xspec = pl.BlockSpec((nb, tm, d_model), lambda b, s: (b, s, 0))
    return pl.pallas_call(
        _ffn_kernel, grid=(bsz // nb, seq // tm),
        in_specs=[xspec, pl.BlockSpec((nb, 6, d_model), lambda b, s: (b, 0, 0)),
                  _layer_spec(lw["nfg"], lw["layer"]), _layer_spec(lw["wgu"], lw["layer"]),
                  _layer_spec(lw["wd"], lw["layer"])],
        out_specs=xspec, out_shape=jax.ShapeDtypeStruct(x.shape, F32),
        compiler_params=pltpu.CompilerParams(dimension_semantics=("arbitrary", "arbitrary"),
                                             vmem_limit_bytes=VMEM_LIMIT),
        name="ffn",
    )(x, mod, lw["nfg"], lw["wgu"], lw["wd"])


def _stacked_weights(p):
    d_model = p["w_out"].shape[2]
    d_inner = p["w_br_ssm"].shape[1]
    n_ssm_heads = p["dt_bias"].shape[1]
    nq = p["w_br_attn"].shape[1]
    nkv = N_KV_HEADS * HEAD_DIM
    conv_ch = d_inner + 2 * SSM_GROUPS * D_STATE
    wi = p["w_in"]
    bounds = [0, d_inner, d_inner + conv_ch, d_inner + conv_ch + n_ssm_heads]
    bounds += [bounds[-1] + nq, bounds[-1] + nq + nkv, bounds[-1] + nq + 2 * nkv]
    bounds += [bounds[-1] + d_model, bounds[-1] + 2 * d_model]
    z, xbc, dtc, q, k, v, gs, ga = [wi[:, :, a:b] for a, b in zip(bounds[:-1], bounds[1:])]
    assert nq == d_model and d_inner % d_model == 0
    pieces = [("z", z), ("xbc", xbc), ("dt", jnp.pad(dtc, ((0, 0), (0, 0), (0, LANES - n_ssm_heads)))),
              ("q", q), ("k", k), ("v", v), ("gs", gs), ("ga", ga),
              ("wba", p["w_br_attn"]), ("wo", p["w_out"])]
    pieces += [("wbs%d" % i, p["w_br_ssm"][:, i * d_model:(i + 1) * d_model]) for i in range(d_inner // d_model)]
    cols, off = {}, 0
    for name, w in pieces:
        cols[name] = off
        off += w.shape[2]
    return dict(cols=cols, wall=jnp.concatenate([w for _, w in pieces], axis=2).astype(BF16),
                wgu=p["w_gate_up"].astype(BF16), wd=p["w_down"].astype(BF16),
                d_inner=d_inner, n_heads=nq // HEAD_DIM)


def _stacked_vectors(p):
    n_ssm_heads = p["dt_bias"].shape[1]
    row = lambda a: a[:, None, :]
    rep = lambda a: row(jnp.repeat(a, SSM_HEAD_DIM, axis=1))
    padl = lambda a: row(jnp.pad(a, ((0, 0), (0, LANES - n_ssm_heads))))
    return dict(
        nmg=row(p["norm_mix_g"]), nfg=row(p["norm_ffn_g"]), convw=p["conv_w"], convb=row(p["conv_b"]),
        dtb=padl(p["dt_bias"]), alog=padl(p["a_log"]), dskip=rep(p["d_skip"]), ssmg=row(p["ssm_norm_g"]),
        qg=row(jnp.tile(p["q_norm_g"] * (HEAD_DIM ** -0.5 * LOG2E), (1, 2))), kg=row(jnp.tile(p["k_norm_g"], (1, 2))),
        sinks=p["sinks"],
    )


def _expand_matrix(n_ssm_heads):
    assert 3 * n_ssm_heads <= LANES
    r = jnp.arange(LANES)[:, None]
    src = (r < 2 * n_ssm_heads) | (r >= LANES - n_ssm_heads)
    c = jnp.arange(n_ssm_heads * SSM_HEAD_DIM)[None, :] // SSM_HEAD_DIM
    return (src & ((r % n_ssm_heads) == c)).astype(BF16)


def kernel(x_prompt, x_sample, cache_k, cache_v, state_conv, state_ssm, c_prompt, c_sample, rel_bias, ada_w, ada_b, norm_mix_g, norm_ffn_g, w_in, conv_w, conv_b, dt_bias, a_log, d_skip, ssm_norm_g, q_norm_g, k_norm_g, sinks, w_br_ssm, w_br_attn, w_out, w_gate_up, w_down):
    p = dict(norm_mix_g=norm_mix_g, norm_ffn_g=norm_ffn_g, w_in=w_in, conv_w=conv_w, conv_b=conv_b,
             dt_bias=dt_bias, a_log=a_log, d_skip=d_skip, ssm_norm_g=ssm_norm_g, q_norm_g=q_norm_g,
             k_norm_g=k_norm_g, sinks=sinks, w_br_ssm=w_br_ssm, w_br_attn=w_br_attn, w_out=w_out,
             w_gate_up=w_gate_up, w_down=w_down)
    depth = w_in.shape[0]
    bp, seq_p, d_model = x_prompt.shape
    bs, seq_s, _ = x_sample.shape
    n_ssm_heads, ssm_p, d_state = state_ssm.shape[2:]
    d_inner = n_ssm_heads * ssm_p
    nkv = N_KV_HEADS * HEAD_DIM

    mod = _ada_mod(jnp.concatenate([c_prompt, c_sample], axis=0), ada_w, ada_b)
    mod = mod.reshape(depth, bp + bs, 6, d_model)
    pair_bias = _pair_bias(rel_bias)
    e_mat = _expand_matrix(n_ssm_heads)

    xp, xs = x_prompt, x_sample
    outs = [[] for _ in range(8)]
    stacked = dict(_stacked_weights(p), **_stacked_vectors(p))
    for l in range(depth):
        lw = dict(stacked, layer=l)
        mod_p, mod_s = mod[l, :bp], mod[l, bp:]
        past = (jnp.pad(state_conv[l], ((0, 0), (SUBLANES - (CONV_K - 1), 0), (0, 0))),
                state_ssm[l].reshape(bs, d_inner, d_state),
                cache_k[l].reshape(bs, WINDOW, nkv), cache_v[l].reshape(bs, WINDOW, nkv))
        xp, cvp, ssp, kp, vp = _mixer_layer(xp, mod_p, None, lw, pair_bias, e_mat, nc=MIXER_CHUNKS, ns=1)
        ns_s = math.gcd(bs, MIXER_CHUNKS)
        xs, cvs, sss, ksn, vsn = _mixer_layer(xs, mod_s, past, lw, pair_bias, e_mat, nc=ns_s, ns=ns_s)
        xp = _ffn_layer(xp, mod_p, lw, nb=1, tm=min(FFN_ROWS, seq_p))
        nb_s = math.gcd(bs, max(1, FFN_ROWS // seq_s))
        xs = _ffn_layer(xs, mod_s, lw, nb=nb_s, tm=seq_s)
        tail = SUBLANES - (CONV_K - 1)
        for lst, val in zip(outs, (
                cvp[:, tail:], cvs[:, tail:],
                ssp.reshape(bp, n_ssm_heads, ssm_p, d_state), sss.reshape(bs, n_ssm_heads, ssm_p, d_state),
                kp.reshape(bp, WINDOW, N_KV_HEADS, HEAD_DIM), ksn.reshape(bs, seq_s, N_KV_HEADS, HEAD_DIM),
                vp.reshape(bp, WINDOW, N_KV_HEADS, HEAD_DIM), vsn.reshape(bs, seq_s, N_KV_HEADS, HEAD_DIM))):
            lst.append(val)
    return (xp, xs) + tuple(jnp.stack(o) for o in outs)
```

```python
import functools
import math

import jax
import jax.numpy as jnp
from jax import lax
from jax.experimental import pallas as pl
from jax.experimental.pallas import tpu as pltpu

F32 = jnp.float32
BF16 = jnp.bfloat16

CHUNK = 64
EPS = 1e-6
SSM_GROUPS = 4
D_STATE = 128
SSM_HEAD_DIM = 64
CONV_K = 4
HEAD_DIM = 64
N_KV_HEADS = 4
WINDOW = 128
N_BUCKETS = 32
MAX_DISTANCE = 128

LANES = 128
SUBLANES = 8
KEYS = WINDOW + CHUNK
NEG = -1e30
LOG2E = math.log2(math.e)
VMEM_LIMIT = 56 * 1024 * 1024
FFN_ROWS = 1024
MIXER_CHUNKS = 2
COL_BLK = 512


def _dot(a, b):
    return jnp.dot(a, b, preferred_element_type=F32)


def _dot_nt(a, b):
    return lax.dot_general(a, b, (((1,), (1,)), ((), ())), preferred_element_type=F32)


def _dot_tn(a, b):
    return lax.dot_general(a, b, (((0,), (0,)), ((), ())), preferred_element_type=F32)


def _sigmoid(x):
    return 1.0 / (1.0 + jnp.exp(-x))


def _interleave(*streams):
    totals = [float(sum(w for w, _ in st)) or 1.0 for st in streams]
    pos, done = [0] * len(streams), [0.0] * len(streams)
    while True:
        live = [k for k in range(len(streams)) if pos[k] < len(streams[k])]
        if not live:
            return
        k = min(live, key=lambda k: done[k] / totals[k])
        w, fn = streams[k][pos[k]]
        fn()
        pos[k] += 1
        done[k] += w


def _rms_scale(x):
    return lax.rsqrt(jnp.mean(x * x, axis=-1, keepdims=True) + EPS)


def _ada_kernel(c_ref, w_ref, b_ref, o_ref):
    c = c_ref[...]
    s = (c * _sigmoid(c)).astype(BF16)
    o_ref[0] = _dot(s, w_ref[0].astype(BF16)) + b_ref[0]


def _ada_mod(c_all, ada_w, ada_b):
    depth, d, six_d = ada_w.shape
    n = c_all.shape[0]
    wblk = six_d // 2
    return pl.pallas_call(
        _ada_kernel,
        grid=(depth, six_d // wblk),
        in_specs=[
            pl.BlockSpec((n, d), lambda l, j: (0, 0)),
            pl.BlockSpec((1, d, wblk), lambda l, j: (l, 0, j)),
            pl.BlockSpec((1, 1, wblk), lambda l, j: (l, 0, j)),
        ],
        out_specs=pl.BlockSpec((1, n, wblk), lambda l, j: (l, 0, j)),
        out_shape=jax.ShapeDtypeStruct((depth, n, six_d), F32),
        name="ada_mod",
    )(c_all, ada_w, ada_b.reshape(depth, 1, six_d))


def _bias_kernel(bkt_ref, rb_ref, o_ref, *, n_heads):
    bkt = bkt_ref[...]
    khw = bkt.shape[1]
    for h in range(n_heads):
        def body(b, acc, h=h):
            return jnp.where(bkt == b, rb_ref[b, h] * LOG2E, acc)
        acc = lax.fori_loop(0, N_BUCKETS, body, jnp.full(bkt.shape, NEG, F32))
        o_ref[h // 2, :, (h % 2) * khw:(h % 2 + 1) * khw] = acc


def _t5_bucket(rel):
    n = -rel
    half = N_BUCKETS // 2
    max_exact = half // 2
    ret = jnp.where(n < 0, half, 0)
    n = jnp.abs(n)
    nf = jnp.maximum(n, 1).astype(F32)
    large = max_exact + (jnp.log(nf / max_exact) / math.log(MAX_DISTANCE / max_exact)
                         * (half - max_exact)).astype(jnp.int32)
    large = jnp.minimum(large, half - 1)
    return ret + jnp.where(n < max_exact, n, large)


def _key_width(tq):
    return pl.cdiv(WINDOW + tq, LANES) * LANES


def _pair_bias(rel_bias, ncs, window_valid):
    n_heads = rel_bias.shape[1]
    tq = ncs * CHUNK
    khw = _key_width(tq)
    q = jnp.arange(tq)[:, None]
    key = jnp.arange(khw)[None, :]
    band = key - (q // CHUNK) * CHUNK
    rel = band - WINDOW - q % CHUNK
    seen = (band >= 0) & (band < KEYS) & ((key >= WINDOW) | window_valid)
    bkt = jnp.where(seen, _t5_bucket(rel), -1).astype(jnp.int32)
    return pl.pallas_call(
        functools.partial(_bias_kernel, n_heads=n_heads),
        in_specs=[pl.BlockSpec((tq, khw), lambda: (0, 0)),
                  pl.BlockSpec(memory_space=pltpu.SMEM)],
        out_specs=pl.BlockSpec((n_heads // 2, tq, 2 * khw), lambda: (0, 0, 0)),
        out_shape=jax.ShapeDtypeStruct((n_heads // 2, tq, 2 * khw), F32),
        name="pair_bias",
    )(bkt, rel_bias)


def _mixer_kernel(*refs, nc, ns, has_past, d_model, d_inner, n_heads, n_ssm_heads, cols, layer):
    t = nc * CHUNK
    tq = t // ns
    ncs = nc // ns
    stride = SUBLANES + tq
    blk = COL_BLK
    conv_ch = d_inner + 2 * SSM_GROUPS * D_STATE
    gw = d_inner // SSM_GROUPS
    n_pairs = n_heads // 2
    kvw = N_KV_HEADS * LANES
    it = iter(refs)
    x_ref, mod_ref = next(it), next(it)
    if has_past:
        conv0_ref, ssm0_ref, k0_ref, v0_ref = next(it), next(it), next(it), next(it)
    (nmg_ref, wall_ref, convw_ref, convb_ref, dtb_ref, alog_ref, dskip_ref, ssmg_ref, qg_ref, kg_ref,
     sinks_ref, bias_ref, e_ref) = [next(it) for _ in range(13)]
    (nmg_ref, wall_ref, convw_ref, convb_ref, dtb_ref, alog_ref, dskip_ref, ssmg_ref, qg_ref, kg_ref) = [
        r.at[0] for r in (nmg_ref, wall_ref, convw_ref, convb_ref, dtb_ref, alog_ref, dskip_ref, ssmg_ref,
                          qg_ref, kg_ref)]
    xo_ref, convn_ref, ssmn_ref, kn_ref, vn_ref = [next(it) for _ in range(5)]
    (h_s, cb_s, z_s, xs_s, bc_s, dt_s, q_s, klo_s, khi_s, vlo_s, vhi_s, y_s, o_s, st_s,
     xlo_s, xhi_s, xt_s, eae_s, edl_s, mp_s, yin_s, sc_s, pe_s, rd_s, knat_s, vnat_s, ae_s, mix_s,
     gate_s) = list(it)

    s_idx = pl.program_id(1)
    n_steps = pl.num_programs(1)

    lane = lax.broadcasted_iota(jnp.int32, (CHUNK, LANES), 1)
    row = lax.broadcasted_iota(jnp.int32, (CHUNK, LANES), 0)
    causal = row >= (lane & (CHUNK - 1))
    diag = row == (lane & (CHUNK - 1))
    lowk = lax.broadcasted_iota(jnp.int32, (WINDOW, LANES), 1) < HEAD_DIM
    lowf = jnp.where(lax.broadcasted_iota(jnp.int32, (1, LANES), 1) < HEAD_DIM, 1.0, 0.0)
    highf = 1.0 - lowf

    @pl.when(s_idx == 0)
    def _():
        for i in range(ns):
            if has_past:
                cb_s[i * stride:i * stride + SUBLANES, :] = conv0_ref[i]
                st_s[i] = ssm0_ref[i].T
                for src_ref, lo_s, hi_s in ((k0_ref, klo_s, khi_s), (v0_ref, vlo_s, vhi_s)):
                    for j in range(N_KV_HEADS // 2):
                        nat = src_ref[i, :, j * LANES:(j + 1) * LANES]
                        rol = pltpu.roll(nat, HEAD_DIM, 1)
                        a, b = 2 * j * LANES, (2 * j + 1) * LANES
                        lo_s[i, 0:WINDOW, a:a + LANES] = jnp.where(lowk, nat, 0.0).astype(BF16)
                        hi_s[i, 0:WINDOW, a:a + LANES] = jnp.where(lowk, 0.0, rol).astype(BF16)
                        lo_s[i, 0:WINDOW, b:b + LANES] = jnp.where(lowk, rol, 0.0).astype(BF16)
                        hi_s[i, 0:WINDOW, b:b + LANES] = jnp.where(lowk, 0.0, nat).astype(BF16)
            else:
                cb_s[i * stride:i * stride + SUBLANES, :] = jnp.zeros((SUBLANES, conv_ch), F32)
                st_s[i] = jnp.zeros(st_s.shape[1:], F32)
                for buf in (klo_s, khi_s, vlo_s, vhi_s):
                    buf[i, 0:WINDOW, :] = jnp.zeros((WINDOW, kvw), BF16)

    a_mxu, a_late, a_vpu, b_first, b_front, b_scans, b_merge = [], [], [], [], [], [], []
    h_b, z_b, xs_b, bc_b, dt_b, q_b, klo_b, khi_b, vlo_b, vhi_b = (
        h_s, z_s, xs_s, bc_s, dt_s, q_s, klo_s, khi_s, vlo_s, vhi_s)

    def a_norm():
        for i in range(ns):
            x = x_ref[i]
            shift, scale = mod_ref[i, 0:1, :], mod_ref[i, 1:2, :]
            h_s[i * tq:(i + 1) * tq, :] = (x * _rms_scale(x) * nmg_ref[...] * (1.0 + scale) + shift).astype(BF16)
    a_mxu.append((150, a_norm))

    def a_xbc(j):
        res = _dot(h_s[...], wall_ref[:, cols["xbc"] + j:cols["xbc"] + j + blk])
        for i in range(ns):
            cb_s[i * stride + SUBLANES:(i + 1) * stride, j:j + blk] = res[i * tq:(i + 1) * tq]
    for j in range(0, conv_ch, blk):
        a_mxu.append((256, functools.partial(a_xbc, j)))

    def a_dt():
        dt_s[...] = _dot(h_s[...], wall_ref[:, cols["dt"]:cols["dt"] + LANES])
    a_mxu.append((30, a_dt))

    def head_rms(v):
        v2 = v * v
        sa = jnp.sum(v2 * lowf, axis=-1, keepdims=True)
        sb = jnp.sum(v2 * highf, axis=-1, keepdims=True)
        return lax.rsqrt((sa * lowf + sb * highf) * (1.0 / HEAD_DIM) + EPS)

    def a_q(p2):
        qv = _dot(h_s[...], wall_ref[:, cols["q"] + p2 * 2 * LANES:cols["q"] + (p2 + 1) * 2 * LANES])
        for i in range(2):
            v = qv[:, i * LANES:(i + 1) * LANES]
            q_s[:, (2 * p2 + i) * LANES:(2 * p2 + i + 1) * LANES] = (v * head_rms(v) * qg_ref[...]).astype(BF16)
    for p2 in range(n_pairs // 2):
        a_mxu.append((160, functools.partial(a_q, p2)))

    def a_kv():
        nkv = N_KV_HEADS * HEAD_DIM
        kd = _dot(h_s[...], wall_ref[:, cols["k"]:cols["k"] + nkv])
        vd = _dot(h_s[...], wall_ref[:, cols["v"]:cols["v"] + nkv])
        for j in range(N_KV_HEADS // 2):
            ls = slice(j * LANES, (j + 1) * LANES)
            kx = kd[:, ls]
            kn = kx * head_rms(kx) * kg_ref[...]
            vx = vd[:, ls]
            knat_s[:, ls] = kn
            vnat_s[:, ls] = vx
            for val, lo_s, hi_s in ((kn, klo_s, khi_s), (vx, vlo_s, vhi_s)):
                rol = pltpu.roll(val, HEAD_DIM, 1)
                a, b = 2 * j * LANES, (2 * j + 1) * LANES
                parts = ((lo_s, a, val * lowf), (hi_s, a, rol * highf), (lo_s, b, rol * lowf), (hi_s, b, val * highf))
                for dst_s, c0, piece in parts:
                    piece = piece.astype(BF16)
                    for i in range(ns):
                        dst_s[i, WINDOW:WINDOW + tq, c0:c0 + LANES] = piece[i * tq:(i + 1) * tq]
    a_mxu.append((300, a_kv))

    def a_z(j):
        zz = _dot(h_s[...], wall_ref[:, cols["z"] + j:cols["z"] + j + blk])
        z_s[:, j:j + blk] = zz * _sigmoid(zz)
    for j in range(0, d_inner, blk):
        a_late.append((256, functools.partial(a_z, j)))

    def a_conv(j):
        for i in range(ns):
            r0 = i * stride
            acc = cb_s[r0 + SUBLANES:r0 + stride, j:j + blk] * convw_ref[CONV_K - 1:CONV_K, j:j + blk]
            for k in range(CONV_K - 1):
                o = r0 + SUBLANES - (CONV_K - 1) + k
                acc = acc + cb_s[o:o + tq, j:j + blk] * convw_ref[k:k + 1, j:j + blk]
            acc = acc + convb_ref[:, j:j + blk]
            u = acc * _sigmoid(acc)
            if j < d_inner:
                xs_s[i * tq:(i + 1) * tq, j:j + blk] = u
            else:
                bc_s[i * tq:(i + 1) * tq, j - d_inner:j - d_inner + blk] = u
    for j in range(0, conv_ch, blk):
        a_vpu.append((300, functools.partial(a_conv, j)))

    def a_tail():
        for i in range(ns):
            cb_s[i * stride:i * stride + SUBLANES, :] = cb_s[i * stride + tq:(i + 1) * stride, :]
    if not has_past:
        a_vpu.append((10, a_tail))

    sv = {}
    npg = gw // LANES

    def b_prep():
        a_row = -jnp.exp(alog_ref[...]) * LOG2E
        head_lanes = jnp.where(lax.broadcasted_iota(jnp.int32, (1, LANES), 1) < n_ssm_heads, 1.0, 0.0)
        xv = dt_b[...] + dtb_ref[...]
        dt = (jnp.maximum(xv, 0.0) + jnp.log1p(jnp.exp(-jnp.abs(xv)))) * head_lanes
        da = dt * a_row
        d1 = da.astype(BF16)
        r1 = da - d1.astype(F32)
        d2 = r1.astype(BF16)
        d3 = (r1 - d2.astype(F32)).astype(BF16)
        chunk_shift = CHUNK.bit_length() - 1
        br = lax.broadcasted_iota(jnp.int32, (t, t), 0)
        bc = lax.broadcasted_iota(jnp.int32, (t, t), 1)
        same_chunk = jnp.right_shift(br, chunk_shift) == jnp.right_shift(bc, chunk_shift)
        tril = jnp.where((br >= bc) & same_chunk, 1.0, 0.0).astype(BF16)
        acum = _dot(tril, d1) + _dot(tril, d2) + _dot(tril, d3)

        def pack_hi_lo(v):
            hi = v.astype(BF16).astype(F32)
            return (hi + pltpu.roll(v - hi, n_ssm_heads, 1)).astype(BF16)

        sv["pdt"], sv["pac"] = pack_hi_lo(dt), pack_hi_lo(acum)
        lowf_g = jnp.concatenate([lowf] * npg, axis=1)
        sv["lowf_g"], sv["highf_g"] = lowf_g, 1.0 - lowf_g
    b_front.append((300, b_prep))

    def b_expand(g):
        gs = slice(g * gw, (g + 1) * gw)
        e_g = e_ref[:, gs]
        xdt = xs_b[:, gs] * _dot(sv["pdt"], e_g)
        ae = _dot(sv["pac"], e_g)
        xlo_s[:, gs] = (xdt * sv["lowf_g"]).astype(BF16)
        xhi_s[:, gs] = (xdt * sv["highf_g"]).astype(BF16)
        eae_s[:, gs] = jnp.exp2(ae)
        ae_s[:, gs] = ae
        for c in range(nc):
            rows = slice(c * CHUNK, (c + 1) * CHUNK)
            ae_c = ae[rows]
            ae_l = ae_c[CHUNK - 1:CHUNK, :]
            xt_s[rows, gs] = (xdt[rows] * jnp.exp2(ae_l - ae_c)).astype(BF16)
            edl_s[c * SUBLANES:(c + 1) * SUBLANES, gs] = jnp.broadcast_to(jnp.exp2(ae_l), (SUBLANES, gw))

    def b_decay(g, c):
        rows = slice(c * CHUNK, (c + 1) * CHUNK)
        bb = bc_b[rows, g * D_STATE:(g + 1) * D_STATE].astype(BF16)
        cc = bc_b[rows, (SSM_GROUPS + g) * D_STATE:(SSM_GROUPS + g + 1) * D_STATE].astype(BF16)
        cb2 = _dot_nt(cc, jnp.concatenate([bb, bb], axis=0))
        for i in range(npg):
            acol = ae_s[rows, g * gw + i * LANES:g * gw + (i + 1) * LANES]
            arow = jnp.sum(jnp.where(diag, acol, 0.0), axis=0, keepdims=True)
            dec = jnp.exp2(jnp.where(causal, acol - arow, NEG))
            mp_s[(c * SSM_GROUPS + g) * npg + i] = (cb2 * dec).astype(BF16)

    def b_yin(g, c):
        rows = slice(c * CHUNK, (c + 1) * CHUNK)
        zblk = jnp.zeros((2 * CHUNK, LANES), BF16)
        for i in range(0, npg, 2):
            base = (c * SSM_GROUPS + g) * npg + i
            l0 = slice(g * gw + i * LANES, g * gw + (i + 1) * LANES)
            l1 = slice(g * gw + (i + 1) * LANES, g * gw + (i + 2) * LANES)
            x0 = jnp.concatenate([xlo_s[rows, l0], xhi_s[rows, l0]], axis=0)
            x1 = jnp.concatenate([xlo_s[rows, l1], xhi_s[rows, l1]], axis=0)
            rhs = jnp.concatenate([jnp.concatenate([x0, zblk], axis=1), jnp.concatenate([zblk, x1], axis=1)], axis=0)
            lhs = jnp.concatenate([mp_s[base], mp_s[base + 1]], axis=1)
            yin_s[rows, g * gw + i * LANES:g * gw + (i + 2) * LANES] = _dot(lhs, rhs)

    for g in range(SSM_GROUPS):
        b_front.append((400, functools.partial(b_expand, g)))
        for c in range(nc):
            b_front.append((150, functools.partial(b_decay, g, c)))
    for g in range(SSM_GROUPS):
        for c in range(nc):
            b_front.append((60, functools.partial(b_yin, g, c)))

    ppk = n_pairs // N_KV_HEADS

    kw = WINDOW + tq
    khw = bias_ref.shape[2] // 2
    assert khw == _key_width(tq) and khw & (khw - 1) == 0

    def slab(lo_b, hi_b, i, kv):
        pad = [jnp.zeros((khw - kw, LANES), BF16)] if khw > kw else []
        c0 = kv * LANES
        return jnp.concatenate([lo_b[i, 0:kw, c0:c0 + LANES]] + pad + [hi_b[i, 0:kw, c0:c0 + LANES]] + pad, axis=0)

    def b_scores(i, kv):
        rows = slice(i * tq, (i + 1) * tq)
        kbd = slab(klo_b, khi_b, i, kv)
        for pp in range(ppk):
            p = kv * ppk + pp
            table = p if has_past else jnp.where(s_idx == 0, n_pairs + p, p)
            sc_s[i * n_pairs + p] = _dot_nt(q_b[rows, p * LANES:(p + 1) * LANES], kbd) + bias_ref[table]

    def b_softmax(i, cl, p):
        rows = slice(cl * CHUNK, (cl + 1) * CHUNK)
        es, rs = [], []
        for hh in range(2):
            sh = sc_s[i * n_pairs + p, rows, hh * khw:(hh + 1) * khw]
            sink = sinks_ref[layer, 2 * p + hh] * LOG2E
            m = jnp.maximum(jnp.max(sh, axis=-1, keepdims=True), sink)
            e = jnp.exp2(sh - m)
            den = jnp.sum(e, axis=-1, keepdims=True) + jnp.exp2(sink - m)
            es.append(e.astype(BF16))
            rs.append(1.0 / den)
        pe_s[i * n_pairs + p, rows, :] = jnp.concatenate(es, axis=1)
        rd_s[i * n_pairs + p, rows, :] = rs[0] * lowf + rs[1] * highf

    def b_pv(i, kv):
        rows = slice(i * tq, (i + 1) * tq)
        vbd = slab(vlo_b, vhi_b, i, kv)
        for pp in range(ppk):
            p = kv * ppk + pp
            o = _dot(pe_s[i * n_pairs + p], vbd) * rd_s[i * n_pairs + p]
            o_s[rows, p * LANES:(p + 1) * LANES] = o.astype(BF16)

    for i in range(ns):
        for kv in range(N_KV_HEADS):
            b_first.append((70 * ncs, functools.partial(b_scores, i, kv)))
    for i in range(ns):
        for cl in range(ncs):
            for p in range(n_pairs):
                b_front.append((110, functools.partial(b_softmax, i, cl, p)))
    for i in range(ns):
        for kv in range(N_KV_HEADS):
            b_front.append((70 * ncs, functools.partial(b_pv, i, kv)))

    def b_scan(c, g):
        rows = slice(c * CHUNK, (c + 1) * CHUNK)
        gs = slice(g * gw, (g + 1) * gw)
        bb = bc_b[rows, g * D_STATE:(g + 1) * D_STATE].astype(BF16)
        cc = bc_b[rows, (SSM_GROUPS + g) * D_STATE:(SSM_GROUPS + g + 1) * D_STATE].astype(BF16)
        i = c // ncs
        st = st_s[i, :, gs]
        y = yin_s[rows, gs] + _dot(cc, st.astype(BF16)) * eae_s[rows, gs] + xs_b[rows, gs] * dskip_ref[:, gs]
        y = y * z_b[rows, gs]
        y = y * _rms_scale(y) * ssmg_ref[:, gs]
        y_s[rows, gs] = y.astype(BF16)
        st_s[i, :, gs] = st * edl_s[c * SUBLANES:c * SUBLANES + 1, gs] + _dot_tn(bb, xt_s[rows, gs])
    for c in range(nc):
        for g in range(SSM_GROUPS):
            b_scans.append((160, functools.partial(b_scan, c, g)))

    def b_gate(i, name):
        gate_s[i] = _sigmoid(_dot(h_b[...], wall_ref[:, cols[name]:cols[name] + d_model]))
    b_gates = [(512, functools.partial(b_gate, 0, "gs")), (512, functools.partial(b_gate, 1, "ga"))]

    def wcol(name):
        return wall_ref[:, cols[name]:cols[name] + d_model]

    def b_branch(i, src_s, names):
        acc = _dot(src_s[:, 0:d_model], wcol(names[0]))
        for k in range(1, len(names)):
            acc = acc + _dot(src_s[:, k * d_model:(k + 1) * d_model], wcol(names[k]))
        part = gate_s[i] * acc
        mix_s[...] = part if i == 0 else mix_s[...] + part
    b_merge.append((1000, functools.partial(b_branch, 0, y_s, ["wbs%d" % k for k in range(d_inner // d_model)])))
    b_merge.append((500, functools.partial(b_branch, 1, o_s, ["wba"])))

    def b_out():
        out = _dot(mix_s[...].astype(BF16), wcol("wo"))
        for i in range(ns):
            xo_ref[i] = x_ref[i] + mod_ref[i, 2:3, :] * out[i * tq:(i + 1) * tq]
    b_merge.append((500, b_out))

    _interleave(a_mxu)
    _interleave(a_vpu, a_late + b_gates)
    _interleave(b_first + b_front + b_scans)
    _interleave(b_merge)

    if not has_past:
        for buf in (klo_s, khi_s, vlo_s, vhi_s):
            for i in range(ns):
                buf[i, 0:WINDOW, :] = buf[i, tq:tq + WINDOW, :]

    @pl.when(s_idx == n_steps - 1)
    def _():
        kr = kn_ref.shape[1]
        for i in range(ns):
            convn_ref[i] = cb_s[i * stride + tq:(i + 1) * stride, :]
            ssmn_ref[i] = st_s[i].T
            kn_ref[i] = knat_s[(i + 1) * tq - kr:(i + 1) * tq, :]
            vn_ref[i] = vnat_s[(i + 1) * tq - kr:(i + 1) * tq, :]


def _const_spec(shape, single_buffer=False):
    idx = (0,) * len(shape)
    if single_buffer:
        return pl.BlockSpec(shape, lambda *_: idx, pipeline_mode=pl.Buffered(1))
    return pl.BlockSpec(shape, lambda *_: idx)


def _layer_spec(stacked, layer):
    blk = (1,) + stacked.shape[1:]
    idx = (layer,) + (0,) * (stacked.ndim - 1)
    if math.prod(blk) > (1 << 20):
        return pl.BlockSpec(blk, lambda *_: idx, pipeline_mode=pl.Buffered(1))
    return pl.BlockSpec(blk, lambda *_: idx)


def _mixer_layer(x, mod, past, lw, pair_bias, e_mat, *, nc, ns):
    bsz, seq, d_model = x.shape
    t = nc * CHUNK
    tq = t // ns
    has_past = past is not None
    d_inner, n_heads = lw["d_inner"], lw["n_heads"]
    conv_ch = d_inner + 2 * SSM_GROUPS * D_STATE
    kvw = N_KV_HEADS * LANES
    kv_rows = seq if has_past else WINDOW
    assert nc % ns == 0 and bsz % ns == 0 and seq % tq == 0
    assert has_past and seq == tq or not has_past and tq >= WINDOW

    x_spec = pl.BlockSpec((ns, tq, d_model), lambda b, s: (b, s, 0))

    def per_seq(*shape):
        return pl.BlockSpec((ns,) + shape, lambda b, s: (b,) + (0,) * len(shape))

    in_specs = [x_spec, per_seq(6, d_model)]
    args = [x, mod]
    if has_past:
        in_specs += [per_seq(SUBLANES, conv_ch), per_seq(d_inner, D_STATE),
                     per_seq(WINDOW, N_KV_HEADS * HEAD_DIM), per_seq(WINDOW, N_KV_HEADS * HEAD_DIM)]
        args += list(past)
    consts = [lw["nmg"], lw["wall"], lw["convw"], lw["convb"], lw["dtb"], lw["alog"], lw["dskip"], lw["ssmg"],
              lw["qg"], lw["kg"]]
    in_specs += [_layer_spec(a, lw["layer"]) for a in consts]
    in_specs.append(pl.BlockSpec(memory_space=pltpu.SMEM))
    tail = [pair_bias, e_mat]
    in_specs += [_const_spec(a.shape, single_buffer=a.size > (1 << 18)) for a in tail]
    args += consts + [lw["sinks"]] + tail

    out_shape = (jax.ShapeDtypeStruct(x.shape, F32),
                 jax.ShapeDtypeStruct((bsz, SUBLANES, conv_ch), F32),
                 jax.ShapeDtypeStruct((bsz, d_inner, D_STATE), F32),
                 jax.ShapeDtypeStruct((bsz, kv_rows, N_KV_HEADS * HEAD_DIM), F32),
                 jax.ShapeDtypeStruct((bsz, kv_rows, N_KV_HEADS * HEAD_DIM), F32))
    out_specs = (x_spec, per_seq(SUBLANES, conv_ch), per_seq(d_inner, D_STATE),
                 per_seq(kv_rows, N_KV_HEADS * HEAD_DIM), per_seq(kv_rows, N_KV_HEADS * HEAD_DIM))
    scratch = [
        pltpu.VMEM((t, d_model), BF16),
        pltpu.VMEM((ns * (SUBLANES + tq), conv_ch), F32),
        pltpu.VMEM((t, d_inner), F32),
        pltpu.VMEM((t, d_inner), F32),
        pltpu.VMEM((t, 2 * SSM_GROUPS * D_STATE), F32),
        pltpu.VMEM((t, LANES), F32),
        pltpu.VMEM((t, d_model), BF16),
        pltpu.VMEM((ns, WINDOW + tq, kvw), BF16),
        pltpu.VMEM((ns, WINDOW + tq, kvw), BF16),
        pltpu.VMEM((ns, WINDOW + tq, kvw), BF16),
        pltpu.VMEM((ns, WINDOW + tq, kvw), BF16),
        pltpu.VMEM((t, d_inner), BF16),
        pltpu.VMEM((t, d_model), BF16),
        pltpu.VMEM((ns, D_STATE, d_inner), F32),
        pltpu.VMEM((t, d_inner), BF16),
        pltpu.VMEM((t, d_inner), BF16),
        pltpu.VMEM((t, d_inner), BF16),
        pltpu.VMEM((t, d_inner), F32),
        pltpu.VMEM((nc * SUBLANES, d_inner), F32),
        pltpu.VMEM((nc * d_inner // LANES, CHUNK, LANES), BF16),
        pltpu.VMEM((t, d_inner), F32),
        pltpu.VMEM((ns * n_heads // 2, tq, 2 * _key_width(tq)), F32),
        pltpu.VMEM((ns * n_heads // 2, tq, 2 * _key_width(tq)), BF16),
        pltpu.VMEM((ns * n_heads // 2, tq, LANES), F32),
        pltpu.VMEM((t, N_KV_HEADS * HEAD_DIM), F32),
        pltpu.VMEM((t, N_KV_HEADS * HEAD_DIM), F32),
        pltpu.VMEM((t, d_inner), F32),
        pltpu.VMEM((t, d_model), F32),
        pltpu.VMEM((2, t, d_model), F32),
    ]
    kern = functools.partial(_mixer_kernel, nc=nc, ns=ns, has_past=has_past, d_model=d_model, d_inner=d_inner,
                             n_heads=n_heads, n_ssm_heads=d_inner // SSM_HEAD_DIM, cols=lw["cols"],
                             layer=lw["layer"])
    return pl.pallas_call(
        kern, grid=(bsz // ns, seq // tq), in_specs=in_specs, out_specs=out_specs, out_shape=out_shape,
        scratch_shapes=scratch,
        compiler_params=pltpu.CompilerParams(dimension_semantics=("arbitrary", "arbitrary"),
                                             vmem_limit_bytes=VMEM_LIMIT),
        name="mixer_past" if has_past else "mixer_prompt",
    )(*args)


def _ffn_kernel(x_ref, mod_ref, g_ref, wgu_ref, wd_ref, o_ref):
    nb, tm, _ = x_ref.shape
    d_ff = wd_ref.shape[1]
    wg_ref, wu_ref, wd_ref = wgu_ref.at[0, :, 0:d_ff], wgu_ref.at[0, :, d_ff:2 * d_ff], wd_ref.at[0]
    hs = []
    for i in range(nb):
        x = x_ref[i]
        shift, scale = mod_ref[i, 3:4, :], mod_ref[i, 4:5, :]
        hs.append((x * _rms_scale(x) * g_ref[0] * (1.0 + scale) + shift).astype(BF16))
    h = hs[0] if nb == 1 else jnp.concatenate(hs, axis=0)
    gate = _dot(h, wg_ref[...])
    up = _dot(h, wu_ref[...])
    act = (gate * _sigmoid(gate) * up).astype(BF16)
    y = _dot(act, wd_ref[...])
    for i in range(nb):
        o_ref[i] = x_ref[i] + mod_ref[i, 5:6, :] * y[i * tm:(i + 1) * tm]


def _ffn_layer(x, mod, lw, *, nb, tm):
    bsz, seq, d_model = x.shape
    assert seq % tm == 0 and bsz % nb == 0
    xspec = pl.BlockSpec((nb, tm, d_model), lambda b, s: (b, s, 0))
    return pl.pallas_call(
        _ffn_kernel, grid=(bsz // nb, seq // tm),
        in_specs=[xspec, pl.BlockSpec((nb, 6, d_model), lambda b, s: (b, 0, 0)),
                  _layer_spec(lw["nfg"], lw["layer"]), _layer_spec(lw["wgu"], lw["layer"]),
                  _layer_spec(lw["wd"], lw["layer"])],
        out_specs=xspec, out_shape=jax.ShapeDtypeStruct(x.shape, F32),
        compiler_params=pltpu.CompilerParams(dimension_semantics=("arbitrary", "arbitrary"),
                                             vmem_limit_bytes=VMEM_LIMIT),
        name="ffn",
    )(x, mod, lw["nfg"], lw["wgu"], lw["wd"])


def _stacked_weights(p):
    d_model = p["w_out"].shape[2]
    d_inner = p["w_br_ssm"].shape[1]
    n_ssm_heads = p["dt_bias"].shape[1]
    nq = p["w_br_attn"].shape[1]
    nkv = N_KV_HEADS * HEAD_DIM
    conv_ch = d_inner + 2 * SSM_GROUPS * D_STATE
    wi = p["w_in"]
    bounds = [0, d_inner, d_inner + conv_ch, d_inner + conv_ch + n_ssm_heads]
    bounds += [bounds[-1] + nq, bounds[-1] + nq + nkv, bounds[-1] + nq + 2 * nkv]
    bounds += [bounds[-1] + d_model, bounds[-1] + 2 * d_model]
    z, xbc, dtc, q, k, v, gs, ga = [wi[:, :, a:b] for a, b in zip(bounds[:-1], bounds[1:])]
    assert nq == d_model and d_inner % d_model == 0
    pieces = [("z", z), ("xbc", xbc), ("dt", jnp.pad(dtc, ((0, 0), (0, 0), (0, LANES - n_ssm_heads)))),
              ("q", q), ("k", k), ("v", v), ("gs", gs), ("ga", ga),
              ("wba", p["w_br_attn"]), ("wo", p["w_out"])]
    pieces += [("wbs%d" % i, p["w_br_ssm"][:, i * d_model:(i + 1) * d_model]) for i in range(d_inner // d_model)]
    cols, off = {}, 0
    for name, w in pieces:
        cols[name] = off
        off += w.shape[2]
    return dict(cols=cols, wall=jnp.concatenate([w for _, w in pieces], axis=2).astype(BF16),
                wgu=p["w_gate_up"].astype(BF16), wd=p["w_down"].astype(BF16),
                d_inner=d_inner, n_heads=nq // HEAD_DIM)


def _stacked_vectors(p):
    n_ssm_heads = p["dt_bias"].shape[1]
    row = lambda a: a[:, None, :]
    rep = lambda a: row(jnp.repeat(a, SSM_HEAD_DIM, axis=1))
    padl = lambda a: row(jnp.pad(a, ((0, 0), (0, LANES - n_ssm_heads))))
    return dict(
        nmg=row(p["norm_mix_g"]), nfg=row(p["norm_ffn_g"]), convw=p["conv_w"], convb=row(p["conv_b"]),
        dtb=padl(p["dt_bias"]), alog=padl(p["a_log"]), dskip=rep(p["d_skip"]), ssmg=row(p["ssm_norm_g"]),
        qg=row(jnp.tile(p["q_norm_g"] * (HEAD_DIM ** -0.5 * LOG2E), (1, 2))), kg=row(jnp.tile(p["k_norm_g"], (1, 2))),
        sinks=p["sinks"],
    )


def _expand_matrix(n_ssm_heads):
    assert 3 * n_ssm_heads <= LANES
    r = jnp.arange(LANES)[:, None]
    src = (r < 2 * n_ssm_heads) | (r >= LANES - n_ssm_heads)
    c = jnp.arange(n_ssm_heads * SSM_HEAD_DIM)[None, :] // SSM_HEAD_DIM
    return (src & ((r % n_ssm_heads) == c)).astype(BF16)


def kernel(x_prompt, x_sample, cache_k, cache_v, state_conv, state_ssm, c_prompt, c_sample, rel_bias, ada_w, ada_b, norm_mix_g, norm_ffn_g, w_in, conv_w, conv_b, dt_bias, a_log, d_skip, ssm_norm_g, q_norm_g, k_norm_g, sinks, w_br_ssm, w_br_attn, w_out, w_gate_up, w_down):
    p = dict(norm_mix_g=norm_mix_g, norm_ffn_g=norm_ffn_g, w_in=w_in, conv_w=conv_w, conv_b=conv_b,
             dt_bias=dt_bias, a_log=a_log, d_skip=d_skip, ssm_norm_g=ssm_norm_g, q_norm_g=q_norm_g,
             k_norm_g=k_norm_g, sinks=sinks, w_br_ssm=w_br_ssm, w_br_attn=w_br_attn, w_out=w_out,
             w_gate_up=w_gate_up, w_down=w_down)
    depth = w_in.shape[0]
    bp, seq_p, d_model = x_prompt.shape
    bs, seq_s, _ = x_sample.shape
    n_ssm_heads, ssm_p, d_state = state_ssm.shape[2:]
    d_inner = n_ssm_heads * ssm_p
    nkv = N_KV_HEADS * HEAD_DIM

    mod = _ada_mod(jnp.concatenate([c_prompt, c_sample], axis=0), ada_w, ada_b)
    mod = mod.reshape(depth, bp + bs, 6, d_model)
    assert MIXER_CHUNKS * CHUNK >= WINDOW
    bias_p = jnp.concatenate([_pair_bias(rel_bias, MIXER_CHUNKS, True), _pair_bias(rel_bias, MIXER_CHUNKS, False)])
    bias_s = _pair_bias(rel_bias, 1, True)
    e_mat = _expand_matrix(n_ssm_heads)

    xp, xs = x_prompt, x_sample
    outs = [[] for _ in range(8)]
    stacked = dict(_stacked_weights(p), **_stacked_vectors(p))
    for l in range(depth):
        lw = dict(stacked, layer=l)
        mod_p, mod_s = mod[l, :bp], mod[l, bp:]
        past = (jnp.pad(state_conv[l], ((0, 0), (SUBLANES - (CONV_K - 1), 0), (0, 0))),
                state_ssm[l].reshape(bs, d_inner, d_state),
                cache_k[l].reshape(bs, WINDOW, nkv), cache_v[l].reshape(bs, WINDOW, nkv))
        xp, cvp, ssp, kp, vp = _mixer_layer(xp, mod_p, None, lw, bias_p, e_mat, nc=MIXER_CHUNKS, ns=1)
        ns_s = math.gcd(bs, MIXER_CHUNKS)
        xs, cvs, sss, ksn, vsn = _mixer_layer(xs, mod_s, past, lw, bias_s, e_mat, nc=ns_s, ns=ns_s)
        xp = _ffn_layer(xp, mod_p, lw, nb=1, tm=min(FFN_ROWS, seq_p))
        nb_s = math.gcd(bs, max(1, FFN_ROWS // seq_s))
        xs = _ffn_layer(xs, mod_s, lw, nb=nb_s, tm=seq_s)
        tail = SUBLANES - (CONV_K - 1)
        for lst, val in zip(outs, (
                cvp[:, tail:], cvs[:, tail:],
                ssp.reshape(bp, n_ssm_heads, ssm_p, d_state), sss.reshape(bs, n_ssm_heads, ssm_p, d_state),
                kp.reshape(bp, WINDOW, N_KV_HEADS, HEAD_DIM), ksn.reshape(bs, seq_s, N_KV_HEADS, HEAD_DIM),
                vp.reshape(bp, WINDOW, N_KV_HEADS, HEAD_DIM), vsn.reshape(bs, seq_s, N_KV_HEADS, HEAD_DIM))):
            lst.append(val)
    return (xp, xs) + tuple(jnp.stack(o) for o in outs)
```

```python
import functools
import math

import jax
import jax.numpy as jnp
from jax import lax
from jax.experimental import pallas as pl
from jax.experimental.pallas import tpu as pltpu

F32 = jnp.float32
BF16 = jnp.bfloat16

CHUNK = 64
EPS = 1e-6
SSM_GROUPS = 4
D_STATE = 128
SSM_HEAD_DIM = 64
CONV_K = 4
HEAD_DIM = 64
N_KV_HEADS = 4
WINDOW = 128
N_BUCKETS = 32
MAX_DISTANCE = 128

LANES = 128
SUBLANES = 8
KEYS = WINDOW + CHUNK
NEG = -1e30
LOG2E = math.log2(math.e)
VMEM_LIMIT = 56 * 1024 * 1024
FFN_ROWS = 1024
MIXER_CHUNKS = 2
COL_BLK = 512


def _dot(a, b):
    return jnp.dot(a, b, preferred_element_type=F32)


def _dot_nt(a, b):
    return lax.dot_general(a, b, (((1,), (1,)), ((), ())), preferred_element_type=F32)


def _dot_tn(a, b):
    return lax.dot_general(a, b, (((0,), (0,)), ((), ())), preferred_element_type=F32)


def _sigmoid(x):
    return 1.0 / (1.0 + jnp.exp(-x))


def _interleave(*streams):
    totals = [float(sum(w for w, _ in st)) or 1.0 for st in streams]
    pos, done = [0] * len(streams), [0.0] * len(streams)
    while True:
        live = [k for k in range(len(streams)) if pos[k] < len(streams[k])]
        if not live:
            return
        k = min(live, key=lambda k: done[k] / totals[k])
        w, fn = streams[k][pos[k]]
        fn()
        pos[k] += 1
        done[k] += w


def _rms_scale(x):
    return lax.rsqrt(jnp.mean(x * x, axis=-1, keepdims=True) + EPS)


def _ada_kernel(c_ref, w_ref, b_ref, o_ref):
    c = c_ref[...]
    s = (c * _sigmoid(c)).astype(BF16)
    o_ref[0] = _dot(s, w_ref[0].astype(BF16)) + b_ref[0]


def _ada_mod(c_all, ada_w, ada_b):
    depth, d, six_d = ada_w.shape
    n = c_all.shape[0]
    wblk = six_d // 2
    return pl.pallas_call(
        _ada_kernel,
        grid=(depth, six_d // wblk),
        in_specs=[
            pl.BlockSpec((n, d), lambda l, j: (0, 0)),
            pl.BlockSpec((1, d, wblk), lambda l, j: (l, 0, j)),
            pl.BlockSpec((1, 1, wblk), lambda l, j: (l, 0, j)),
        ],
        out_specs=pl.BlockSpec((1, n, wblk), lambda l, j: (l, 0, j)),
        out_shape=jax.ShapeDtypeStruct((depth, n, six_d), F32),
        name="ada_mod",
    )(c_all, ada_w, ada_b.reshape(depth, 1, six_d))


def _bias_kernel(bkt_ref, rb_ref, o_ref, *, n_heads):
    bkt = bkt_ref[...]
    khw = bkt.shape[1]
    for h in range(n_heads):
        def body(b, acc, h=h):
            return jnp.where(bkt == b, rb_ref[b, h] * LOG2E, acc)
        acc = lax.fori_loop(0, N_BUCKETS, body, jnp.full(bkt.shape, NEG, F32))
        o_ref[h // 2, :, (h % 2) * khw:(h % 2 + 1) * khw] = acc


def _t5_bucket(rel):
    n = -rel
    half = N_BUCKETS // 2
    max_exact = half // 2
    ret = jnp.where(n < 0, half, 0)
    n = jnp.abs(n)
    nf = jnp.maximum(n, 1).astype(F32)
    large = max_exact + (jnp.log(nf / max_exact) / math.log(MAX_DISTANCE / max_exact)
                         * (half - max_exact)).astype(jnp.int32)
    large = jnp.minimum(large, half - 1)
    return ret + jnp.where(n < max_exact, n, large)


def _key_width(tq):
    return pl.cdiv(WINDOW + tq, LANES) * LANES


def _pair_bias(rel_bias, ncs, window_valid):
    n_heads = rel_bias.shape[1]
    tq = ncs * CHUNK
    khw = _key_width(tq)
    q = jnp.arange(tq)[:, None]
    key = jnp.arange(khw)[None, :]
    band = key - (q // CHUNK) * CHUNK
    rel = band - WINDOW - q % CHUNK
    seen = (band >= 0) & (band < KEYS) & ((key >= WINDOW) | window_valid)
    bkt = jnp.where(seen, _t5_bucket(rel), -1).astype(jnp.int32)
    return pl.pallas_call(
        functools.partial(_bias_kernel, n_heads=n_heads),
        in_specs=[pl.BlockSpec((tq, khw), lambda: (0, 0)),
                  pl.BlockSpec(memory_space=pltpu.SMEM)],
        out_specs=pl.BlockSpec((n_heads // 2, tq, 2 * khw), lambda: (0, 0, 0)),
        out_shape=jax.ShapeDtypeStruct((n_heads // 2, tq, 2 * khw), F32),
        name="pair_bias",
    )(bkt, rel_bias)


def _mixer_kernel(*refs, nc, ns, has_past, d_model, d_inner, n_heads, n_ssm_heads, cols, layer):
    t = nc * CHUNK
    tq = t // ns
    ncs = nc // ns
    stride = SUBLANES + tq
    blk = COL_BLK
    conv_ch = d_inner + 2 * SSM_GROUPS * D_STATE
    gw = d_inner // SSM_GROUPS
    n_pairs = n_heads // 2
    kvw = N_KV_HEADS * LANES
    it = iter(refs)
    x_ref, mod_ref = next(it), next(it)
    if has_past:
        conv0_ref, ssm0_ref, k0_ref, v0_ref = next(it), next(it), next(it), next(it)
    (nmg_ref, wall_ref, convw_ref, convb_ref, dtb_ref, alog_ref, dskip_ref, ssmg_ref, qg_ref, kg_ref,
     sinks_ref, bias_ref, e_ref) = [next(it) for _ in range(13)]
    (nmg_ref, wall_ref, convw_ref, convb_ref, dtb_ref, alog_ref, dskip_ref, ssmg_ref, qg_ref, kg_ref) = [
        r.at[0] for r in (nmg_ref, wall_ref, convw_ref, convb_ref, dtb_ref, alog_ref, dskip_ref, ssmg_ref,
                          qg_ref, kg_ref)]
    xo_ref, convn_ref, ssmn_ref, kn_ref, vn_ref = [next(it) for _ in range(5)]
    (h_s, cb_s, z_s, xs_s, bc_s, dt_s, q_s, klo_s, khi_s, vlo_s, vhi_s, y_s, o_s, st_s,
     xlo_s, xhi_s, xt_s, eae_s, edl_s, mp_s, yin_s, sc_s, pe_s, rd_s, knat_s, vnat_s, ae_s, mix_s,
     gate_s) = list(it)

    s_idx = pl.program_id(1)
    n_steps = pl.num_programs(1)

    lane = lax.broadcasted_iota(jnp.int32, (CHUNK, LANES), 1)
    row = lax.broadcasted_iota(jnp.int32, (CHUNK, LANES), 0)
    causal = row >= (lane & (CHUNK - 1))
    diag = row == (lane & (CHUNK - 1))
    lowk = lax.broadcasted_iota(jnp.int32, (WINDOW, LANES), 1) < HEAD_DIM
    lowf = jnp.where(lax.broadcasted_iota(jnp.int32, (1, LANES), 1) < HEAD_DIM, 1.0, 0.0)
    highf = 1.0 - lowf

    @pl.when(s_idx == 0)
    def _():
        for i in range(ns):
            if has_past:
                cb_s[i * stride:i * stride + SUBLANES, :] = conv0_ref[i]
                st_s[i] = ssm0_ref[i].T
                for src_ref, lo_s, hi_s in ((k0_ref, klo_s, khi_s), (v0_ref, vlo_s, vhi_s)):
                    for j in range(N_KV_HEADS // 2):
                        nat = src_ref[i, :, j * LANES:(j + 1) * LANES]
                        rol = pltpu.roll(nat, HEAD_DIM, 1)
                        a, b = 2 * j * LANES, (2 * j + 1) * LANES
                        lo_s[i, 0:WINDOW, a:a + LANES] = jnp.where(lowk, nat, 0.0).astype(BF16)
                        hi_s[i, 0:WINDOW, a:a + LANES] = jnp.where(lowk, 0.0, rol).astype(BF16)
                        lo_s[i, 0:WINDOW, b:b + LANES] = jnp.where(lowk, rol, 0.0).astype(BF16)
                        hi_s[i, 0:WINDOW, b:b + LANES] = jnp.where(lowk, 0.0, nat).astype(BF16)
            else:
                cb_s[i * stride:i * stride + SUBLANES, :] = jnp.zeros((SUBLANES, conv_ch), F32)
                st_s[i] = jnp.zeros(st_s.shape[1:], F32)
                for buf in (klo_s, khi_s, vlo_s, vhi_s):
                    buf[i, 0:WINDOW, :] = jnp.zeros((WINDOW, kvw), BF16)

    a_mxu, a_late, a_vpu, b_first, b_front, b_scans, b_merge = [], [], [], [], [], [], []
    h_b, z_b, xs_b, bc_b, dt_b, q_b, klo_b, khi_b, vlo_b, vhi_b = (
        h_s, z_s, xs_s, bc_s, dt_s, q_s, klo_s, khi_s, vlo_s, vhi_s)

    def a_norm():
        for i in range(ns):
            x = x_ref[i]
            shift, scale = mod_ref[i, 0:1, :], mod_ref[i, 1:2, :]
            h_s[i * tq:(i + 1) * tq, :] = (x * _rms_scale(x) * nmg_ref[...] * (1.0 + scale) + shift).astype(BF16)
    a_mxu.append((150, a_norm))

    def a_xbc(j):
        res = _dot(h_s[...], wall_ref[:, cols["xbc"] + j:cols["xbc"] + j + blk])
        for i in range(ns):
            cb_s[i * stride + SUBLANES:(i + 1) * stride, j:j + blk] = res[i * tq:(i + 1) * tq]
    for j in range(0, conv_ch, blk):
        a_mxu.append((256, functools.partial(a_xbc, j)))

    def a_dt():
        dt_s[...] = _dot(h_s[...], wall_ref[:, cols["dt"]:cols["dt"] + LANES])
    a_mxu.append((30, a_dt))

    def head_rms(v):
        v2 = v * v
        sa = jnp.sum(v2 * lowf, axis=-1, keepdims=True)
        sb = jnp.sum(v2 * highf, axis=-1, keepdims=True)
        return lax.rsqrt((sa * lowf + sb * highf) * (1.0 / HEAD_DIM) + EPS)

    def a_q(p2):
        qv = _dot(h_s[...], wall_ref[:, cols["q"] + p2 * 2 * LANES:cols["q"] + (p2 + 1) * 2 * LANES])
        for i in range(2):
            v = qv[:, i * LANES:(i + 1) * LANES]
            q_s[:, (2 * p2 + i) * LANES:(2 * p2 + i + 1) * LANES] = (v * head_rms(v) * qg_ref[...]).astype(BF16)
    for p2 in range(n_pairs // 2):
        a_mxu.append((160, functools.partial(a_q, p2)))

    def a_kv():
        nkv = N_KV_HEADS * HEAD_DIM
        kd = _dot(h_s[...], wall_ref[:, cols["k"]:cols["k"] + nkv])
        vd = _dot(h_s[...], wall_ref[:, cols["v"]:cols["v"] + nkv])
        for j in range(N_KV_HEADS // 2):
            ls = slice(j * LANES, (j + 1) * LANES)
            kx = kd[:, ls]
            kn = kx * head_rms(kx) * kg_ref[...]
            vx = vd[:, ls]
            knat_s[:, ls] = kn
            vnat_s[:, ls] = vx
            for val, lo_s, hi_s in ((kn, klo_s, khi_s), (vx, vlo_s, vhi_s)):
                rol = pltpu.roll(val, HEAD_DIM, 1)
                a, b = 2 * j * LANES, (2 * j + 1) * LANES
                parts = ((lo_s, a, val * lowf), (hi_s, a, rol * highf), (lo_s, b, rol * lowf), (hi_s, b, val * highf))
                for dst_s, c0, piece in parts:
                    piece = piece.astype(BF16)
                    for i in range(ns):
                        dst_s[i, WINDOW:WINDOW + tq, c0:c0 + LANES] = piece[i * tq:(i + 1) * tq]
    a_mxu.append((300, a_kv))

    def a_z(j):
        zz = _dot(h_s[...], wall_ref[:, cols["z"] + j:cols["z"] + j + blk])
        z_s[:, j:j + blk] = zz * _sigmoid(zz)
    for j in range(0, d_inner, blk):
        a_late.append((256, functools.partial(a_z, j)))

    def a_conv(j):
        for i in range(ns):
            r0 = i * stride
            acc = cb_s[r0 + SUBLANES:r0 + stride, j:j + blk] * convw_ref[CONV_K - 1:CONV_K, j:j + blk]
            for k in range(CONV_K - 1):
                o = r0 + SUBLANES - (CONV_K - 1) + k
                acc = acc + cb_s[o:o + tq, j:j + blk] * convw_ref[k:k + 1, j:j + blk]
            acc = acc + convb_ref[:, j:j + blk]
            u = acc * _sigmoid(acc)
            if j < d_inner:
                xs_s[i * tq:(i + 1) * tq, j:j + blk] = u
            else:
                bc_s[i * tq:(i + 1) * tq, j - d_inner:j - d_inner + blk] = u
    for j in range(0, conv_ch, blk):
        a_vpu.append((300, functools.partial(a_conv, j)))

    def a_tail():
        for i in range(ns):
            cb_s[i * stride:i * stride + SUBLANES, :] = cb_s[i * stride + tq:(i + 1) * stride, :]
    if not has_past:
        a_vpu.append((10, a_tail))

    sv = {}
    npg = gw // LANES

    def b_prep():
        a_row = -jnp.exp(alog_ref[...]) * LOG2E
        head_lanes = jnp.where(lax.broadcasted_iota(jnp.int32, (1, LANES), 1) < n_ssm_heads, 1.0, 0.0)
        xv = dt_b[...] + dtb_ref[...]
        dt = (jnp.maximum(xv, 0.0) + jnp.log1p(jnp.exp(-jnp.abs(xv)))) * head_lanes
        da = dt * a_row
        d1 = da.astype(BF16)
        r1 = da - d1.astype(F32)
        d2 = r1.astype(BF16)
        d3 = (r1 - d2.astype(F32)).astype(BF16)
        chunk_shift = CHUNK.bit_length() - 1
        br = lax.broadcasted_iota(jnp.int32, (t, t), 0)
        bc = lax.broadcasted_iota(jnp.int32, (t, t), 1)
        same_chunk = jnp.right_shift(br, chunk_shift) == jnp.right_shift(bc, chunk_shift)
        tril = jnp.where((br >= bc) & same_chunk, 1.0, 0.0).astype(BF16)
        cs = _dot(tril, jnp.concatenate([d1, d2, d3], axis=1))
        acum = cs[:, 0:LANES] + cs[:, LANES:2 * LANES] + cs[:, 2 * LANES:3 * LANES]

        def pack_hi_lo(v):
            hi = v.astype(BF16).astype(F32)
            return (hi + pltpu.roll(v - hi, n_ssm_heads, 1)).astype(BF16)

        sv["packed"] = jnp.concatenate([pack_hi_lo(dt), pack_hi_lo(acum)], axis=0)
        lowf_g = jnp.concatenate([lowf] * npg, axis=1)
        sv["lowf_g"], sv["highf_g"] = lowf_g, 1.0 - lowf_g
    b_front.append((300, b_prep))

    def b_expand(g):
        gs = slice(g * gw, (g + 1) * gw)
        e_g = e_ref[:, gs]
        ex = _dot(sv["packed"], e_g)
        xdt = xs_b[:, gs] * ex[0:t]
        ae = ex[t:2 * t]
        xlo_s[:, gs] = (xdt * sv["lowf_g"]).astype(BF16)
        xhi_s[:, gs] = (xdt * sv["highf_g"]).astype(BF16)
        eae_s[:, gs] = jnp.exp2(ae)
        ae_s[:, gs] = ae
        for c in range(nc):
            rows = slice(c * CHUNK, (c + 1) * CHUNK)
            ae_c = ae[rows]
            ae_l = ae_c[CHUNK - 1:CHUNK, :]
            xt_s[rows, gs] = (xdt[rows] * jnp.exp2(ae_l - ae_c)).astype(BF16)
            edl_s[c * SUBLANES:(c + 1) * SUBLANES, gs] = jnp.broadcast_to(jnp.exp2(ae_l), (SUBLANES, gw))

    def b_decay(g, c):
        rows = slice(c * CHUNK, (c + 1) * CHUNK)
        bb = bc_b[rows, g * D_STATE:(g + 1) * D_STATE].astype(BF16)
        cc = bc_b[rows, (SSM_GROUPS + g) * D_STATE:(SSM_GROUPS + g + 1) * D_STATE].astype(BF16)
        cb2 = _dot_nt(cc, jnp.concatenate([bb, bb], axis=0))
        for i in range(npg):
            acol = ae_s[rows, g * gw + i * LANES:g * gw + (i + 1) * LANES]
            arow = jnp.sum(jnp.where(diag, acol, 0.0), axis=0, keepdims=True)
            dec = jnp.exp2(jnp.where(causal, acol - arow, NEG))
            mp_s[(c * SSM_GROUPS + g) * npg + i] = (cb2 * dec).astype(BF16)

    def b_yin(g, c):
        rows = slice(c * CHUNK, (c + 1) * CHUNK)
        zblk = jnp.zeros((2 * CHUNK, LANES), BF16)
        for i in range(0, npg, 2):
            base = (c * SSM_GROUPS + g) * npg + i
            l0 = slice(g * gw + i * LANES, g * gw + (i + 1) * LANES)
            l1 = slice(g * gw + (i + 1) * LANES, g * gw + (i + 2) * LANES)
            x0 = jnp.concatenate([xlo_s[rows, l0], xhi_s[rows, l0]], axis=0)
            x1 = jnp.concatenate([xlo_s[rows, l1], xhi_s[rows, l1]], axis=0)
            rhs = jnp.concatenate([jnp.concatenate([x0, zblk], axis=1), jnp.concatenate([zblk, x1], axis=1)], axis=0)
            lhs = jnp.concatenate([mp_s[base], mp_s[base + 1]], axis=1)
            yin_s[rows, g * gw + i * LANES:g * gw + (i + 2) * LANES] = _dot(lhs, rhs)

    for g in range(SSM_GROUPS):
        b_front.append((400, functools.partial(b_expand, g)))
        for c in range(nc):
            b_front.append((150, functools.partial(b_decay, g, c)))
    for g in range(SSM_GROUPS):
        for c in range(nc):
            b_front.append((60, functools.partial(b_yin, g, c)))

    ppk = n_pairs // N_KV_HEADS

    kw = WINDOW + tq
    khw = bias_ref.shape[2] // 2
    assert khw == _key_width(tq) and khw & (khw - 1) == 0

    def slab(lo_b, hi_b, i, kv):
        pad = [jnp.zeros((khw - kw, LANES), BF16)] if khw > kw else []
        c0 = kv * LANES
        return jnp.concatenate([lo_b[i, 0:kw, c0:c0 + LANES]] + pad + [hi_b[i, 0:kw, c0:c0 + LANES]] + pad, axis=0)

    def b_scores(i, kv):
        rows = slice(i * tq, (i + 1) * tq)
        kbd = slab(klo_b, khi_b, i, kv)
        qs = jnp.concatenate([q_b[rows, (kv * ppk + pp) * LANES:(kv * ppk + pp + 1) * LANES] for pp in range(ppk)],
                             axis=0)
        sc = _dot_nt(qs, kbd)
        for pp in range(ppk):
            p = kv * ppk + pp
            table = p if has_past else jnp.where(s_idx == 0, n_pairs + p, p)
            sc_s[i * n_pairs + p] = sc[pp * tq:(pp + 1) * tq] + bias_ref[table]

    def b_softmax(i, cl, p):
        rows = slice(cl * CHUNK, (cl + 1) * CHUNK)
        es, rs = [], []
        for hh in range(2):
            sh = sc_s[i * n_pairs + p, rows, hh * khw:(hh + 1) * khw]
            sink = sinks_ref[layer, 2 * p + hh] * LOG2E
            m = jnp.maximum(jnp.max(sh, axis=-1, keepdims=True), sink)
            e = jnp.exp2(sh - m)
            den = jnp.sum(e, axis=-1, keepdims=True) + jnp.exp2(sink - m)
            es.append(e.astype(BF16))
            rs.append(1.0 / den)
        pe_s[i * n_pairs + p, rows, :] = jnp.concatenate(es, axis=1)
        rd_s[i * n_pairs + p, rows, :] = rs[0] * lowf + rs[1] * highf

    def b_pv(i, kv):
        rows = slice(i * tq, (i + 1) * tq)
        vbd = slab(vlo_b, vhi_b, i, kv)
        pv = _dot(jnp.concatenate([pe_s[i * n_pairs + kv * ppk + pp] for pp in range(ppk)], axis=0), vbd)
        for pp in range(ppk):
            p = kv * ppk + pp
            o = pv[pp * tq:(pp + 1) * tq] * rd_s[i * n_pairs + p]
            o_s[rows, p * LANES:(p + 1) * LANES] = o.astype(BF16)

    for i in range(ns):
        for kv in range(N_KV_HEADS):
            b_first.append((70 * ncs, functools.partial(b_scores, i, kv)))
    for i in range(ns):
        for cl in range(ncs):
            for p in range(n_pairs):
                b_front.append((110, functools.partial(b_softmax, i, cl, p)))
    for i in range(ns):
        for kv in range(N_KV_HEADS):
            b_front.append((70 * ncs, functools.partial(b_pv, i, kv)))

    def b_scan(c, g):
        rows = slice(c * CHUNK, (c + 1) * CHUNK)
        gs = slice(g * gw, (g + 1) * gw)
        bb = bc_b[rows, g * D_STATE:(g + 1) * D_STATE].astype(BF16)
        cc = bc_b[rows, (SSM_GROUPS + g) * D_STATE:(SSM_GROUPS + g + 1) * D_STATE].astype(BF16)
        i = c // ncs
        st = st_s[i, :, gs]
        y = yin_s[rows, gs] + _dot(cc, st.astype(BF16)) * eae_s[rows, gs] + xs_b[rows, gs] * dskip_ref[:, gs]
        y = y * z_b[rows, gs]
        y = y * _rms_scale(y) * ssmg_ref[:, gs]
        y_s[rows, gs] = y.astype(BF16)
        st_s[i, :, gs] = st * edl_s[c * SUBLANES:c * SUBLANES + 1, gs] + _dot_tn(bb, xt_s[rows, gs])
    for c in range(nc):
        for g in range(SSM_GROUPS):
            b_scans.append((160, functools.partial(b_scan, c, g)))

    def b_gate(i, name):
        gate_s[i] = _sigmoid(_dot(h_b[...], wall_ref[:, cols[name]:cols[name] + d_model]))
    b_gates = [(512, functools.partial(b_gate, 0, "gs")), (512, functools.partial(b_gate, 1, "ga"))]

    def wcol(name):
        return wall_ref[:, cols[name]:cols[name] + d_model]

    def b_branch(i, src_s, names):
        acc = _dot(src_s[:, 0:d_model], wcol(names[0]))
        for k in range(1, len(names)):
            acc = acc + _dot(src_s[:, k * d_model:(k + 1) * d_model], wcol(names[k]))
        part = gate_s[i] * acc
        mix_s[...] = part if i == 0 else mix_s[...] + part
    b_merge.append((1000, functools.partial(b_branch, 0, y_s, ["wbs%d" % k for k in range(d_inner // d_model)])))
    b_merge.append((500, functools.partial(b_branch, 1, o_s, ["wba"])))

    def b_out():
        out = _dot(mix_s[...].astype(BF16), wcol("wo"))
        for i in range(ns):
            xo_ref[i] = x_ref[i] + mod_ref[i, 2:3, :] * out[i * tq:(i + 1) * tq]
    b_merge.append((500, b_out))

    _interleave(a_mxu)
    _interleave(a_vpu, a_late + b_gates)
    _interleave(b_first + b_front + b_scans)
    _interleave(b_merge)

    if not has_past:
        for buf in (klo_s, khi_s, vlo_s, vhi_s):
            for i in range(ns):
                buf[i, 0:WINDOW, :] = buf[i, tq:tq + WINDOW, :]

    @pl.when(s_idx == n_steps - 1)
    def _():
        kr = kn_ref.shape[1]
        for i in range(ns):
            convn_ref[i] = cb_s[i * stride + tq:(i + 1) * stride, :]
            ssmn_ref[i] = st_s[i].T
            kn_ref[i] = knat_s[(i + 1) * tq - kr:(i + 1) * tq, :]
            vn_ref[i] = vnat_s[(i + 1) * tq - kr:(i + 1) * tq, :]


def _const_spec(shape, single_buffer=False):
    idx = (0,) * len(shape)
    if single_buffer:
        return pl.BlockSpec(shape, lambda *_: idx, pipeline_mode=pl.Buffered(1))
    return pl.BlockSpec(shape, lambda *_: idx)


def _layer_spec(stacked, layer):
    blk = (1,) + stacked.shape[1:]
    idx = (layer,) + (0,) * (stacked.ndim - 1)
    if math.prod(blk) > (1 << 20):
        return pl.BlockSpec(blk, lambda *_: idx, pipeline_mode=pl.Buffered(1))
    return pl.BlockSpec(blk, lambda *_: idx)


def _mixer_layer(x, mod, past, lw, pair_bias, e_mat, *, nc, ns):
    bsz, seq, d_model = x.shape
    t = nc * CHUNK
    tq = t // ns
    has_past = past is not None
    d_inner, n_heads = lw["d_inner"], lw["n_heads"]
    conv_ch = d_inner + 2 * SSM_GROUPS * D_STATE
    kvw = N_KV_HEADS * LANES
    kv_rows = seq if has_past else WINDOW
    assert nc % ns == 0 and bsz % ns == 0 and seq % tq == 0
    assert has_past and seq == tq or not has_past and tq >= WINDOW

    x_spec = pl.BlockSpec((ns, tq, d_model), lambda b, s: (b, s, 0))

    def per_seq(*shape):
        return pl.BlockSpec((ns,) + shape, lambda b, s: (b,) + (0,) * len(shape))

    in_specs = [x_spec, per_seq(6, d_model)]
    args = [x, mod]
    if has_past:
        in_specs += [per_seq(SUBLANES, conv_ch), per_seq(d_inner, D_STATE),
                     per_seq(WINDOW, N_KV_HEADS * HEAD_DIM), per_seq(WINDOW, N_KV_HEADS * HEAD_DIM)]
        args += list(past)
    consts = [lw["nmg"], lw["wall"], lw["convw"], lw["convb"], lw["dtb"], lw["alog"], lw["dskip"], lw["ssmg"],
              lw["qg"], lw["kg"]]
    in_specs += [_layer_spec(a, lw["layer"]) for a in consts]
    in_specs.append(pl.BlockSpec(memory_space=pltpu.SMEM))
    tail = [pair_bias, e_mat]
    in_specs += [_const_spec(a.shape, single_buffer=a.size > (1 << 18)) for a in tail]
    args += consts + [lw["sinks"]] + tail

    out_shape = (jax.ShapeDtypeStruct(x.shape, F32),
                 jax.ShapeDtypeStruct((bsz, SUBLANES, conv_ch), F32),
                 jax.ShapeDtypeStruct((bsz, d_inner, D_STATE), F32),
                 jax.ShapeDtypeStruct((bsz, kv_rows, N_KV_HEADS * HEAD_DIM), F32),
                 jax.ShapeDtypeStruct((bsz, kv_rows, N_KV_HEADS * HEAD_DIM), F32))
    out_specs = (x_spec, per_seq(SUBLANES, conv_ch), per_seq(d_inner, D_STATE),
                 per_seq(kv_rows, N_KV_HEADS * HEAD_DIM), per_seq(kv_rows, N_KV_HEADS * HEAD_DIM))
    scratch = [
        pltpu.VMEM((t, d_model), BF16),
        pltpu.VMEM((ns * (SUBLANES + tq), conv_ch), F32),
        pltpu.VMEM((t, d_inner), F32),
        pltpu.VMEM((t, d_inner), F32),
        pltpu.VMEM((t, 2 * SSM_GROUPS * D_STATE), F32),
        pltpu.VMEM((t, LANES), F32),
        pltpu.VMEM((t, d_model), BF16),
        pltpu.VMEM((ns, WINDOW + tq, kvw), BF16),
        pltpu.VMEM((ns, WINDOW + tq, kvw), BF16),
        pltpu.VMEM((ns, WINDOW + tq, kvw), BF16),
        pltpu.VMEM((ns, WINDOW + tq, kvw), BF16),
        pltpu.VMEM((t, d_inner), BF16),
        pltpu.VMEM((t, d_model), BF16),
        pltpu.VMEM((ns, D_STATE, d_inner), F32),
        pltpu.VMEM((t, d_inner), BF16),
        pltpu.VMEM((t, d_inner), BF16),
        pltpu.VMEM((t, d_inner), BF16),
        pltpu.VMEM((t, d_inner), F32),
        pltpu.VMEM((nc * SUBLANES, d_inner), F32),
        pltpu.VMEM((nc * d_inner // LANES, CHUNK, LANES), BF16),
        pltpu.VMEM((t, d_inner), F32),
        pltpu.VMEM((ns * n_heads // 2, tq, 2 * _key_width(tq)), F32),
        pltpu.VMEM((ns * n_heads // 2, tq, 2 * _key_width(tq)), BF16),
        pltpu.VMEM((ns * n_heads // 2, tq, LANES), F32),
        pltpu.VMEM((t, N_KV_HEADS * HEAD_DIM), F32),
        pltpu.VMEM((t, N_KV_HEADS * HEAD_DIM), F32),
        pltpu.VMEM((t, d_inner), F32),
        pltpu.VMEM((t, d_model), F32),
        pltpu.VMEM((2, t, d_model), F32),
    ]
    kern = functools.partial(_mixer_kernel, nc=nc, ns=ns, has_past=has_past, d_model=d_model, d_inner=d_inner,
                             n_heads=n_heads, n_ssm_heads=d_inner // SSM_HEAD_DIM, cols=lw["cols"],
                             layer=lw["layer"])
    return pl.pallas_call(
        kern, grid=(bsz // ns, seq // tq), in_specs=in_specs, out_specs=out_specs, out_shape=out_shape,
        scratch_shapes=scratch,
        compiler_params=pltpu.CompilerParams(dimension_semantics=("arbitrary", "arbitrary"),
                                             vmem_limit_bytes=VMEM_LIMIT),
        name="mixer_past" if has_past else "mixer_prompt",
    )(*args)


def _ffn_kernel(x_ref, mod_ref, g_ref, wgu_ref, wd_ref, o_ref):
    nb, tm, _ = x_ref.shape
    d_ff = wd_ref.shape[1]
    wg_ref, wu_ref, wd_ref = wgu_ref.at[0, :, 0:d_ff], wgu_ref.at[0, :, d_ff:2 * d_ff], wd_ref.at[0]
    hs = []
    for i in range(nb):
        x = x_ref[i]
        shift, scale = mod_ref[i, 3:4, :], mod_ref[i, 4:5, :]
        hs.append((x * _rms_scale(x) * g_ref[0] * (1.0 + scale) + shift).astype(BF16))
    h = hs[0] if nb == 1 else jnp.concatenate(hs, axis=0)
    gate = _dot(h, wg_ref[...])
    up = _dot(h, wu_ref[...])
    act = (gate * _sigmoid(gate) * up).astype(BF16)
    y = _dot(act, wd_ref[...])
    for i in range(nb):
        o_ref[i] = x_ref[i] + mod_ref[i, 5:6, :] * y[i * tm:(i + 1) * tm]


def _ffn_layer(x, mod, lw, *, nb, tm):
    bsz, seq, d_model = x.shape
    assert seq % tm == 0 and bsz % nb == 0
    xspec = pl.BlockSpec((nb, tm, d_model), lambda b, s: (b, s, 0))
    return pl.pallas_call(
        _ffn_kernel, grid=(bsz // nb, seq // tm),
        in_specs=[xspec, pl.BlockSpec((nb, 6, d_model), lambda b, s: (b, 0, 0)),
                  _layer_spec(lw["nfg"], lw["layer"]), _layer_spec(lw["wgu"], lw["layer"]),
                  _layer_spec(lw["wd"], lw["layer"])],
        out_specs=xspec, out_shape=jax.ShapeDtypeStruct(x.shape, F32),
        compiler_params=pltpu.CompilerParams(dimension_semantics=("arbitrary", "arbitrary"),
                                             vmem_limit_bytes=VMEM_LIMIT),
        name="ffn",
    )(x, mod, lw["nfg"], lw["wgu"], lw["wd"])


def _stacked_weights(p):
    d_model = p["w_out"].shape[2]
    d_inner = p["w_br_ssm"].shape[1]
    n_ssm_heads = p["dt_bias"].shape[1]
    nq = p["w_br_attn"].shape[1]
    nkv = N_KV_HEADS * HEAD_DIM
    conv_ch = d_inner + 2 * SSM_GROUPS * D_STATE
    wi = p["w_in"]
    bounds = [0, d_inner, d_inner + conv_ch, d_inner + conv_ch + n_ssm_heads]
    bounds += [bounds[-1] + nq, bounds[-1] + nq + nkv, bounds[-1] + nq + 2 * nkv]
    bounds += [bounds[-1] + d_model, bounds[-1] + 2 * d_model]
    z, xbc, dtc, q, k, v, gs, ga = [wi[:, :, a:b] for a, b in zip(bounds[:-1], bounds[1:])]
    assert nq == d_model and d_inner % d_model == 0
    pieces = [("z", z), ("xbc", xbc), ("dt", jnp.pad(dtc, ((0, 0), (0, 0), (0, LANES - n_ssm_heads)))),
              ("q", q), ("k", k), ("v", v), ("gs", gs), ("ga", ga),
              ("wba", p["w_br_attn"]), ("wo", p["w_out"])]
    pieces += [("wbs%d" % i, p["w_br_ssm"][:, i * d_model:(i + 1) * d_model]) for i in range(d_inner // d_model)]
    cols, off = {}, 0
    for name, w in pieces:
        cols[name] = off
        off += w.shape[2]
    return dict(cols=cols, wall=jnp.concatenate([w for _, w in pieces], axis=2).astype(BF16),
                wgu=p["w_gate_up"].astype(BF16), wd=p["w_down"].astype(BF16),
                d_inner=d_inner, n_heads=nq // HEAD_DIM)


def _stacked_vectors(p):
    n_ssm_heads = p["dt_bias"].shape[1]
    row = lambda a: a[:, None, :]
    rep = lambda a: row(jnp.repeat(a, SSM_HEAD_DIM, axis=1))
    padl = lambda a: row(jnp.pad(a, ((0, 0), (0, LANES - n_ssm_heads))))
    return dict(
        nmg=row(p["norm_mix_g"]), nfg=row(p["norm_ffn_g"]), convw=p["conv_w"], convb=row(p["conv_b"]),
        dtb=padl(p["dt_bias"]), alog=padl(p["a_log"]), dskip=rep(p["d_skip"]), ssmg=row(p["ssm_norm_g"]),
        qg=row(jnp.tile(p["q_norm_g"] * (HEAD_DIM ** -0.5 * LOG2E), (1, 2))), kg=row(jnp.tile(p["k_norm_g"], (1, 2))),
        sinks=p["sinks"],
    )


def _expand_matrix(n_ssm_heads):
    assert 3 * n_ssm_heads <= LANES
    r = jnp.arange(LANES)[:, None]
    src = (r < 2 * n_ssm_heads) | (r >= LANES - n_ssm_heads)
    c = jnp.arange(n_ssm_heads * SSM_HEAD_DIM)[None, :] // SSM_HEAD_DIM
    return (src & ((r % n_ssm_heads) == c)).astype(BF16)


def kernel(x_prompt, x_sample, cache_k, cache_v, state_conv, state_ssm, c_prompt, c_sample, rel_bias, ada_w, ada_b, norm_mix_g, norm_ffn_g, w_in, conv_w, conv_b, dt_bias, a_log, d_skip, ssm_norm_g, q_norm_g, k_norm_g, sinks, w_br_ssm, w_br_attn, w_out, w_gate_up, w_down):
    p = dict(norm_mix_g=norm_mix_g, norm_ffn_g=norm_ffn_g, w_in=w_in, conv_w=conv_w, conv_b=conv_b,
             dt_bias=dt_bias, a_log=a_log, d_skip=d_skip, ssm_norm_g=ssm_norm_g, q_norm_g=q_norm_g,
             k_norm_g=k_norm_g, sinks=sinks, w_br_ssm=w_br_ssm, w_br_attn=w_br_attn, w_out=w_out,
             w_gate_up=w_gate_up, w_down=w_down)
    depth = w_in.shape[0]
    bp, seq_p, d_model = x_prompt.shape
    bs, seq_s, _ = x_sample.shape
    n_ssm_heads, ssm_p, d_state = state_ssm.shape[2:]
    d_inner = n_ssm_heads * ssm_p
    nkv = N_KV_HEADS * HEAD_DIM

    mod = _ada_mod(jnp.concatenate([c_prompt, c_sample], axis=0), ada_w, ada_b)
    mod = mod.reshape(depth, bp + bs, 6, d_model)
    assert MIXER_CHUNKS * CHUNK >= WINDOW
    bias_p = jnp.concatenate([_pair_bias(rel_bias, MIXER_CHUNKS, True), _pair_bias(rel_bias, MIXER_CHUNKS, False)])
    bias_s = _pair_bias(rel_bias, 1, True)
    e_mat = _expand_matrix(n_ssm_heads)

    xp, xs = x_prompt, x_sample
    outs = [[] for _ in range(8)]
    stacked = dict(_stacked_weights(p), **_stacked_vectors(p))
    for l in range(depth):
        lw = dict(stacked, layer=l)
        mod_p, mod_s = mod[l, :bp], mod[l, bp:]
        past = (jnp.pad(state_conv[l], ((0, 0), (SUBLANES - (CONV_K - 1), 0), (0, 0))),
                state_ssm[l].reshape(bs, d_inner, d_state),
                cache_k[l].reshape(bs, WINDOW, nkv), cache_v[l].reshape(bs, WINDOW, nkv))
        xp, cvp, ssp, kp, vp = _mixer_layer(xp, mod_p, None, lw, bias_p, e_mat, nc=MIXER_CHUNKS, ns=1)
        ns_s = math.gcd(bs, MIXER_CHUNKS)
        xs, cvs, sss, ksn, vsn = _mixer_layer(xs, mod_s, past, lw, bias_s, e_mat, nc=ns_s, ns=ns_s)
        xp = _ffn_layer(xp, mod_p, lw, nb=1, tm=min(FFN_ROWS, seq_p))
        nb_s = math.gcd(bs, max(1, FFN_ROWS // seq_s))
        xs = _ffn_layer(xs, mod_s, lw, nb=nb_s, tm=seq_s)
        tail = SUBLANES - (CONV_K - 1)
        for lst, val in zip(outs, (
                cvp[:, tail:], cvs[:, tail:],
                ssp.reshape(bp, n_ssm_heads, ssm_p, d_state), sss.reshape(bs, n_ssm_heads, ssm_p, d_state),
                kp.reshape(bp, WINDOW, N_KV_HEADS, HEAD_DIM), ksn.reshape(bs, seq_s, N_KV_HEADS, HEAD_DIM),
                vp.reshape(bp, WINDOW, N_KV_HEADS, HEAD_DIM), vsn.reshape(bs, seq_s, N_KV_HEADS, HEAD_DIM))):
            lst.append(val)
    return (xp, xs) + tuple(jnp.stack(o) for o in outs)
```

```python
import functools
import math

import jax
import jax.numpy as jnp
from jax import lax
from jax.experimental import pallas as pl
from jax.experimental.pallas import tpu as pltpu

F32 = jnp.float32
BF16 = jnp.bfloat16

CHUNK = 64
EPS = 1e-6
SSM_GROUPS = 4
D_STATE = 128
SSM_HEAD_DIM = 64
CONV_K = 4
HEAD_DIM = 64
N_KV_HEADS = 4
WINDOW = 128
N_BUCKETS = 32
MAX_DISTANCE = 128

LANES = 128
SUBLANES = 8
KEYS = WINDOW + CHUNK
NEG = -1e30
LOG2E = math.log2(math.e)
VMEM_LIMIT = 56 * 1024 * 1024
FFN_ROWS = 1024
MIXER_CHUNKS = 2
COL_BLK = 512


def _dot(a, b):
    return jnp.dot(a, b, preferred_element_type=F32)


def _dot_nt(a, b):
    return lax.dot_general(a, b, (((1,), (1,)), ((), ())), preferred_element_type=F32)


def _dot_tn(a, b):
    return lax.dot_general(a, b, (((0,), (0,)), ((), ())), preferred_element_type=F32)


def _sigmoid(x):
    return 0.5 * jnp.tanh(0.5 * x) + 0.5


def _interleave(*streams):
    totals = [float(sum(w for w, _ in st)) or 1.0 for st in streams]
    pos, done = [0] * len(streams), [0.0] * len(streams)
    while True:
        live = [k for k in range(len(streams)) if pos[k] < len(streams[k])]
        if not live:
            return
        k = min(live, key=lambda k: done[k] / totals[k])
        w, fn = streams[k][pos[k]]
        fn()
        pos[k] += 1
        done[k] += w


def _rms_scale(x):
    return lax.rsqrt(jnp.mean(x * x, axis=-1, keepdims=True) + EPS)


def _ada_kernel(c_ref, w_ref, b_ref, o_ref):
    c = c_ref[...]
    s = (c * _sigmoid(c)).astype(BF16)
    o_ref[0] = _dot(s, w_ref[0].astype(BF16)) + b_ref[0]


def _ada_mod(c_all, ada_w, ada_b):
    depth, d, six_d = ada_w.shape
    n = c_all.shape[0]
    wblk = six_d // 2
    return pl.pallas_call(
        _ada_kernel,
        grid=(depth, six_d // wblk),
        in_specs=[
            pl.BlockSpec((n, d), lambda l, j: (0, 0)),
            pl.BlockSpec((1, d, wblk), lambda l, j: (l, 0, j)),
            pl.BlockSpec((1, 1, wblk), lambda l, j: (l, 0, j)),
        ],
        out_specs=pl.BlockSpec((1, n, wblk), lambda l, j: (l, 0, j)),
        out_shape=jax.ShapeDtypeStruct((depth, n, six_d), F32),
        name="ada_mod",
    )(c_all, ada_w, ada_b.reshape(depth, 1, six_d))


def _bias_kernel(bkt_ref, rb_ref, o_ref, *, n_heads):
    bkt = bkt_ref[...]
    khw = bkt.shape[1]
    for h in range(n_heads):
        def body(b, acc, h=h):
            return jnp.where(bkt == b, rb_ref[b, h] * LOG2E, acc)
        acc = lax.fori_loop(0, N_BUCKETS, body, jnp.full(bkt.shape, NEG, F32))
        o_ref[h // 2, :, (h % 2) * khw:(h % 2 + 1) * khw] = acc


def _t5_bucket(rel):
    n = -rel
    half = N_BUCKETS // 2
    max_exact = half // 2
    ret = jnp.where(n < 0, half, 0)
    n = jnp.abs(n)
    nf = jnp.maximum(n, 1).astype(F32)
    large = max_exact + (jnp.log(nf / max_exact) / math.log(MAX_DISTANCE / max_exact)
                         * (half - max_exact)).astype(jnp.int32)
    large = jnp.minimum(large, half - 1)
    return ret + jnp.where(n < max_exact, n, large)


def _key_width(tq):
    return pl.cdiv(WINDOW + tq, LANES) * LANES


def _pair_bias(rel_bias, ncs, window_valid):
    n_heads = rel_bias.shape[1]
    tq = ncs * CHUNK
    khw = _key_width(tq)
    q = jnp.arange(tq)[:, None]
    key = jnp.arange(khw)[None, :]
    band = key - (q // CHUNK) * CHUNK
    rel = band - WINDOW - q % CHUNK
    seen = (band >= 0) & (band < KEYS) & ((key >= WINDOW) | window_valid)
    bkt = jnp.where(seen, _t5_bucket(rel), -1).astype(jnp.int32)
    return pl.pallas_call(
        functools.partial(_bias_kernel, n_heads=n_heads),
        in_specs=[pl.BlockSpec((tq, khw), lambda: (0, 0)),
                  pl.BlockSpec(memory_space=pltpu.SMEM)],
        out_specs=pl.BlockSpec((n_heads // 2, tq, 2 * khw), lambda: (0, 0, 0)),
        out_shape=jax.ShapeDtypeStruct((n_heads // 2, tq, 2 * khw), F32),
        name="pair_bias",
    )(bkt, rel_bias)


def _mixer_kernel(*refs, nc, ns, has_past, d_model, d_inner, n_heads, n_ssm_heads, cols, layer):
    t = nc * CHUNK
    tq = t // ns
    ncs = nc // ns
    stride = SUBLANES + tq
    blk = COL_BLK
    conv_ch = d_inner + 2 * SSM_GROUPS * D_STATE
    gw = d_inner // SSM_GROUPS
    n_pairs = n_heads // 2
    kvw = N_KV_HEADS * LANES
    it = iter(refs)
    x_ref, mod_ref = next(it), next(it)
    if has_past:
        conv0_ref, ssm0_ref, k0_ref, v0_ref = next(it), next(it), next(it), next(it)
    (nmg_ref, wall_ref, convw_ref, convb_ref, dtb_ref, alog_ref, dskip_ref, ssmg_ref, qg_ref, kg_ref,
     sinks_ref, bias_ref, e_ref) = [next(it) for _ in range(13)]
    (nmg_ref, wall_ref, convw_ref, convb_ref, dtb_ref, alog_ref, dskip_ref, ssmg_ref, qg_ref, kg_ref) = [
        r.at[0] for r in (nmg_ref, wall_ref, convw_ref, convb_ref, dtb_ref, alog_ref, dskip_ref, ssmg_ref,
                          qg_ref, kg_ref)]
    xo_ref, convn_ref, ssmn_ref, kn_ref, vn_ref = [next(it) for _ in range(5)]
    (h_s, cb_s, z_s, xs_s, bc_s, dt_s, q_s, klo_s, khi_s, vlo_s, vhi_s, y_s, o_s, st_s,
     xlo_s, xhi_s, xt_s, eae_s, edl_s, mp_s, yin_s, sc_s, pe_s, rd_s, knat_s, vnat_s, ae_s, mix_s,
     gate_s) = list(it)

    s_idx = pl.program_id(1)
    n_steps = pl.num_programs(1)

    lane = lax.broadcasted_iota(jnp.int32, (CHUNK, LANES), 1)
    row = lax.broadcasted_iota(jnp.int32, (CHUNK, LANES), 0)
    causal = row >= (lane & (CHUNK - 1))
    diag = row == (lane & (CHUNK - 1))
    lowk = lax.broadcasted_iota(jnp.int32, (WINDOW, LANES), 1) < HEAD_DIM
    lowf = jnp.where(lax.broadcasted_iota(jnp.int32, (1, LANES), 1) < HEAD_DIM, 1.0, 0.0)
    highf = 1.0 - lowf

    @pl.when(s_idx == 0)
    def _():
        for i in range(ns):
            if has_past:
                cb_s[i * stride:i * stride + SUBLANES, :] = conv0_ref[i]
                st_s[i] = ssm0_ref[i].T
                for src_ref, lo_s, hi_s in ((k0_ref, klo_s, khi_s), (v0_ref, vlo_s, vhi_s)):
                    for j in range(N_KV_HEADS // 2):
                        nat = src_ref[i, :, j * LANES:(j + 1) * LANES]
                        rol = pltpu.roll(nat, HEAD_DIM, 1)
                        a, b = 2 * j * LANES, (2 * j + 1) * LANES
                        lo_s[i, 0:WINDOW, a:a + LANES] = jnp.where(lowk, nat, 0.0).astype(BF16)
                        hi_s[i, 0:WINDOW, a:a + LANES] = jnp.where(lowk, 0.0, rol).astype(BF16)
                        lo_s[i, 0:WINDOW, b:b + LANES] = jnp.where(lowk, rol, 0.0).astype(BF16)
                        hi_s[i, 0:WINDOW, b:b + LANES] = jnp.where(lowk, 0.0, nat).astype(BF16)
            else:
                cb_s[i * stride:i * stride + SUBLANES, :] = jnp.zeros((SUBLANES, conv_ch), F32)
                st_s[i] = jnp.zeros(st_s.shape[1:], F32)
                for buf in (klo_s, khi_s, vlo_s, vhi_s):
                    buf[i, 0:WINDOW, :] = jnp.zeros((WINDOW, kvw), BF16)

    a_mxu, a_late, a_vpu, b_first, b_front, b_scans, b_merge = [], [], [], [], [], [], []
    h_b, z_b, xs_b, bc_b, dt_b, q_b, klo_b, khi_b, vlo_b, vhi_b = (
        h_s, z_s, xs_s, bc_s, dt_s, q_s, klo_s, khi_s, vlo_s, vhi_s)

    def a_norm():
        for i in range(ns):
            x = x_ref[i]
            shift, scale = mod_ref[i, 0:1, :], mod_ref[i, 1:2, :]
            h_s[i * tq:(i + 1) * tq, :] = (x * _rms_scale(x) * nmg_ref[...] * (1.0 + scale) + shift).astype(BF16)
    a_mxu.append((150, a_norm))

    def a_xbc(j):
        res = _dot(h_s[...], wall_ref[:, cols["xbc"] + j:cols["xbc"] + j + blk])
        for i in range(ns):
            cb_s[i * stride + SUBLANES:(i + 1) * stride, j:j + blk] = res[i * tq:(i + 1) * tq]
    for j in range(0, conv_ch, blk):
        a_mxu.append((256, functools.partial(a_xbc, j)))

    def a_dt():
        dt_s[...] = _dot(h_s[...], wall_ref[:, cols["dt"]:cols["dt"] + LANES])
    a_mxu.append((30, a_dt))

    def head_rms(v):
        v2 = v * v
        sa = jnp.sum(v2 * lowf, axis=-1, keepdims=True)
        sb = jnp.sum(v2 * highf, axis=-1, keepdims=True)
        return lax.rsqrt((sa * lowf + sb * highf) * (1.0 / HEAD_DIM) + EPS)

    def a_q(p2):
        qv = _dot(h_s[...], wall_ref[:, cols["q"] + p2 * 2 * LANES:cols["q"] + (p2 + 1) * 2 * LANES])
        for i in range(2):
            v = qv[:, i * LANES:(i + 1) * LANES]
            q_s[:, (2 * p2 + i) * LANES:(2 * p2 + i + 1) * LANES] = (v * head_rms(v) * qg_ref[...]).astype(BF16)
    for p2 in range(n_pairs // 2):
        a_mxu.append((160, functools.partial(a_q, p2)))

    def a_kv():
        nkv = N_KV_HEADS * HEAD_DIM
        kd = _dot(h_s[...], wall_ref[:, cols["k"]:cols["k"] + nkv])
        vd = _dot(h_s[...], wall_ref[:, cols["v"]:cols["v"] + nkv])
        for j in range(N_KV_HEADS // 2):
            ls = slice(j * LANES, (j + 1) * LANES)
            kx = kd[:, ls]
            kn = kx * head_rms(kx) * kg_ref[...]
            vx = vd[:, ls]
            knat_s[:, ls] = kn
            vnat_s[:, ls] = vx
            for val, lo_s, hi_s in ((kn, klo_s, khi_s), (vx, vlo_s, vhi_s)):
                rol = pltpu.roll(val, HEAD_DIM, 1)
                a, b = 2 * j * LANES, (2 * j + 1) * LANES
                parts = ((lo_s, a, val * lowf), (hi_s, a, rol * highf), (lo_s, b, rol * lowf), (hi_s, b, val * highf))
                for dst_s, c0, piece in parts:
                    piece = piece.astype(BF16)
                    for i in range(ns):
                        dst_s[i, WINDOW:WINDOW + tq, c0:c0 + LANES] = piece[i * tq:(i + 1) * tq]
    a_mxu.append((300, a_kv))

    def a_z(j):
        zz = _dot(h_s[...], wall_ref[:, cols["z"] + j:cols["z"] + j + blk])
        z_s[:, j:j + blk] = zz * _sigmoid(zz)
    for j in range(0, d_inner, blk):
        a_late.append((256, functools.partial(a_z, j)))

    def a_conv(j):
        for i in range(ns):
            r0 = i * stride
            acc = cb_s[r0 + SUBLANES:r0 + stride, j:j + blk] * convw_ref[CONV_K - 1:CONV_K, j:j + blk]
            for k in range(CONV_K - 1):
                o = r0 + SUBLANES - (CONV_K - 1) + k
                acc = acc + cb_s[o:o + tq, j:j + blk] * convw_ref[k:k + 1, j:j + blk]
            acc = acc + convb_ref[:, j:j + blk]
            u = acc * _sigmoid(acc)
            if j < d_inner:
                xs_s[i * tq:(i + 1) * tq, j:j + blk] = u
            else:
                bc_s[i * tq:(i + 1) * tq, j - d_inner:j - d_inner + blk] = u
    for j in range(0, conv_ch, blk):
        a_vpu.append((300, functools.partial(a_conv, j)))

    def a_tail():
        for i in range(ns):
            cb_s[i * stride:i * stride + SUBLANES, :] = cb_s[i * stride + tq:(i + 1) * stride, :]
    if not has_past:
        a_vpu.append((10, a_tail))

    sv = {}
    npg = gw // LANES

    def b_prep():
        a_row = -jnp.exp(alog_ref[...]) * LOG2E
        head_lanes = jnp.where(lax.broadcasted_iota(jnp.int32, (1, LANES), 1) < n_ssm_heads, 1.0, 0.0)
        xv = dt_b[...] + dtb_ref[...]
        dt = (jnp.maximum(xv, 0.0) + jnp.log1p(jnp.exp(-jnp.abs(xv)))) * head_lanes
        da = dt * a_row
        d1 = da.astype(BF16)
        r1 = da - d1.astype(F32)
        d2 = r1.astype(BF16)
        d3 = (r1 - d2.astype(F32)).astype(BF16)
        chunk_shift = CHUNK.bit_length() - 1
        br = lax.broadcasted_iota(jnp.int32, (t, t), 0)
        bc = lax.broadcasted_iota(jnp.int32, (t, t), 1)
        same_chunk = jnp.right_shift(br, chunk_shift) == jnp.right_shift(bc, chunk_shift)
        tril = jnp.where((br >= bc) & same_chunk, 1.0, 0.0).astype(BF16)
        cs = _dot(tril, jnp.concatenate([d1, d2, d3], axis=1))
        acum = cs[:, 0:LANES] + cs[:, LANES:2 * LANES] + cs[:, 2 * LANES:3 * LANES]

        def pack_hi_lo(v):
            hi = v.astype(BF16).astype(F32)
            return (hi + pltpu.roll(v - hi, n_ssm_heads, 1)).astype(BF16)

        sv["packed"] = jnp.concatenate([pack_hi_lo(dt), pack_hi_lo(acum)], axis=0)
        lowf_g = jnp.concatenate([lowf] * npg, axis=1)
        sv["lowf_g"], sv["highf_g"] = lowf_g, 1.0 - lowf_g
    b_front.append((300, b_prep))

    def b_expand(g):
        gs = slice(g * gw, (g + 1) * gw)
        e_g = e_ref[:, gs]
        ex = _dot(sv["packed"], e_g)
        xdt = xs_b[:, gs] * ex[0:t]
        ae = ex[t:2 * t]
        xlo_s[:, gs] = (xdt * sv["lowf_g"]).astype(BF16)
        xhi_s[:, gs] = (xdt * sv["highf_g"]).astype(BF16)
        eae_s[:, gs] = jnp.exp2(ae)
        ae_s[:, gs] = ae
        for c in range(nc):
            rows = slice(c * CHUNK, (c + 1) * CHUNK)
            ae_c = ae[rows]
            ae_l = ae_c[CHUNK - 1:CHUNK, :]
            xt_s[rows, gs] = (xdt[rows] * jnp.exp2(ae_l - ae_c)).astype(BF16)
            edl_s[c * SUBLANES:(c + 1) * SUBLANES, gs] = jnp.broadcast_to(jnp.exp2(ae_l), (SUBLANES, gw))

    def b_decay(g, c):
        rows = slice(c * CHUNK, (c + 1) * CHUNK)
        bb = bc_b[rows, g * D_STATE:(g + 1) * D_STATE].astype(BF16)
        cc = bc_b[rows, (SSM_GROUPS + g) * D_STATE:(SSM_GROUPS + g + 1) * D_STATE].astype(BF16)
        cb2 = _dot_nt(cc, jnp.concatenate([bb, bb], axis=0))
        for i in range(npg):
            acol = ae_s[rows, g * gw + i * LANES:g * gw + (i + 1) * LANES]
            arow = jnp.sum(jnp.where(diag, acol, 0.0), axis=0, keepdims=True)
            dec = jnp.exp2(jnp.where(causal, acol - arow, NEG))
            mp_s[(c * SSM_GROUPS + g) * npg + i] = (cb2 * dec).astype(BF16)

    def b_yin(g, c):
        rows = slice(c * CHUNK, (c + 1) * CHUNK)
        zblk = jnp.zeros((2 * CHUNK, LANES), BF16)
        for i in range(0, npg, 2):
            base = (c * SSM_GROUPS + g) * npg + i
            l0 = slice(g * gw + i * LANES, g * gw + (i + 1) * LANES)
            l1 = slice(g * gw + (i + 1) * LANES, g * gw + (i + 2) * LANES)
            x0 = jnp.concatenate([xlo_s[rows, l0], xhi_s[rows, l0]], axis=0)
            x1 = jnp.concatenate([xlo_s[rows, l1], xhi_s[rows, l1]], axis=0)
            rhs = jnp.concatenate([jnp.concatenate([x0, zblk], axis=1), jnp.concatenate([zblk, x1], axis=1)], axis=0)
            lhs = jnp.concatenate([mp_s[base], mp_s[base + 1]], axis=1)
            yin_s[rows, g * gw + i * LANES:g * gw + (i + 2) * LANES] = _dot(lhs, rhs)

    for g in range(SSM_GROUPS):
        b_front.append((400, functools.partial(b_expand, g)))
        for c in range(nc):
            b_front.append((150, functools.partial(b_decay, g, c)))
    for g in range(SSM_GROUPS):
        for c in range(nc):
            b_front.append((60, functools.partial(b_yin, g, c)))

    ppk = n_pairs // N_KV_HEADS

    kw = WINDOW + tq
    khw = bias_ref.shape[2] // 2
    assert khw == _key_width(tq) and khw & (khw - 1) == 0

    def slab(lo_b, hi_b, i, kv):
        pad = [jnp.zeros((khw - kw, LANES), BF16)] if khw > kw else []
        c0 = kv * LANES
        return jnp.concatenate([lo_b[i, 0:kw, c0:c0 + LANES]] + pad + [hi_b[i, 0:kw, c0:c0 + LANES]] + pad, axis=0)

    def b_scores(i, kv):
        rows = slice(i * tq, (i + 1) * tq)
        kbd = slab(klo_b, khi_b, i, kv)
        qs = jnp.concatenate([q_b[rows, (kv * ppk + pp) * LANES:(kv * ppk + pp + 1) * LANES] for pp in range(ppk)],
                             axis=0)
        sc = _dot_nt(qs, kbd)
        for pp in range(ppk):
            p = kv * ppk + pp
            table = p if has_past else jnp.where(s_idx == 0, n_pairs + p, p)
            sc_s[i * n_pairs + p] = sc[pp * tq:(pp + 1) * tq] + bias_ref[table]

    def b_softmax(i, cl, p):
        rows = slice(cl * CHUNK, (cl + 1) * CHUNK)
        es, rs = [], []
        for hh in range(2):
            sh = sc_s[i * n_pairs + p, rows, hh * khw:(hh + 1) * khw]
            sink = sinks_ref[layer, 2 * p + hh] * LOG2E
            m = jnp.maximum(jnp.max(sh, axis=-1, keepdims=True), sink)
            e = jnp.exp2(sh - m)
            den = jnp.sum(e, axis=-1, keepdims=True) + jnp.exp2(sink - m)
            es.append(e.astype(BF16))
            rs.append(1.0 / den)
        pe_s[i * n_pairs + p, rows, :] = jnp.concatenate(es, axis=1)
        rd_s[i * n_pairs + p, rows, :] = rs[0] * lowf + rs[1] * highf

    def b_pv(i, kv):
        rows = slice(i * tq, (i + 1) * tq)
        vbd = slab(vlo_b, vhi_b, i, kv)
        pv = _dot(jnp.concatenate([pe_s[i * n_pairs + kv * ppk + pp] for pp in range(ppk)], axis=0), vbd)
        for pp in range(ppk):
            p = kv * ppk + pp
            o = pv[pp * tq:(pp + 1) * tq] * rd_s[i * n_pairs + p]
            o_s[rows, p * LANES:(p + 1) * LANES] = o.astype(BF16)

    for i in range(ns):
        for kv in range(N_KV_HEADS):
            b_first.append((70 * ncs, functools.partial(b_scores, i, kv)))
    for i in range(ns):
        for cl in range(ncs):
            for p in range(n_pairs):
                b_front.append((110, functools.partial(b_softmax, i, cl, p)))
    for i in range(ns):
        for kv in range(N_KV_HEADS):
            b_front.append((70 * ncs, functools.partial(b_pv, i, kv)))

    def b_scan(c, g):
        rows = slice(c * CHUNK, (c + 1) * CHUNK)
        gs = slice(g * gw, (g + 1) * gw)
        bb = bc_b[rows, g * D_STATE:(g + 1) * D_STATE].astype(BF16)
        cc = bc_b[rows, (SSM_GROUPS + g) * D_STATE:(SSM_GROUPS + g + 1) * D_STATE].astype(BF16)
        i = c // ncs
        st = st_s[i, :, gs]
        y = yin_s[rows, gs] + _dot(cc, st.astype(BF16)) * eae_s[rows, gs] + xs_b[rows, gs] * dskip_ref[:, gs]
        y = y * z_b[rows, gs]
        y = y * _rms_scale(y) * ssmg_ref[:, gs]
        y_s[rows, gs] = y.astype(BF16)
        st_s[i, :, gs] = st * edl_s[c * SUBLANES:c * SUBLANES + 1, gs] + _dot_tn(bb, xt_s[rows, gs])
    for c in range(nc):
        for g in range(SSM_GROUPS):
            b_scans.append((160, functools.partial(b_scan, c, g)))

    def b_gate(i, name):
        gate_s[i] = _sigmoid(_dot(h_b[...], wall_ref[:, cols[name]:cols[name] + d_model]))
    b_gates = [(512, functools.partial(b_gate, 0, "gs")), (512, functools.partial(b_gate, 1, "ga"))]

    def wcol(name):
        return wall_ref[:, cols[name]:cols[name] + d_model]

    def b_branch(i, src_s, names):
        acc = _dot(src_s[:, 0:d_model], wcol(names[0]))
        for k in range(1, len(names)):
            acc = acc + _dot(src_s[:, k * d_model:(k + 1) * d_model], wcol(names[k]))
        part = gate_s[i] * acc
        mix_s[...] = part if i == 0 else mix_s[...] + part
    b_merge.append((1000, functools.partial(b_branch, 0, y_s, ["wbs%d" % k for k in range(d_inner // d_model)])))
    b_merge.append((500, functools.partial(b_branch, 1, o_s, ["wba"])))

    def b_out():
        out = _dot(mix_s[...].astype(BF16), wcol("wo"))
        for i in range(ns):
            xo_ref[i] = x_ref[i] + mod_ref[i, 2:3, :] * out[i * tq:(i + 1) * tq]
    b_merge.append((500, b_out))

    _interleave(a_mxu)
    _interleave(a_vpu, a_late + b_gates)
    _interleave(b_first + b_front + b_scans)
    _interleave(b_merge)

    if not has_past:
        for buf in (klo_s, khi_s, vlo_s, vhi_s):
            for i in range(ns):
                buf[i, 0:WINDOW, :] = buf[i, tq:tq + WINDOW, :]

    @pl.when(s_idx == n_steps - 1)
    def _():
        kr = kn_ref.shape[1]
        for i in range(ns):
            convn_ref[i] = cb_s[i * stride + tq:(i + 1) * stride, :]
            ssmn_ref[i] = st_s[i].T
            kn_ref[i] = knat_s[(i + 1) * tq - kr:(i + 1) * tq, :]
            vn_ref[i] = vnat_s[(i + 1) * tq - kr:(i + 1) * tq, :]


def _const_spec(shape, single_buffer=False):
    idx = (0,) * len(shape)
    if single_buffer:
        return pl.BlockSpec(shape, lambda *_: idx, pipeline_mode=pl.Buffered(1))
    return pl.BlockSpec(shape, lambda *_: idx)


def _layer_spec(stacked, layer):
    blk = (1,) + stacked.shape[1:]
    idx = (layer,) + (0,) * (stacked.ndim - 1)
    if math.prod(blk) > (1 << 20):
        return pl.BlockSpec(blk, lambda *_: idx, pipeline_mode=pl.Buffered(1))
    return pl.BlockSpec(blk, lambda *_: idx)


def _mixer_layer(x, mod, past, lw, pair_bias, e_mat, *, nc, ns):
    bsz, seq, d_model = x.shape
    t = nc * CHUNK
    tq = t // ns
    has_past = past is not None
    d_inner, n_heads = lw["d_inner"], lw["n_heads"]
    conv_ch = d_inner + 2 * SSM_GROUPS * D_STATE
    kvw = N_KV_HEADS * LANES
    kv_rows = seq if has_past else WINDOW
    assert nc % ns == 0 and bsz % ns == 0 and seq % tq == 0
    assert has_past and seq == tq or not has_past and tq >= WINDOW

    x_spec = pl.BlockSpec((ns, tq, d_model), lambda b, s: (b, s, 0))

    def per_seq(*shape):
        return pl.BlockSpec((ns,) + shape, lambda b, s: (b,) + (0,) * len(shape))

    in_specs = [x_spec, per_seq(6, d_model)]
    args = [x, mod]
    if has_past:
        in_specs += [per_seq(SUBLANES, conv_ch), per_seq(d_inner, D_STATE),
                     per_seq(WINDOW, N_KV_HEADS * HEAD_DIM), per_seq(WINDOW, N_KV_HEADS * HEAD_DIM)]
        args += list(past)
    consts = [lw["nmg"], lw["wall"], lw["convw"], lw["convb"], lw["dtb"], lw["alog"], lw["dskip"], lw["ssmg"],
              lw["qg"], lw["kg"]]
    in_specs += [_layer_spec(a, lw["layer"]) for a in consts]
    in_specs.append(pl.BlockSpec(memory_space=pltpu.SMEM))
    tail = [pair_bias, e_mat]
    in_specs += [_const_spec(a.shape, single_buffer=a.size > (1 << 18)) for a in tail]
    args += consts + [lw["sinks"]] + tail

    out_shape = (jax.ShapeDtypeStruct(x.shape, F32),
                 jax.ShapeDtypeStruct((bsz, SUBLANES, conv_ch), F32),
                 jax.ShapeDtypeStruct((bsz, d_inner, D_STATE), F32),
                 jax.ShapeDtypeStruct((bsz, kv_rows, N_KV_HEADS * HEAD_DIM), F32),
                 jax.ShapeDtypeStruct((bsz, kv_rows, N_KV_HEADS * HEAD_DIM), F32))
    out_specs = (x_spec, per_seq(SUBLANES, conv_ch), per_seq(d_inner, D_STATE),
                 per_seq(kv_rows, N_KV_HEADS * HEAD_DIM), per_seq(kv_rows, N_KV_HEADS * HEAD_DIM))
    scratch = [
        pltpu.VMEM((t, d_model), BF16),
        pltpu.VMEM((ns * (SUBLANES + tq), conv_ch), F32),
        pltpu.VMEM((t, d_inner), F32),
        pltpu.VMEM((t, d_inner), F32),
        pltpu.VMEM((t, 2 * SSM_GROUPS * D_STATE), F32),
        pltpu.VMEM((t, LANES), F32),
        pltpu.VMEM((t, d_model), BF16),
        pltpu.VMEM((ns, WINDOW + tq, kvw), BF16),
        pltpu.VMEM((ns, WINDOW + tq, kvw), BF16),
        pltpu.VMEM((ns, WINDOW + tq, kvw), BF16),
        pltpu.VMEM((ns, WINDOW + tq, kvw), BF16),
        pltpu.VMEM((t, d_inner), BF16),
        pltpu.VMEM((t, d_model), BF16),
        pltpu.VMEM((ns, D_STATE, d_inner), F32),
        pltpu.VMEM((t, d_inner), BF16),
        pltpu.VMEM((t, d_inner), BF16),
        pltpu.VMEM((t, d_inner), BF16),
        pltpu.VMEM((t, d_inner), F32),
        pltpu.VMEM((nc * SUBLANES, d_inner), F32),
        pltpu.VMEM((nc * d_inner // LANES, CHUNK, LANES), BF16),
        pltpu.VMEM((t, d_inner), F32),
        pltpu.VMEM((ns * n_heads // 2, tq, 2 * _key_width(tq)), F32),
        pltpu.VMEM((ns * n_heads // 2, tq, 2 * _key_width(tq)), BF16),
        pltpu.VMEM((ns * n_heads // 2, tq, LANES), F32),
        pltpu.VMEM((t, N_KV_HEADS * HEAD_DIM), F32),
        pltpu.VMEM((t, N_KV_HEADS * HEAD_DIM), F32),
        pltpu.VMEM((t, d_inner), F32),
        pltpu.VMEM((t, d_model), F32),
        pltpu.VMEM((2, t, d_model), F32),
    ]
    kern = functools.partial(_mixer_kernel, nc=nc, ns=ns, has_past=has_past, d_model=d_model, d_inner=d_inner,
                             n_heads=n_heads, n_ssm_heads=d_inner // SSM_HEAD_DIM, cols=lw["cols"],
                             layer=lw["layer"])
    return pl.pallas_call(
        kern, grid=(bsz // ns, seq // tq), in_specs=in_specs, out_specs=out_specs, out_shape=out_shape,
        scratch_shapes=scratch,
        compiler_params=pltpu.CompilerParams(dimension_semantics=("arbitrary", "arbitrary"),
                                             vmem_limit_bytes=VMEM_LIMIT),
        name="mixer_past" if has_past else "mixer_prompt",
    )(*args)


def _ffn_kernel(x_ref, mod_ref, g_ref, wgu_ref, wd_ref, o_ref):
    nb, tm, _ = x_ref.shape
    d_ff = wd_ref.shape[1]
    wg_ref, wu_ref, wd_ref = wgu_ref.at[0, :, 0:d_ff], wgu_ref.at[0, :, d_ff:2 * d_ff], wd_ref.at[0]
    hs = []
    for i in range(nb):
        x = x_ref[i]
        shift, scale = mod_ref[i, 3:4, :], mod_ref[i, 4:5, :]
        hs.append((x * _rms_scale(x) * g_ref[0] * (1.0 + scale) + shift).astype(BF16))
    h = hs[0] if nb == 1 else jnp.concatenate(hs, axis=0)
    gate = _dot(h, wg_ref[...])
    up = _dot(h, wu_ref[...])
    act = (gate * _sigmoid(gate) * up).astype(BF16)
    y = _dot(act, wd_ref[...])
    for i in range(nb):
        o_ref[i] = x_ref[i] + mod_ref[i, 5:6, :] * y[i * tm:(i + 1) * tm]


def _ffn_layer(x, mod, lw, *, nb, tm):
    bsz, seq, d_model = x.shape
    assert seq % tm == 0 and bsz % nb == 0
    xspec = pl.BlockSpec((nb, tm, d_model), lambda b, s: (b, s, 0))
    return pl.pallas_call(
        _ffn_kernel, grid=(bsz // nb, seq // tm),
        in_specs=[xspec, pl.BlockSpec((nb, 6, d_model), lambda b, s: (b, 0, 0)),
                  _layer_spec(lw["nfg"], lw["layer"]), _layer_spec(lw["wgu"], lw["layer"]),
                  _layer_spec(lw["wd"], lw["layer"])],
        out_specs=xspec, out_shape=jax.ShapeDtypeStruct(x.shape, F32),
        compiler_params=pltpu.CompilerParams(dimension_semantics=("arbitrary", "arbitrary"),
                                             vmem_limit_bytes=VMEM_LIMIT),
        name="ffn",
    )(x, mod, lw["nfg"], lw["wgu"], lw["wd"])


def _stacked_weights(p):
    d_model = p["w_out"].shape[2]
    d_inner = p["w_br_ssm"].shape[1]
    n_ssm_heads = p["dt_bias"].shape[1]
    nq = p["w_br_attn"].shape[1]
    nkv = N_KV_HEADS * HEAD_DIM
    conv_ch = d_inner + 2 * SSM_GROUPS * D_STATE
    wi = p["w_in"]
    bounds = [0, d_inner, d_inner + conv_ch, d_inner + conv_ch + n_ssm_heads]
    bounds += [bounds[-1] + nq, bounds[-1] + nq + nkv, bounds[-1] + nq + 2 * nkv]
    bounds += [bounds[-1] + d_model, bounds[-1] + 2 * d_model]
    z, xbc, dtc, q, k, v, gs, ga = [wi[:, :, a:b] for a, b in zip(bounds[:-1], bounds[1:])]
    assert nq == d_model and d_inner % d_model == 0
    pieces = [("z", z), ("xbc", xbc), ("dt", jnp.pad(dtc, ((0, 0), (0, 0), (0, LANES - n_ssm_heads)))),
              ("q", q), ("k", k), ("v", v), ("gs", gs), ("ga", ga),
              ("wba", p["w_br_attn"]), ("wo", p["w_out"])]
    pieces += [("wbs%d" % i, p["w_br_ssm"][:, i * d_model:(i + 1) * d_model]) for i in range(d_inner // d_model)]
    cols, off = {}, 0
    for name, w in pieces:
        cols[name] = off
        off += w.shape[2]
    return dict(cols=cols, wall=jnp.concatenate([w for _, w in pieces], axis=2).astype(BF16),
                wgu=p["w_gate_up"].astype(BF16), wd=p["w_down"].astype(BF16),
                d_inner=d_inner, n_heads=nq // HEAD_DIM)


def _stacked_vectors(p):
    n_ssm_heads = p["dt_bias"].shape[1]
    row = lambda a: a[:, None, :]
    rep = lambda a: row(jnp.repeat(a, SSM_HEAD_DIM, axis=1))
    padl = lambda a: row(jnp.pad(a, ((0, 0), (0, LANES - n_ssm_heads))))
    return dict(
        nmg=row(p["norm_mix_g"]), nfg=row(p["norm_ffn_g"]), convw=p["conv_w"], convb=row(p["conv_b"]),
        dtb=padl(p["dt_bias"]), alog=padl(p["a_log"]), dskip=rep(p["d_skip"]), ssmg=row(p["ssm_norm_g"]),
        qg=row(jnp.tile(p["q_norm_g"] * (HEAD_DIM ** -0.5 * LOG2E), (1, 2))), kg=row(jnp.tile(p["k_norm_g"], (1, 2))),
        sinks=p["sinks"],
    )


def _expand_matrix(n_ssm_heads):
    assert 3 * n_ssm_heads <= LANES
    r = jnp.arange(LANES)[:, None]
    src = (r < 2 * n_ssm_heads) | (r >= LANES - n_ssm_heads)
    c = jnp.arange(n_ssm_heads * SSM_HEAD_DIM)[None, :] // SSM_HEAD_DIM
    return (src & ((r % n_ssm_heads) == c)).astype(BF16)


def kernel(x_prompt, x_sample, cache_k, cache_v, state_conv, state_ssm, c_prompt, c_sample, rel_bias, ada_w, ada_b, norm_mix_g, norm_ffn_g, w_in, conv_w, conv_b, dt_bias, a_log, d_skip, ssm_norm_g, q_norm_g, k_norm_g, sinks, w_br_ssm, w_br_attn, w_out, w_gate_up, w_down):
    p = dict(norm_mix_g=norm_mix_g, norm_ffn_g=norm_ffn_g, w_in=w_in, conv_w=conv_w, conv_b=conv_b,
             dt_bias=dt_bias, a_log=a_log, d_skip=d_skip, ssm_norm_g=ssm_norm_g, q_norm_g=q_norm_g,
             k_norm_g=k_norm_g, sinks=sinks, w_br_ssm=w_br_ssm, w_br_attn=w_br_attn, w_out=w_out,
             w_gate_up=w_gate_up, w_down=w_down)
    depth = w_in.shape[0]
    bp, seq_p, d_model = x_prompt.shape
    bs, seq_s, _ = x_sample.shape
    n_ssm_heads, ssm_p, d_state = state_ssm.shape[2:]
    d_inner = n_ssm_heads * ssm_p
    nkv = N_KV_HEADS * HEAD_DIM

    mod = _ada_mod(jnp.concatenate([c_prompt, c_sample], axis=0), ada_w, ada_b)
    mod = mod.reshape(depth, bp + bs, 6, d_model)
    assert MIXER_CHUNKS * CHUNK >= WINDOW
    bias_p = jnp.concatenate([_pair_bias(rel_bias, MIXER_CHUNKS, True), _pair_bias(rel_bias, MIXER_CHUNKS, False)])
    bias_s = _pair_bias(rel_bias, 1, True)
    e_mat = _expand_matrix(n_ssm_heads)

    xp, xs = x_prompt, x_sample
    outs = [[] for _ in range(8)]
    stacked = dict(_stacked_weights(p), **_stacked_vectors(p))
    for l in range(depth):
        lw = dict(stacked, layer=l)
        mod_p, mod_s = mod[l, :bp], mod[l, bp:]
        past = (jnp.pad(state_conv[l], ((0, 0), (SUBLANES - (CONV_K - 1), 0), (0, 0))),
                state_ssm[l].reshape(bs, d_inner, d_state),
                cache_k[l].reshape(bs, WINDOW, nkv), cache_v[l].reshape(bs, WINDOW, nkv))
        xp, cvp, ssp, kp, vp = _mixer_layer(xp, mod_p, None, lw, bias_p, e_mat, nc=MIXER_CHUNKS, ns=1)
        ns_s = math.gcd(bs, MIXER_CHUNKS)
        xs, cvs, sss, ksn, vsn = _mixer_layer(xs, mod_s, past, lw, bias_s, e_mat, nc=ns_s, ns=ns_s)
        xp = _ffn_layer(xp, mod_p, lw, nb=1, tm=min(FFN_ROWS, seq_p))
        nb_s = math.gcd(bs, max(1, FFN_ROWS // seq_s))
        xs = _ffn_layer(xs, mod_s, lw, nb=nb_s, tm=seq_s)
        tail = SUBLANES - (CONV_K - 1)
        for lst, val in zip(outs, (
                cvp[:, tail:], cvs[:, tail:],
                ssp.reshape(bp, n_ssm_heads, ssm_p, d_state), sss.reshape(bs, n_ssm_heads, ssm_p, d_state),
                kp.reshape(bp, WINDOW, N_KV_HEADS, HEAD_DIM), ksn.reshape(bs, seq_s, N_KV_HEADS, HEAD_DIM),
                vp.reshape(bp, WINDOW, N_KV_HEADS, HEAD_DIM), vsn.reshape(bs, seq_s, N_KV_HEADS, HEAD_DIM))):
            lst.append(val)
    return (xp, xs) + tuple(jnp.stack(o) for o in outs)
```

```python
import functools
import math

import jax
import jax.numpy as jnp
from jax import lax
from jax.experimental import pallas as pl
from jax.experimental.pallas import tpu as pltpu

F32 = jnp.float32
BF16 = jnp.bfloat16

CHUNK = 64
EPS = 1e-6
SSM_GROUPS = 4
D_STATE = 128
SSM_HEAD_DIM = 64
CONV_K = 4
HEAD_DIM = 64
N_KV_HEADS = 4
WINDOW = 128
N_BUCKETS = 32
MAX_DISTANCE = 128

LANES = 128
SUBLANES = 8
KEYS = WINDOW + CHUNK
NEG = -1e30
LOG2E = math.log2(math.e)
VMEM_LIMIT = 56 * 1024 * 1024
FFN_ROWS = 1024
MIXER_CHUNKS = 2
COL_BLK = 512


def _dot(a, b):
    return jnp.dot(a, b, preferred_element_type=F32)


def _dot_nt(a, b):
    return lax.dot_general(a, b, (((1,), (1,)), ((), ())), preferred_element_type=F32)


def _dot_tn(a, b):
    return lax.dot_general(a, b, (((0,), (0,)), ((), ())), preferred_element_type=F32)


def _sigmoid(x):
    return 0.5 * jnp.tanh(0.5 * x) + 0.5


def _silu(x):
    h = 0.5 * x
    return h * jnp.tanh(h) + h


def _interleave(*streams):
    totals = [float(sum(w for w, _ in st)) or 1.0 for st in streams]
    pos, done = [0] * len(streams), [0.0] * len(streams)
    while True:
        live = [k for k in range(len(streams)) if pos[k] < len(streams[k])]
        if not live:
            return
        k = min(live, key=lambda k: done[k] / totals[k])
        w, fn = streams[k][pos[k]]
        fn()
        pos[k] += 1
        done[k] += w


def _rms_scale(x):
    return lax.rsqrt(jnp.mean(x * x, axis=-1, keepdims=True) + EPS)


def _ada_kernel(c_ref, w_ref, b_ref, o_ref):
    c = c_ref[...]
    s = _silu(c).astype(BF16)
    o_ref[0] = _dot(s, w_ref[0].astype(BF16)) + b_ref[0]


def _ada_mod(c_all, ada_w, ada_b):
    depth, d, six_d = ada_w.shape
    n = c_all.shape[0]
    wblk = six_d // 2
    return pl.pallas_call(
        _ada_kernel,
        grid=(depth, six_d // wblk),
        in_specs=[
            pl.BlockSpec((n, d), lambda l, j: (0, 0)),
            pl.BlockSpec((1, d, wblk), lambda l, j: (l, 0, j)),
            pl.BlockSpec((1, 1, wblk), lambda l, j: (l, 0, j)),
        ],
        out_specs=pl.BlockSpec((1, n, wblk), lambda l, j: (l, 0, j)),
        out_shape=jax.ShapeDtypeStruct((depth, n, six_d), F32),
        name="ada_mod",
    )(c_all, ada_w, ada_b.reshape(depth, 1, six_d))


def _bias_kernel(bkt_ref, rb_ref, o_ref, *, n_heads):
    bkt = bkt_ref[...]
    khw = bkt.shape[1]
    for h in range(n_heads):
        def body(b, acc, h=h):
            return jnp.where(bkt == b, rb_ref[b, h] * LOG2E, acc)
        acc = lax.fori_loop(0, N_BUCKETS, body, jnp.full(bkt.shape, NEG, F32))
        o_ref[h // 2, :, (h % 2) * khw:(h % 2 + 1) * khw] = acc


def _t5_bucket(rel):
    n = -rel
    half = N_BUCKETS // 2
    max_exact = half // 2
    ret = jnp.where(n < 0, half, 0)
    n = jnp.abs(n)
    nf = jnp.maximum(n, 1).astype(F32)
    large = max_exact + (jnp.log(nf / max_exact) / math.log(MAX_DISTANCE / max_exact)
                         * (half - max_exact)).astype(jnp.int32)
    large = jnp.minimum(large, half - 1)
    return ret + jnp.where(n < max_exact, n, large)


def _key_width(tq):
    return pl.cdiv(WINDOW + tq, LANES) * LANES


def _pair_bias(rel_bias, ncs, window_valid):
    n_heads = rel_bias.shape[1]
    tq = ncs * CHUNK
    khw = _key_width(tq)
    q = jnp.arange(tq)[:, None]
    key = jnp.arange(khw)[None, :]
    band = key - (q // CHUNK) * CHUNK
    rel = band - WINDOW - q % CHUNK
    seen = (band >= 0) & (band < KEYS) & ((key >= WINDOW) | window_valid)
    bkt = jnp.where(seen, _t5_bucket(rel), -1).astype(jnp.int32)
    return pl.pallas_call(
        functools.partial(_bias_kernel, n_heads=n_heads),
        in_specs=[pl.BlockSpec((tq, khw), lambda: (0, 0)),
                  pl.BlockSpec(memory_space=pltpu.SMEM)],
        out_specs=pl.BlockSpec((n_heads // 2, tq, 2 * khw), lambda: (0, 0, 0)),
        out_shape=jax.ShapeDtypeStruct((n_heads // 2, tq, 2 * khw), F32),
        name="pair_bias",
    )(bkt, rel_bias)


def _mixer_kernel(*refs, nc, ns, has_past, d_model, d_inner, n_heads, n_ssm_heads, cols, layer):
    t = nc * CHUNK
    tq = t // ns
    ncs = nc // ns
    stride = SUBLANES + tq
    blk = COL_BLK
    conv_ch = d_inner + 2 * SSM_GROUPS * D_STATE
    gw = d_inner // SSM_GROUPS
    n_pairs = n_heads // 2
    kvw = N_KV_HEADS * LANES
    it = iter(refs)
    x_ref, mod_ref = next(it), next(it)
    if has_past:
        conv0_ref, ssm0_ref, k0_ref, v0_ref = next(it), next(it), next(it), next(it)
    (nmg_ref, wall_ref, convw_ref, convb_ref, dtb_ref, alog_ref, dskip_ref, ssmg_ref, qg_ref, kg_ref,
     sinks_ref, bias_ref, e_ref) = [next(it) for _ in range(13)]
    (nmg_ref, wall_ref, convw_ref, convb_ref, dtb_ref, alog_ref, dskip_ref, ssmg_ref, qg_ref, kg_ref) = [
        r.at[0] for r in (nmg_ref, wall_ref, convw_ref, convb_ref, dtb_ref, alog_ref, dskip_ref, ssmg_ref,
                          qg_ref, kg_ref)]
    xo_ref, convn_ref, ssmn_ref, kn_ref, vn_ref = [next(it) for _ in range(5)]
    (h_s, cb_s, z_s, xs_s, bc_s, dt_s, q_s, klo_s, khi_s, vlo_s, vhi_s, y_s, o_s, st_s,
     xlo_s, xhi_s, xt_s, eae_s, edl_s, mp_s, yin_s, sc_s, pe_s, rd_s, knat_s, vnat_s, ae_s, mix_s,
     gate_s) = list(it)

    s_idx = pl.program_id(1)
    n_steps = pl.num_programs(1)

    lane = lax.broadcasted_iota(jnp.int32, (CHUNK, LANES), 1)
    row = lax.broadcasted_iota(jnp.int32, (CHUNK, LANES), 0)
    causal = row >= (lane & (CHUNK - 1))
    diag = row == (lane & (CHUNK - 1))
    lowk = lax.broadcasted_iota(jnp.int32, (WINDOW, LANES), 1) < HEAD_DIM
    lowf = jnp.where(lax.broadcasted_iota(jnp.int32, (1, LANES), 1) < HEAD_DIM, 1.0, 0.0)
    highf = 1.0 - lowf

    @pl.when(s_idx == 0)
    def _():
        for i in range(ns):
            if has_past:
                cb_s[i * stride:i * stride + SUBLANES, :] = conv0_ref[i]
                st_s[i] = ssm0_ref[i].T
                for src_ref, lo_s, hi_s in ((k0_ref, klo_s, khi_s), (v0_ref, vlo_s, vhi_s)):
                    for j in range(N_KV_HEADS // 2):
                        nat = src_ref[i, :, j * LANES:(j + 1) * LANES]
                        rol = pltpu.roll(nat, HEAD_DIM, 1)
                        a, b = 2 * j * LANES, (2 * j + 1) * LANES
                        lo_s[i, 0:WINDOW, a:a + LANES] = jnp.where(lowk, nat, 0.0).astype(BF16)
                        hi_s[i, 0:WINDOW, a:a + LANES] = jnp.where(lowk, 0.0, rol).astype(BF16)
                        lo_s[i, 0:WINDOW, b:b + LANES] = jnp.where(lowk, rol, 0.0).astype(BF16)
                        hi_s[i, 0:WINDOW, b:b + LANES] = jnp.where(lowk, 0.0, nat).astype(BF16)
            else:
                cb_s[i * stride:i * stride + SUBLANES, :] = jnp.zeros((SUBLANES, conv_ch), F32)
                st_s[i] = jnp.zeros(st_s.shape[1:], F32)
                for buf in (klo_s, khi_s, vlo_s, vhi_s):
                    buf[i, 0:WINDOW, :] = jnp.zeros((WINDOW, kvw), BF16)

    a_mxu, a_late, a_vpu, b_first, b_front, b_scans, b_merge = [], [], [], [], [], [], []
    h_b, z_b, xs_b, bc_b, dt_b, q_b, klo_b, khi_b, vlo_b, vhi_b = (
        h_s, z_s, xs_s, bc_s, dt_s, q_s, klo_s, khi_s, vlo_s, vhi_s)

    def a_norm():
        for i in range(ns):
            x = x_ref[i]
            shift, scale = mod_ref[i, 0:1, :], mod_ref[i, 1:2, :]
            h_s[i * tq:(i + 1) * tq, :] = (x * _rms_scale(x) * nmg_ref[...] * (1.0 + scale) + shift).astype(BF16)
    a_mxu.append((150, a_norm))

    def a_xbc(j):
        res = _dot(h_s[...], wall_ref[:, cols["xbc"] + j:cols["xbc"] + j + blk])
        for i in range(ns):
            cb_s[i * stride + SUBLANES:(i + 1) * stride, j:j + blk] = res[i * tq:(i + 1) * tq]
    for j in range(0, conv_ch, blk):
        a_mxu.append((256, functools.partial(a_xbc, j)))

    def a_dt():
        dt_s[...] = _dot(h_s[...], wall_ref[:, cols["dt"]:cols["dt"] + LANES])
    a_mxu.append((30, a_dt))

    def head_rms(v):
        v2 = v * v
        sa = jnp.sum(v2 * lowf, axis=-1, keepdims=True)
        sb = jnp.sum(v2 * highf, axis=-1, keepdims=True)
        return lax.rsqrt((sa * lowf + sb * highf) * (1.0 / HEAD_DIM) + EPS)

    def a_q(p2):
        qv = _dot(h_s[...], wall_ref[:, cols["q"] + p2 * 2 * LANES:cols["q"] + (p2 + 1) * 2 * LANES])
        for i in range(2):
            v = qv[:, i * LANES:(i + 1) * LANES]
            q_s[:, (2 * p2 + i) * LANES:(2 * p2 + i + 1) * LANES] = (v * head_rms(v) * qg_ref[...]).astype(BF16)
    for p2 in range(n_pairs // 2):
        a_mxu.append((160, functools.partial(a_q, p2)))

    def a_kv():
        nkv = N_KV_HEADS * HEAD_DIM
        kd = _dot(h_s[...], wall_ref[:, cols["k"]:cols["k"] + nkv])
        vd = _dot(h_s[...], wall_ref[:, cols["v"]:cols["v"] + nkv])
        for j in range(N_KV_HEADS // 2):
            ls = slice(j * LANES, (j + 1) * LANES)
            kx = kd[:, ls]
            kn = kx * head_rms(kx) * kg_ref[...]
            vx = vd[:, ls]
            knat_s[:, ls] = kn
            vnat_s[:, ls] = vx
            for val, lo_s, hi_s in ((kn, klo_s, khi_s), (vx, vlo_s, vhi_s)):
                rol = pltpu.roll(val, HEAD_DIM, 1)
                a, b = 2 * j * LANES, (2 * j + 1) * LANES
                parts = ((lo_s, a, val * lowf), (hi_s, a, rol * highf), (lo_s, b, rol * lowf), (hi_s, b, val * highf))
                for dst_s, c0, piece in parts:
                    piece = piece.astype(BF16)
                    for i in range(ns):
                        dst_s[i, WINDOW:WINDOW + tq, c0:c0 + LANES] = piece[i * tq:(i + 1) * tq]
    a_mxu.append((300, a_kv))

    def a_z(j):
        zz = _dot(h_s[...], wall_ref[:, cols["z"] + j:cols["z"] + j + blk])
        z_s[:, j:j + blk] = _silu(zz)
    for j in range(0, d_inner, blk):
        a_late.append((256, functools.partial(a_z, j)))

    def a_conv(j):
        for i in range(ns):
            r0 = i * stride
            acc = cb_s[r0 + SUBLANES:r0 + stride, j:j + blk] * convw_ref[CONV_K - 1:CONV_K, j:j + blk]
            for k in range(CONV_K - 1):
                o = r0 + SUBLANES - (CONV_K - 1) + k
                acc = acc + cb_s[o:o + tq, j:j + blk] * convw_ref[k:k + 1, j:j + blk]
            acc = acc + convb_ref[:, j:j + blk]
            u = _silu(acc)
            if j < d_inner:
                xs_s[i * tq:(i + 1) * tq, j:j + blk] = u
            else:
                bc_s[i * tq:(i + 1) * tq, j - d_inner:j - d_inner + blk] = u
    for j in range(0, conv_ch, blk):
        a_vpu.append((300, functools.partial(a_conv, j)))

    def a_tail():
        for i in range(ns):
            cb_s[i * stride:i * stride + SUBLANES, :] = cb_s[i * stride + tq:(i + 1) * stride, :]
    if not has_past:
        a_vpu.append((10, a_tail))

    sv = {}
    npg = gw // LANES

    def b_prep():
        a_row = -jnp.exp(alog_ref[...]) * LOG2E
        head_lanes = jnp.where(lax.broadcasted_iota(jnp.int32, (1, LANES), 1) < n_ssm_heads, 1.0, 0.0)
        xv = dt_b[...] + dtb_ref[...]
        dt = (jnp.maximum(xv, 0.0) + jnp.log1p(jnp.exp(-jnp.abs(xv)))) * head_lanes
        da = dt * a_row
        d1 = da.astype(BF16)
        r1 = da - d1.astype(F32)
        d2 = r1.astype(BF16)
        d3 = (r1 - d2.astype(F32)).astype(BF16)
        chunk_shift = CHUNK.bit_length() - 1
        br = lax.broadcasted_iota(jnp.int32, (t, t), 0)
        bc = lax.broadcasted_iota(jnp.int32, (t, t), 1)
        same_chunk = jnp.right_shift(br, chunk_shift) == jnp.right_shift(bc, chunk_shift)
        tril = jnp.where((br >= bc) & same_chunk, 1.0, 0.0).astype(BF16)
        cs = _dot(tril, jnp.concatenate([d1, d2, d3], axis=1))
        acum = cs[:, 0:LANES] + cs[:, LANES:2 * LANES] + cs[:, 2 * LANES:3 * LANES]

        def pack_hi_lo(v):
            hi = v.astype(BF16).astype(F32)
            return (hi + pltpu.roll(v - hi, n_ssm_heads, 1)).astype(BF16)

        sv["packed"] = jnp.concatenate([pack_hi_lo(dt), pack_hi_lo(acum)], axis=0)
        lowf_g = jnp.concatenate([lowf] * npg, axis=1)
        sv["lowf_g"], sv["highf_g"] = lowf_g, 1.0 - lowf_g
    b_front.append((300, b_prep))

    def b_expand(g):
        gs = slice(g * gw, (g + 1) * gw)
        e_g = e_ref[:, gs]
        ex = _dot(sv["packed"], e_g)
        xdt = xs_b[:, gs] * ex[0:t]
        ae = ex[t:2 * t]
        xlo_s[:, gs] = (xdt * sv["lowf_g"]).astype(BF16)
        xhi_s[:, gs] = (xdt * sv["highf_g"]).astype(BF16)
        eae_s[:, gs] = jnp.exp2(ae)
        ae_s[:, gs] = ae
        for c in range(nc):
            rows = slice(c * CHUNK, (c + 1) * CHUNK)
            ae_c = ae[rows]
            ae_l = ae_c[CHUNK - 1:CHUNK, :]
            xt_s[rows, gs] = (xdt[rows] * jnp.exp2(ae_l - ae_c)).astype(BF16)
            edl_s[c * SUBLANES:(c + 1) * SUBLANES, gs] = jnp.broadcast_to(jnp.exp2(ae_l), (SUBLANES, gw))

    def b_decay(g, c):
        rows = slice(c * CHUNK, (c + 1) * CHUNK)
        bb = bc_b[rows, g * D_STATE:(g + 1) * D_STATE].astype(BF16)
        cc = bc_b[rows, (SSM_GROUPS + g) * D_STATE:(SSM_GROUPS + g + 1) * D_STATE].astype(BF16)
        cb2 = _dot_nt(cc, jnp.concatenate([bb, bb], axis=0))
        for i in range(npg):
            acol = ae_s[rows, g * gw + i * LANES:g * gw + (i + 1) * LANES]
            arow = jnp.sum(jnp.where(diag, acol, 0.0), axis=0, keepdims=True)
            dec = jnp.exp2(jnp.where(causal, acol - arow, NEG))
            mp_s[(c * SSM_GROUPS + g) * npg + i] = (cb2 * dec).astype(BF16)

    def b_yin(g, c):
        rows = slice(c * CHUNK, (c + 1) * CHUNK)
        zblk = jnp.zeros((2 * CHUNK, LANES), BF16)
        for i in range(0, npg, 2):
            base = (c * SSM_GROUPS + g) * npg + i
            l0 = slice(g * gw + i * LANES, g * gw + (i + 1) * LANES)
            l1 = slice(g * gw + (i + 1) * LANES, g * gw + (i + 2) * LANES)
            x0 = jnp.concatenate([xlo_s[rows, l0], xhi_s[rows, l0]], axis=0)
            x1 = jnp.concatenate([xlo_s[rows, l1], xhi_s[rows, l1]], axis=0)
            rhs = jnp.concatenate([jnp.concatenate([x0, zblk], axis=1), jnp.concatenate([zblk, x1], axis=1)], axis=0)
            lhs = jnp.concatenate([mp_s[base], mp_s[base + 1]], axis=1)
            yin_s[rows, g * gw + i * LANES:g * gw + (i + 2) * LANES] = _dot(lhs, rhs)

    for g in range(SSM_GROUPS):
        b_front.append((400, functools.partial(b_expand, g)))
        for c in range(nc):
            b_front.append((150, functools.partial(b_decay, g, c)))
    for g in range(SSM_GROUPS):
        for c in range(nc):
            b_front.append((60, functools.partial(b_yin, g, c)))

    ppk = n_pairs // N_KV_HEADS

    kw = WINDOW + tq
    khw = bias_ref.shape[2] // 2
    assert khw == _key_width(tq) and khw & (khw - 1) == 0

    def slab(lo_b, hi_b, i, kv):
        pad = [jnp.zeros((khw - kw, LANES), BF16)] if khw > kw else []
        c0 = kv * LANES
        return jnp.concatenate([lo_b[i, 0:kw, c0:c0 + LANES]] + pad + [hi_b[i, 0:kw, c0:c0 + LANES]] + pad, axis=0)

    def b_scores(i, kv):
        rows = slice(i * tq, (i + 1) * tq)
        kbd = slab(klo_b, khi_b, i, kv)
        qs = jnp.concatenate([q_b[rows, (kv * ppk + pp) * LANES:(kv * ppk + pp + 1) * LANES] for pp in range(ppk)],
                             axis=0)
        sc = _dot_nt(qs, kbd)
        for pp in range(ppk):
            p = kv * ppk + pp
            table = p if has_past else jnp.where(s_idx == 0, n_pairs + p, p)
            sc_s[i * n_pairs + p] = sc[pp * tq:(pp + 1) * tq] + bias_ref[table]

    def b_softmax(i, cl, p):
        rows = slice(cl * CHUNK, (cl + 1) * CHUNK)
        es, rs = [], []
        for hh in range(2):
            sh = sc_s[i * n_pairs + p, rows, hh * khw:(hh + 1) * khw]
            sink = sinks_ref[layer, 2 * p + hh] * LOG2E
            m = jnp.maximum(jnp.max(sh, axis=-1, keepdims=True), sink)
            e = jnp.exp2(sh - m)
            den = jnp.sum(e, axis=-1, keepdims=True) + jnp.exp2(sink - m)
            es.append(e.astype(BF16))
            rs.append(den)
        pe_s[i * n_pairs + p, rows, :] = jnp.concatenate(es, axis=1)
        rd_s[i * n_pairs + p, rows, :] = 1.0 / (rs[0] * lowf + rs[1] * highf)

    def b_pv(i, kv):
        rows = slice(i * tq, (i + 1) * tq)
        vbd = slab(vlo_b, vhi_b, i, kv)
        pv = _dot(jnp.concatenate([pe_s[i * n_pairs + kv * ppk + pp] for pp in range(ppk)], axis=0), vbd)
        for pp in range(ppk):
            p = kv * ppk + pp
            o = pv[pp * tq:(pp + 1) * tq] * rd_s[i * n_pairs + p]
            o_s[rows, p * LANES:(p + 1) * LANES] = o.astype(BF16)

    for i in range(ns):
        for kv in range(N_KV_HEADS):
            b_first.append((70 * ncs, functools.partial(b_scores, i, kv)))
    for i in range(ns):
        for cl in range(ncs):
            for p in range(n_pairs):
                b_front.append((110, functools.partial(b_softmax, i, cl, p)))
    for i in range(ns):
        for kv in range(N_KV_HEADS):
            b_front.append((70 * ncs, functools.partial(b_pv, i, kv)))

    def b_scan(c, g):
        rows = slice(c * CHUNK, (c + 1) * CHUNK)
        gs = slice(g * gw, (g + 1) * gw)
        bb = bc_b[rows, g * D_STATE:(g + 1) * D_STATE].astype(BF16)
        cc = bc_b[rows, (SSM_GROUPS + g) * D_STATE:(SSM_GROUPS + g + 1) * D_STATE].astype(BF16)
        i = c // ncs
        st = st_s[i, :, gs]
        y = yin_s[rows, gs] + _dot(cc, st.astype(BF16)) * eae_s[rows, gs] + xs_b[rows, gs] * dskip_ref[:, gs]
        y = y * z_b[rows, gs]
        y = y * _rms_scale(y) * ssmg_ref[:, gs]
        y_s[rows, gs] = y.astype(BF16)
        st_s[i, :, gs] = st * edl_s[c * SUBLANES:c * SUBLANES + 1, gs] + _dot_tn(bb, xt_s[rows, gs])
    for c in range(nc):
        for g in range(SSM_GROUPS):
            b_scans.append((160, functools.partial(b_scan, c, g)))

    def b_gate(i, name):
        gate_s[i] = _sigmoid(_dot(h_b[...], wall_ref[:, cols[name]:cols[name] + d_model]))
    b_gates = [(512, functools.partial(b_gate, 0, "gs")), (512, functools.partial(b_gate, 1, "ga"))]

    def wcol(name):
        return wall_ref[:, cols[name]:cols[name] + d_model]

    def b_branch(i, src_s, names):
        acc = _dot(src_s[:, 0:d_model], wcol(names[0]))
        for k in range(1, len(names)):
            acc = acc + _dot(src_s[:, k * d_model:(k + 1) * d_model], wcol(names[k]))
        part = gate_s[i] * acc
        mix_s[...] = part if i == 0 else mix_s[...] + part
    b_merge.append((1000, functools.partial(b_branch, 0, y_s, ["wbs%d" % k for k in range(d_inner // d_model)])))
    b_merge.append((500, functools.partial(b_branch, 1, o_s, ["wba"])))

    def b_out():
        out = _dot(mix_s[...].astype(BF16), wcol("wo"))
        for i in range(ns):
            xo_ref[i] = x_ref[i] + mod_ref[i, 2:3, :] * out[i * tq:(i + 1) * tq]
    b_merge.append((500, b_out))

    _interleave(a_mxu)
    _interleave(a_vpu, a_late + b_gates)
    _interleave(b_first + b_front + b_scans)
    _interleave(b_merge)

    if not has_past:
        for buf in (klo_s, khi_s, vlo_s, vhi_s):
            for i in range(ns):
                buf[i, 0:WINDOW, :] = buf[i, tq:tq + WINDOW, :]

    @pl.when(s_idx == n_steps - 1)
    def _():
        kr = kn_ref.shape[1]
        for i in range(ns):
            convn_ref[i] = cb_s[i * stride + tq:(i + 1) * stride, :]
            ssmn_ref[i] = st_s[i].T
            kn_ref[i] = knat_s[(i + 1) * tq - kr:(i + 1) * tq, :]
            vn_ref[i] = vnat_s[(i + 1) * tq - kr:(i + 1) * tq, :]


def _const_spec(shape, single_buffer=False):
    idx = (0,) * len(shape)
    if single_buffer:
        return pl.BlockSpec(shape, lambda *_: idx, pipeline_mode=pl.Buffered(1))
    return pl.BlockSpec(shape, lambda *_: idx)


def _layer_spec(stacked, layer):
    blk = (1,) + stacked.shape[1:]
    idx = (layer,) + (0,) * (stacked.ndim - 1)
    if math.prod(blk) > (1 << 20):
        return pl.BlockSpec(blk, lambda *_: idx, pipeline_mode=pl.Buffered(1))
    return pl.BlockSpec(blk, lambda *_: idx)


def _mixer_layer(x, mod, past, lw, pair_bias, e_mat, *, nc, ns):
    bsz, seq, d_model = x.shape
    t = nc * CHUNK
    tq = t // ns
    has_past = past is not None
    d_inner, n_heads = lw["d_inner"], lw["n_heads"]
    conv_ch = d_inner + 2 * SSM_GROUPS * D_STATE
    kvw = N_KV_HEADS * LANES
    kv_rows = seq if has_past else WINDOW
    assert nc % ns == 0 and bsz % ns == 0 and seq % tq == 0
    assert has_past and seq == tq or not has_past and tq >= WINDOW

    x_spec = pl.BlockSpec((ns, tq, d_model), lambda b, s: (b, s, 0))

    def per_seq(*shape):
        return pl.BlockSpec((ns,) + shape, lambda b, s: (b,) + (0,) * len(shape))

    in_specs = [x_spec, per_seq(6, d_model)]
    args = [x, mod]
    if has_past:
        in_specs += [per_seq(SUBLANES, conv_ch), per_seq(d_inner, D_STATE),
                     per_seq(WINDOW, N_KV_HEADS * HEAD_DIM), per_seq(WINDOW, N_KV_HEADS * HEAD_DIM)]
        args += list(past)
    consts = [lw["nmg"], lw["wall"], lw["convw"], lw["convb"], lw["dtb"], lw["alog"], lw["dskip"], lw["ssmg"],
              lw["qg"], lw["kg"]]
    in_specs += [_layer_spec(a, lw["layer"]) for a in consts]
    in_specs.append(pl.BlockSpec(memory_space=pltpu.SMEM))
    tail = [pair_bias, e_mat]
    in_specs += [_const_spec(a.shape, single_buffer=a.size > (1 << 18)) for a in tail]
    args += consts + [lw["sinks"]] + tail

    out_shape = (jax.ShapeDtypeStruct(x.shape, F32),
                 jax.ShapeDtypeStruct((bsz, SUBLANES, conv_ch), F32),
                 jax.ShapeDtypeStruct((bsz, d_inner, D_STATE), F32),
                 jax.ShapeDtypeStruct((bsz, kv_rows, N_KV_HEADS * HEAD_DIM), F32),
                 jax.ShapeDtypeStruct((bsz, kv_rows, N_KV_HEADS * HEAD_DIM), F32))
    out_specs = (x_spec, per_seq(SUBLANES, conv_ch), per_seq(d_inner, D_STATE),
                 per_seq(kv_rows, N_KV_HEADS * HEAD_DIM), per_seq(kv_rows, N_KV_HEADS * HEAD_DIM))
    scratch = [
        pltpu.VMEM((t, d_model), BF16),
        pltpu.VMEM((ns * (SUBLANES + tq), conv_ch), F32),
        pltpu.VMEM((t, d_inner), F32),
        pltpu.VMEM((t, d_inner), F32),
        pltpu.VMEM((t, 2 * SSM_GROUPS * D_STATE), F32),
        pltpu.VMEM((t, LANES), F32),
        pltpu.VMEM((t, d_model), BF16),
        pltpu.VMEM((ns, WINDOW + tq, kvw), BF16),
        pltpu.VMEM((ns, WINDOW + tq, kvw), BF16),
        pltpu.VMEM((ns, WINDOW + tq, kvw), BF16),
        pltpu.VMEM((ns, WINDOW + tq, kvw), BF16),
        pltpu.VMEM((t, d_inner), BF16),
        pltpu.VMEM((t, d_model), BF16),
        pltpu.VMEM((ns, D_STATE, d_inner), F32),
        pltpu.VMEM((t, d_inner), BF16),
        pltpu.VMEM((t, d_inner), BF16),
        pltpu.VMEM((t, d_inner), BF16),
        pltpu.VMEM((t, d_inner), F32),
        pltpu.VMEM((nc * SUBLANES, d_inner), F32),
        pltpu.VMEM((nc * d_inner // LANES, CHUNK, LANES), BF16),
        pltpu.VMEM((t, d_inner), F32),
        pltpu.VMEM((ns * n_heads // 2, tq, 2 * _key_width(tq)), F32),
        pltpu.VMEM((ns * n_heads // 2, tq, 2 * _key_width(tq)), BF16),
        pltpu.VMEM((ns * n_heads // 2, tq, LANES), F32),
        pltpu.VMEM((t, N_KV_HEADS * HEAD_DIM), F32),
        pltpu.VMEM((t, N_KV_HEADS * HEAD_DIM), F32),
        pltpu.VMEM((t, d_inner), F32),
        pltpu.VMEM((t, d_model), F32),
        pltpu.VMEM((2, t, d_model), F32),
    ]
    kern = functools.partial(_mixer_kernel, nc=nc, ns=ns, has_past=has_past, d_model=d_model, d_inner=d_inner,
                             n_heads=n_heads, n_ssm_heads=d_inner // SSM_HEAD_DIM, cols=lw["cols"],
                             layer=lw["layer"])
    return pl.pallas_call(
        kern, grid=(bsz // ns, seq // tq), in_specs=in_specs, out_specs=out_specs, out_shape=out_shape,
        scratch_shapes=scratch,
        compiler_params=pltpu.CompilerParams(dimension_semantics=("arbitrary", "arbitrary"),
                                             vmem_limit_bytes=VMEM_LIMIT),
        name="mixer_past" if has_past else "mixer_prompt",
    )(*args)


def _ffn_kernel(x_ref, mod_ref, g_ref, wgu_ref, wd_ref, o_ref):
    nb, tm, _ = x_ref.shape
    d_ff = wd_ref.shape[1]
    wg_ref, wu_ref, wd_ref = wgu_ref.at[0, :, 0:d_ff], wgu_ref.at[0, :, d_ff:2 * d_ff], wd_ref.at[0]
    hs = []
    for i in range(nb):
        x = x_ref[i]
        shift, scale = mod_ref[i, 3:4, :], mod_ref[i, 4:5, :]
        hs.append((x * _rms_scale(x) * g_ref[0] * (1.0 + scale) + shift).astype(BF16))
    h = hs[0] if nb == 1 else jnp.concatenate(hs, axis=0)
    gate = _dot(h, wg_ref[...])
    up = _dot(h, wu_ref[...])
    act = (_silu(gate) * up).astype(BF16)
    y = _dot(act, wd_ref[...])
    for i in range(nb):
        o_ref[i] = x_ref[i] + mod_ref[i, 5:6, :] * y[i * tm:(i + 1) * tm]


def _ffn_layer(x, mod, lw, *, nb, tm):
    bsz, seq, d_model = x.shape
    assert seq % tm == 0 and bsz % nb == 0
    xspec = pl.BlockSpec((nb, tm, d_model), lambda b, s: (b, s, 0))
    return pl.pallas_call(
        _ffn_kernel, grid=(bsz // nb, seq // tm),
        in_specs=[xspec, pl.BlockSpec((nb, 6, d_model), lambda b, s: (b, 0, 0)),
                  _layer_spec(lw["nfg"], lw["layer"]), _layer_spec(lw["wgu"], lw["layer"]),
                  _layer_spec(lw["wd"], lw["layer"])],
        out_specs=xspec, out_shape=jax.ShapeDtypeStruct(x.shape, F32),
        compiler_params=pltpu.CompilerParams(dimension_semantics=("arbitrary", "arbitrary"),
                                             vmem_limit_bytes=VMEM_LIMIT),
        name="ffn",
    )(x, mod, lw["nfg"], lw["wgu"], lw["wd"])


def _stacked_weights(p):
    d_model = p["w_out"].shape[2]
    d_inner = p["w_br_ssm"].shape[1]
    n_ssm_heads = p["dt_bias"].shape[1]
    nq = p["w_br_attn"].shape[1]
    nkv = N_KV_HEADS * HEAD_DIM
    conv_ch = d_inner + 2 * SSM_GROUPS * D_STATE
    wi = p["w_in"]
    bounds = [0, d_inner, d_inner + conv_ch, d_inner + conv_ch + n_ssm_heads]
    bounds += [bounds[-1] + nq, bounds[-1] + nq + nkv, bounds[-1] + nq + 2 * nkv]
    bounds += [bounds[-1] + d_model, bounds[-1] + 2 * d_model]
    z, xbc, dtc, q, k, v, gs, ga = [wi[:, :, a:b] for a, b in zip(bounds[:-1], bounds[1:])]
    assert nq == d_model and d_inner % d_model == 0
    pieces = [("z", z), ("xbc", xbc), ("dt", jnp.pad(dtc, ((0, 0), (0, 0), (0, LANES - n_ssm_heads)))),
              ("q", q), ("k", k), ("v", v), ("gs", gs), ("ga", ga),
              ("wba", p["w_br_attn"]), ("wo", p["w_out"])]
    pieces += [("wbs%d" % i, p["w_br_ssm"][:, i * d_model:(i + 1) * d_model]) for i in range(d_inner // d_model)]
    cols, off = {}, 0
    for name, w in pieces:
        cols[name] = off
        off += w.shape[2]
    return dict(cols=cols, wall=jnp.concatenate([w for _, w in pieces], axis=2).astype(BF16),
                wgu=p["w_gate_up"].astype(BF16), wd=p["w_down"].astype(BF16),
                d_inner=d_inner, n_heads=nq // HEAD_DIM)


def _stacked_vectors(p):
    n_ssm_heads = p["dt_bias"].shape[1]
    row = lambda a: a[:, None, :]
    rep = lambda a: row(jnp.repeat(a, SSM_HEAD_DIM, axis=1))
    padl = lambda a: row(jnp.pad(a, ((0, 0), (0, LANES - n_ssm_heads))))
    return dict(
        nmg=row(p["norm_mix_g"]), nfg=row(p["norm_ffn_g"]), convw=p["conv_w"], convb=row(p["conv_b"]),
        dtb=padl(p["dt_bias"]), alog=padl(p["a_log"]), dskip=rep(p["d_skip"]), ssmg=row(p["ssm_norm_g"]),
        qg=row(jnp.tile(p["q_norm_g"] * (HEAD_DIM ** -0.5 * LOG2E), (1, 2))), kg=row(jnp.tile(p["k_norm_g"], (1, 2))),
        sinks=p["sinks"],
    )


def _expand_matrix(n_ssm_heads):
    assert 3 * n_ssm_heads <= LANES
    r = jnp.arange(LANES)[:, None]
    src = (r < 2 * n_ssm_heads) | (r >= LANES - n_ssm_heads)
    c = jnp.arange(n_ssm_heads * SSM_HEAD_DIM)[None, :] // SSM_HEAD_DIM
    return (src & ((r % n_ssm_heads) == c)).astype(BF16)


def kernel(x_prompt, x_sample, cache_k, cache_v, state_conv, state_ssm, c_prompt, c_sample, rel_bias, ada_w, ada_b, norm_mix_g, norm_ffn_g, w_in, conv_w, conv_b, dt_bias, a_log, d_skip, ssm_norm_g, q_norm_g, k_norm_g, sinks, w_br_ssm, w_br_attn, w_out, w_gate_up, w_down):
    p = dict(norm_mix_g=norm_mix_g, norm_ffn_g=norm_ffn_g, w_in=w_in, conv_w=conv_w, conv_b=conv_b,
             dt_bias=dt_bias, a_log=a_log, d_skip=d_skip, ssm_norm_g=ssm_norm_g, q_norm_g=q_norm_g,
             k_norm_g=k_norm_g, sinks=sinks, w_br_ssm=w_br_ssm, w_br_attn=w_br_attn, w_out=w_out,
             w_gate_up=w_gate_up, w_down=w_down)
    depth = w_in.shape[0]
    bp, seq_p, d_model = x_prompt.shape
    bs, seq_s, _ = x_sample.shape
    n_ssm_heads, ssm_p, d_state = state_ssm.shape[2:]
    d_inner = n_ssm_heads * ssm_p
    nkv = N_KV_HEADS * HEAD_DIM

    mod = _ada_mod(jnp.concatenate([c_prompt, c_sample], axis=0), ada_w, ada_b)
    mod = mod.reshape(depth, bp + bs, 6, d_model)
    assert MIXER_CHUNKS * CHUNK >= WINDOW
    bias_p = jnp.concatenate([_pair_bias(rel_bias, MIXER_CHUNKS, True), _pair_bias(rel_bias, MIXER_CHUNKS, False)])
    bias_s = _pair_bias(rel_bias, 1, True)
    e_mat = _expand_matrix(n_ssm_heads)

    xp, xs = x_prompt, x_sample
    outs = [[] for _ in range(8)]
    stacked = dict(_stacked_weights(p), **_stacked_vectors(p))
    for l in range(depth):
        lw = dict(stacked, layer=l)
        mod_p, mod_s = mod[l, :bp], mod[l, bp:]
        past = (jnp.pad(state_conv[l], ((0, 0), (SUBLANES - (CONV_K - 1), 0), (0, 0))),
                state_ssm[l].reshape(bs, d_inner, d_state),
                cache_k[l].reshape(bs, WINDOW, nkv), cache_v[l].reshape(bs, WINDOW, nkv))
        xp, cvp, ssp, kp, vp = _mixer_layer(xp, mod_p, None, lw, bias_p, e_mat, nc=MIXER_CHUNKS, ns=1)
        ns_s = math.gcd(bs, MIXER_CHUNKS)
        xs, cvs, sss, ksn, vsn = _mixer_layer(xs, mod_s, past, lw, bias_s, e_mat, nc=ns_s, ns=ns_s)
        xp = _ffn_layer(xp, mod_p, lw, nb=1, tm=min(FFN_ROWS, seq_p))
        nb_s = math.gcd(bs, max(1, FFN_ROWS // seq_s))
        xs = _ffn_layer(xs, mod_s, lw, nb=nb_s, tm=seq_s)
        tail = SUBLANES - (CONV_K - 1)
        for lst, val in zip(outs, (
                cvp[:, tail:], cvs[:, tail:],
                ssp.reshape(bp, n_ssm_heads, ssm_p, d_state), sss.reshape(bs, n_ssm_heads, ssm_p, d_state),
                kp.reshape(bp, WINDOW, N_KV_HEADS, HEAD_DIM), ksn.reshape(bs, seq_s, N_KV_HEADS, HEAD_DIM),
                vp.reshape(bp, WINDOW, N_KV_HEADS, HEAD_DIM), vsn.reshape(bs, seq_s, N_KV_HEADS, HEAD_DIM))):
            lst.append(val)
    return (xp, xs) + tuple(jnp.stack(o) for o in outs)
```
